```python
import jax
import jax.numpy as jnp
from jax import lax
import numpy as np

D_MODEL = 1024
BATCH = 16
SEQ = 4096
DEPTH = 4

CTX_LEN = 256
GRID_W = 64
D_A = D_MODEL // 4
H_A = 4
DH_A = D_A // H_A
CHUNK = 128
D_B = D_MODEL // 4
H_B = 4
DH_B = D_B // H_B
CONV_W = 4
LRU_C = 8.0
H_C = 8
D_V = D_MODEL // (2 * H_C)
D_NOPE = D_V
D_ROPE = D_V // 2
D_QK = D_NOPE + D_ROPE
Q_LORA = 3 * D_MODEL // 8
KV_LORA = D_MODEL // 4
Q_BLOCK = 128
ROPE_BASE = 10000.0
D_C = H_C * D_V
D_MIX = D_A + D_B + D_C
D_IN = 2 * D_A + 2 * D_B + Q_LORA + KV_LORA + D_ROPE
D_FF = ((8 * D_MODEL // 3 + 127) // 128) * 128
N_EXPERTS = 8
TOP_K = 2
D_FF_EXPERT = 7 * D_MODEL // 2
MOE_BLOCK = 128
EPS = 1e-6
N_DENSE = (DEPTH + 1) // 2
N_MOE = DEPTH // 2

kernel_name = "hybrid_sgu_rglru_mla_moe_dit"


def _rmsnorm(x, g):
    xf = x.astype(jnp.float32)
    y = xf * lax.rsqrt(jnp.mean(xf * xf, axis=-1, keepdims=True) + EPS)
    return (y * g.astype(jnp.float32)).astype(x.dtype)


def _layernorm(x, g, b):
    xf = x.astype(jnp.float32)
    mu = jnp.mean(xf, axis=-1, keepdims=True)
    xc = xf - mu
    var = jnp.mean(xc * xc, axis=-1, keepdims=True)
    return (xc * lax.rsqrt(var + EPS) * g.astype(jnp.float32) + b.astype(jnp.float32)).astype(x.dtype)


def _modulation(cond, w_mod, b_mod):
    return jnp.split(jax.nn.silu(cond) @ w_mod + b_mod, 6, axis=-1)


def _modulate(xn, shift, scale):
    return xn * (1.0 + scale) + shift


def _chunk_sgu(z, ln_g, ln_b, w_s, b_s):
    bsz, L, _ = z.shape
    u, v = jnp.split(jax.nn.gelu(z), 2, axis=-1)
    v = _layernorm(v, ln_g, ln_b).reshape(bsz, L // CHUNK, CHUNK, H_A, DH_A)
    s = jnp.einsum('hpq,bnqhd->bnphd', w_s, v) + b_s.T[:, :, None]
    return u * s.reshape(bsz, L, D_A)


def _dwconv(x, w, b):
    c = x.shape[-1]
    y = lax.conv_general_dilated(
        x, w[:, None, :].astype(x.dtype), window_strides=(1,),
        padding=[(CONV_W // 2, CONV_W - 1 - CONV_W // 2)],
        dimension_numbers=('NWC', 'WIO', 'NWC'), feature_group_count=c)
    return y + b


def _rglru_coeffs(x, w_a, b_a, w_x, b_x, lam):
    bsz, L, _ = x.shape
    xh = x.reshape(bsz, L, H_B, DH_B)
    r = jax.nn.sigmoid(jnp.einsum('blhi,hij->blhj', xh, w_a).reshape(bsz, L, D_B) + b_a)
    i = jax.nn.sigmoid(jnp.einsum('blhi,hij->blhj', xh, w_x).reshape(bsz, L, D_B) + b_x)
    log_a = (-LRU_C * r.astype(jnp.float32)) * jax.nn.softplus(-lam.astype(jnp.float32))
    a = jnp.exp(log_a)
    bval = jnp.sqrt(-jnp.expm1(2.0 * log_a)) * (i * x).astype(jnp.float32)
    return a, bval


def _linear_scan(a, b, h0, reverse):
    def combine(lhs, rhs):
        return lhs[0] * rhs[0], rhs[0] * lhs[1] + rhs[1]
    a_cum, b_cum = lax.associative_scan(combine, (a, b), reverse=reverse, axis=1)
    return b_cum + a_cum * h0[:, None, :]


def _rotate(x, cos, sin):
    x1, x2 = jnp.split(x, 2, axis=-1)
    cos = cos[:, None, :]
    sin = sin[:, None, :]
    return jnp.concatenate([x1 * cos - x2 * sin, x1 * sin + x2 * cos], axis=-1)


def _rope_2d(x, rope):
    cos_r, sin_r, cos_c, sin_c = rope
    x_row, x_col = jnp.split(x.astype(jnp.float32), 2, axis=-1)
    y = jnp.concatenate([_rotate(x_row, cos_r, sin_r), _rotate(x_col, cos_c, sin_c)], axis=-1)
    return y.astype(x.dtype)


def _mla_queries(zm, q_norm, w_uq):
    bsz, L, _ = zm.shape
    cq = _rmsnorm(zm[..., :Q_LORA], q_norm)
    return (cq @ w_uq).reshape(bsz, L, H_C, D_QK)


def _mla_keys_values(zm, kv_norm, w_uk, w_uv, rope):
    bsz, L, _ = zm.shape
    ckv = _rmsnorm(zm[..., Q_LORA:Q_LORA + KV_LORA], kv_norm)
    k_rope = zm[..., Q_LORA + KV_LORA:][:, :, None, :]
    if rope is not None:
        k_rope = _rope_2d(k_rope, rope)
    k_nope = (ckv @ w_uk).reshape(bsz, L, H_C, D_NOPE)
    v = (ckv @ w_uv).reshape(bsz, L, H_C, D_V)
    k = jnp.concatenate([k_nope, jnp.broadcast_to(k_rope, (bsz, L, H_C, D_ROPE))], axis=-1)
    return k, v


def _attend(q, k, v):
    s = jnp.einsum('bqhd,bkhd->bhqk', q, k).astype(jnp.float32) * (D_QK ** -0.5)
    p = jax.nn.softmax(s, axis=-1).astype(v.dtype)
    return jnp.einsum('bhqk,bkhd->bqhd', p, v)


def _latent_attention(q, k, v):
    bsz, L = q.shape[:2]
    qb = q.reshape(bsz, L // Q_BLOCK, Q_BLOCK, H_C, D_QK).transpose(1, 0, 2, 3, 4)
    o = lax.map(lambda qi: _attend(qi, k, v), qb)
    return o.transpose(1, 0, 2, 3, 4).reshape(bsz, L, D_C)


def _mixer(h, hc, rope, last, w_in, w_out, sgu_ln_g, sgu_ln_b, sgu_w, sgu_b,
           conv_w, conv_b, lru_w_a, lru_b_a, lru_w_x, lru_b_x, lru_lam,
           q_norm, w_uq, kv_norm, w_uk, w_uv):
    bsz, L, _ = h.shape
    cuts = [2 * D_A, 2 * D_A + 2 * D_B]
    za, zb, zm = jnp.split(h @ w_in, cuts, axis=-1)
    za_c, zb_c, zm_c = jnp.split(hc @ w_in, cuts, axis=-1)

    y_a = _chunk_sgu(za, sgu_ln_g, sgu_ln_b, sgu_w, sgu_b)

    g_b, x_b = jnp.split(zb, 2, axis=-1)
    g_bc, x_bc = jnp.split(zb_c, 2, axis=-1)
    x_b = _dwconv(x_b, conv_w, conv_b)
    x_bc = _dwconv(x_bc, conv_w, conv_b)
    h_lat = jnp.zeros((bsz, L, D_B), jnp.float32)
    ctx_states = []
    for d, rev in enumerate((False, True)):
        gates = (lru_w_a[d], lru_b_a[d], lru_w_x[d], lru_b_x[d], lru_lam[d])
        a_c, b_c = _rglru_coeffs(x_bc, *gates)
        hs_c = _linear_scan(a_c, b_c, jnp.zeros((bsz, D_B), jnp.float32), rev)
        h_end = hs_c[:, 0] if rev else hs_c[:, -1]
        a_l, b_l = _rglru_coeffs(x_b, *gates)
        h_lat = h_lat + _linear_scan(a_l, b_l, h_end, rev)
        ctx_states.append(hs_c)
    y_b = jax.nn.gelu(g_b) * h_lat.astype(h.dtype)

    q = _mla_queries(zm, q_norm, w_uq)
    q = jnp.concatenate([q[..., :D_NOPE], _rope_2d(q[..., D_NOPE:], rope)], axis=-1)
    k, v = _mla_keys_values(zm, kv_norm, w_uk, w_uv, rope)
    k_c, v_c = _mla_keys_values(zm_c, kv_norm, w_uk, w_uv, None)
    y_c = _latent_attention(q, jnp.concatenate([k, k_c], axis=1), jnp.concatenate([v, v_c], axis=1))

    y = jnp.concatenate([y_a, y_b, y_c], axis=-1) @ w_out
    if last:
        return y, None

    lc = hc.shape[1]
    y_ac = _chunk_sgu(za_c, sgu_ln_g, sgu_ln_b, sgu_w, sgu_b)
    y_bc = jax.nn.gelu(g_bc) * (ctx_states[0] + ctx_states[1]).astype(hc.dtype)
    q_c = _mla_queries(zm_c, q_norm, w_uq)
    y_cc = _attend(q_c, k_c, v_c).reshape(bsz, lc, D_C)
    y_ctx = jnp.concatenate([y_ac, y_bc, y_cc], axis=-1) @ w_out
    return y, y_ctx


def _swiglu(h, w1, w3, w2):
    return (jax.nn.silu(h @ w1) * (h @ w3)) @ w2


def _moe_swiglu(h, w_router, w1, w3, w2):
    n, d = h.shape
    logits = h.astype(jnp.float32) @ w_router.astype(jnp.float32)
    top_v, top_e = lax.top_k(logits, TOP_K)
    gates = jax.nn.softmax(top_v, axis=-1)
    n_assign = n * TOP_K
    e_flat = top_e.reshape(-1).astype(jnp.int32)
    tok_flat = jnp.repeat(jnp.arange(n, dtype=jnp.int32), TOP_K)
    g_flat = gates.reshape(-1)
    order = jnp.argsort(e_flat)
    e_sorted = e_flat[order]
    counts = jnp.bincount(e_flat, length=N_EXPERTS).astype(jnp.int32)
    starts = jnp.cumsum(counts) - counts
    padded = (counts + MOE_BLOCK - 1) // MOE_BLOCK * MOE_BLOCK
    pad_ends = jnp.cumsum(padded)
    pad_starts = pad_ends - padded
    dest = pad_starts[e_sorted] + jnp.arange(n_assign, dtype=jnp.int32) - starts[e_sorted]
    n_pad = ((n_assign + MOE_BLOCK - 1) // MOE_BLOCK + N_EXPERTS) * MOE_BLOCK
    n_blk = n_pad // MOE_BLOCK
    tok_pad = jnp.full((n_pad,), n, jnp.int32).at[dest].set(tok_flat[order])
    gate_pad = jnp.zeros((n_pad,), jnp.float32).at[dest].set(g_flat[order])
    blk_e = jnp.minimum(jnp.searchsorted(pad_ends, jnp.arange(n_blk, dtype=jnp.int32) * MOE_BLOCK, side='right'),
                        N_EXPERTS - 1)
    h_ext = jnp.concatenate([h, jnp.zeros((1, d), h.dtype)], axis=0)
    h_blk = h_ext[tok_pad].reshape(n_blk, MOE_BLOCK, d)

    def expert_group(args):
        hb, e = args
        return (jax.nn.silu(hb @ w1[e]) * (hb @ w3[e])) @ w2[e]

    y_pad = lax.map(expert_group, (h_blk, blk_e)).reshape(n_pad, d)
    y = jnp.zeros((n + 1, d), h.dtype).at[tok_pad].add((y_pad * gate_pad[:, None]).astype(h.dtype))
    return y[:n]


def setup_inputs(seed: int = 0) -> dict:
    key = jax.random.key(seed)
    ks = iter(jax.random.split(key, 40))
    f32 = jnp.float32

    def nrm(shape, fan_in, scale=1.0):
        return jax.random.normal(next(ks), shape, f32) * (scale * fan_in ** -0.5)

    def gain(shape):
        return 1.0 + 0.05 * jax.random.normal(next(ks), shape, f32)

    def bias(shape, s=0.02):
        return s * jax.random.normal(next(ks), shape, f32)

    u = jax.random.uniform(next(ks), (DEPTH, 2, D_B), f32, 0.9, 0.999)
    a_base = u ** (1.0 / LRU_C)
    lam = jnp.log(a_base) - jnp.log1p(-a_base)

    return {
        "x": jax.random.normal(next(ks), (BATCH, SEQ, D_MODEL), f32),
        "c": jax.random.normal(next(ks), (BATCH, D_MODEL), f32),
        "ctx": jax.random.normal(next(ks), (BATCH, CTX_LEN, D_MODEL), f32),
        "c_ctx": jax.random.normal(next(ks), (D_MODEL,), f32),
        "w_mod": nrm((DEPTH, D_MODEL, 6 * D_MODEL), D_MODEL, 0.3),
        "b_mod": bias((DEPTH, 6 * D_MODEL)),
        "norm1_g": gain((DEPTH, D_MODEL)),
        "norm2_g": gain((DEPTH, D_MODEL)),
        "w_in": nrm((DEPTH, D_MODEL, D_IN), D_MODEL),
        "w_out": nrm((DEPTH, D_MIX, D_MODEL), D_MIX),
        "sgu_ln_g": gain((DEPTH, D_A)),
        "sgu_ln_b": bias((DEPTH, D_A)),
        "sgu_w": nrm((DEPTH, H_A, CHUNK, CHUNK), CHUNK, 0.5),
        "sgu_b": gain((DEPTH, H_A, CHUNK)),
        "conv_w": nrm((DEPTH, CONV_W, D_B), CONV_W),
        "conv_b": bias((DEPTH, D_B)),
        "lru_w_a": nrm((DEPTH, 2, H_B, DH_B, DH_B), DH_B),
        "lru_b_a": bias((DEPTH, 2, D_B)),
        "lru_w_x": nrm((DEPTH, 2, H_B, DH_B, DH_B), DH_B),
        "lru_b_x": bias((DEPTH, 2, D_B)),
        "lru_lam": lam,
        "mla_q_norm": gain((DEPTH, Q_LORA)),
        "mla_w_uq": nrm((DEPTH, Q_LORA, H_C * D_QK), Q_LORA),
        "mla_kv_norm": gain((DEPTH, KV_LORA)),
        "mla_w_uk": nrm((DEPTH, KV_LORA, H_C * D_NOPE), KV_LORA),
        "mla_w_uv": nrm((DEPTH, KV_LORA, H_C * D_V), KV_LORA),
        "ffn_w1": nrm((N_DENSE, D_MODEL, D_FF), D_MODEL),
        "ffn_w3": nrm((N_DENSE, D_MODEL, D_FF), D_MODEL),
        "ffn_w2": nrm((N_DENSE, D_FF, D_MODEL), D_FF),
        "moe_router": nrm((N_MOE, D_MODEL, N_EXPERTS), D_MODEL),
        "moe_w1": nrm((N_MOE, N_EXPERTS, D_MODEL, D_FF_EXPERT), D_MODEL),
        "moe_w3": nrm((N_MOE, N_EXPERTS, D_MODEL, D_FF_EXPERT), D_MODEL),
        "moe_w2": nrm((N_MOE, N_EXPERTS, D_FF_EXPERT, D_MODEL), D_FF_EXPERT),
        "final_norm_g": gain((D_MODEL,)),
    }


def reference(x, c, ctx, c_ctx, w_mod, b_mod, norm1_g, norm2_g, w_in, w_out,
              sgu_ln_g, sgu_ln_b, sgu_w, sgu_b, conv_w, conv_b,
              lru_w_a, lru_b_a, lru_w_x, lru_b_x, lru_lam,
              mla_q_norm, mla_w_uq, mla_kv_norm, mla_w_uk, mla_w_uv,
              ffn_w1, ffn_w3, ffn_w2, moe_router, moe_w1, moe_w3, moe_w2,
              final_norm_g):
    bsz, L, d = x.shape
    lc = ctx.shape[1]
    rows = L // GRID_W
    row = jnp.repeat(jnp.arange(rows, dtype=jnp.float32), GRID_W)
    col = jnp.tile(jnp.arange(GRID_W, dtype=jnp.float32), rows)
    inv_freq = ROPE_BASE ** (-jnp.arange(0, D_ROPE // 2, 2, dtype=jnp.float32) / (D_ROPE // 2))
    ang_r = row[:, None] * inv_freq
    ang_c = col[:, None] * inv_freq
    rope = (jnp.cos(ang_r), jnp.sin(ang_r), jnp.cos(ang_c), jnp.sin(ang_c))

    xc = ctx
    for l in range(DEPTH):
        last = l == DEPTH - 1
        sh1, sc1, g1, sh2, sc2, g2 = _modulation(c, w_mod[l], b_mod[l])
        sh1c, sc1c, g1c, sh2c, sc2c, g2c = _modulation(c_ctx, w_mod[l], b_mod[l])

        h = _modulate(_rmsnorm(x, norm1_g[l]), sh1[:, None], sc1[:, None])
        hc = _modulate(_rmsnorm(xc, norm1_g[l]), sh1c, sc1c)
        y, y_ctx = _mixer(h, hc, rope, last, w_in[l], w_out[l],
                          sgu_ln_g[l], sgu_ln_b[l], sgu_w[l], sgu_b[l],
                          conv_w[l], conv_b[l], lru_w_a[l], lru_b_a[l], lru_w_x[l], lru_b_x[l], lru_lam[l],
                          mla_q_norm[l], mla_w_uq[l], mla_kv_norm[l], mla_w_uk[l], mla_w_uv[l])
        x = x + g1[:, None] * y
        if not last:
            xc = xc + g1c * y_ctx

        h = _modulate(_rmsnorm(x, norm2_g[l]), sh2[:, None], sc2[:, None])
        if l % 2 == 0:
            i = l // 2
            x = x + g2[:, None] * _swiglu(h, ffn_w1[i], ffn_w3[i], ffn_w2[i])
            if not last:
                hc = _modulate(_rmsnorm(xc, norm2_g[l]), sh2c, sc2c)
                xc = xc + g2c * _swiglu(hc, ffn_w1[i], ffn_w3[i], ffn_w2[i])
        else:
            i = l // 2
            tok = h.reshape(bsz * L, d)
            if not last:
                hc = _modulate(_rmsnorm(xc, norm2_g[l]), sh2c, sc2c)
                tok = jnp.concatenate([tok, hc.reshape(bsz * lc, d)], axis=0)
            f = _moe_swiglu(tok, moe_router[i], moe_w1[i], moe_w3[i], moe_w2[i])
            x = x + g2[:, None] * f[:bsz * L].reshape(bsz, L, d)
            if not last:
                xc = xc + g2c * f[bsz * L:].reshape(bsz, lc, d)

    return _rmsnorm(x, final_norm_g)
```

```python
import functools

import jax
import jax.numpy as jnp
from jax import lax
from jax.experimental import pallas as pl
from jax.experimental.pallas import tpu as pltpu

F32 = jnp.float32
BF16 = jnp.bfloat16

EPS = 1e-6
GRID_W = 64
ROPE_BASE = 10000.0
LRU_C = 8.0
H_C = 8
TOP_K = 2
LANES = 128
TM = 256
T_SCAN = 256
TMB = 1024
TF = 512
VMEM_LIMIT = 56 * 1024 * 1024


def _cparams(sem):
    return pltpu.CompilerParams(dimension_semantics=sem, vmem_limit_bytes=VMEM_LIMIT)


def _rms(x, g):
    return x * lax.rsqrt(jnp.mean(x * x, axis=-1, keepdims=True) + EPS) * g


def _mod_kernel(c_ref, w_ref, b_ref, o_ref):
    c = c_ref[...]
    a = (c * jax.nn.sigmoid(c)).astype(BF16)
    o_ref[0] = jnp.dot(a, w_ref[0], preferred_element_type=F32) + b_ref[0]


def _modulation(cond, w_mod, b_mod):
    depth, d, n = w_mod.shape
    r = cond.shape[0]
    tn = 1024
    return pl.pallas_call(
        _mod_kernel,
        grid=(depth, n // tn),
        in_specs=[pl.BlockSpec((r, d), lambda l, j: (0, 0)),
                  pl.BlockSpec((1, d, tn), lambda l, j: (l, 0, j)),
                  pl.BlockSpec((1, 1, tn), lambda l, j: (l, 0, j))],
        out_specs=pl.BlockSpec((1, r, tn), lambda l, j: (l, 0, j)),
        out_shape=jax.ShapeDtypeStruct((depth, r, n), F32),
        compiler_params=_cparams(("arbitrary", "arbitrary")),
        name="modulation",
    )(cond, w_mod, b_mod)


def _premix_kernel(x_ref, mod_ref, g_ref, win_ref, lng_ref, lnb_ref, ws_ref, bs_ref,
                   qn_ref, wuq_ref, wuqp_ref, kvn_ref, wuk_ref, e_ref, wuv_ref,
                   cq_ref, sq_ref, ck_ref, sk_ref,
                   ya_ref, gg_ref, xb_ref, q_ref, k_ref, v_ref, *, dims):
    d, d_a, d_b, q_lora, kv_lora, d_rope, h_a, chunk = dims
    x = x_ref[0]
    shift = mod_ref[0, :, 0:d]
    scale = mod_ref[0, :, d:2 * d]
    h = (_rms(x, g_ref[...]) * (1.0 + scale) + shift).astype(BF16)
    z = jnp.dot(h, win_ref[...], preferred_element_type=F32)

    o = 0
    u = jax.nn.gelu(z[:, o:o + d_a])
    v = jax.nn.gelu(z[:, o + d_a:o + 2 * d_a])
    mu = jnp.mean(v, axis=-1, keepdims=True)
    vc = v - mu
    var = jnp.mean(vc * vc, axis=-1, keepdims=True)
    vn = (vc * lax.rsqrt(var + EPS) * lng_ref[...] + lnb_ref[...]).astype(BF16)
    dh_a = d_a // h_a
    tm = x.shape[0]
    head_of_lane = lax.broadcasted_iota(jnp.int32, (chunk, d_a), 1) // dh_a
    for c in range(tm // chunk):
        vch = vn[c * chunk:(c + 1) * chunk]
        s = jnp.dot(ws_ref[0], vch, preferred_element_type=F32)
        for hd in range(1, h_a):
            s = jnp.where(head_of_lane == hd,
                          jnp.dot(ws_ref[hd], vch, preferred_element_type=F32), s)
        s = s + bs_ref[...]
        ya_ref[0, c * chunk:(c + 1) * chunk, :] = u[c * chunk:(c + 1) * chunk] * s

    o = 2 * d_a
    gg_ref[0] = jax.nn.gelu(z[:, o:o + d_b])
    xb_ref[0] = z[:, o + d_b:o + 2 * d_b]

    o = 2 * d_a + 2 * d_b
    cq = _rms(z[:, o:o + q_lora], qn_ref[...]).astype(BF16)
    qa = jnp.dot(cq, wuq_ref[...], preferred_element_type=F32)
    qb = jnp.dot(cq, wuqp_ref[...], preferred_element_type=F32)
    cos_q = jnp.concatenate([cq_ref[...]] * H_C, axis=1)
    sin_q = jnp.concatenate([sq_ref[...]] * H_C, axis=1)
    q_ref[0] = (qa * cos_q + qb * sin_q).astype(BF16)
    o += q_lora
    ckv = _rms(z[:, o:o + kv_lora], kvn_ref[...]).astype(BF16)
    o += kv_lora
    zr = z[:, o:o + d_rope]
    zrp = z[:, o + d_rope:o + 2 * d_rope]
    kr = (zr * ck_ref[...] + zrp * sk_ref[...]).astype(BF16)
    kn = jnp.dot(ckv, wuk_ref[...], preferred_element_type=F32)
    k_ref[0] = (kn + jnp.dot(kr, e_ref[...], preferred_element_type=F32)).astype(BF16)
    v_ref[0] = jnp.dot(ckv, wuv_ref[...], preferred_element_type=F32).astype(BF16)


def _tile_mod_index(n_lat_tiles, n_batch):
    def index(b, i):
        return (jnp.where(i < n_lat_tiles, b, n_batch), 0, 0)
    return index


def _premix(xt, mods, g1, win, lng, lnb, ws, bs, qn, wuq, wuqp, kvn, wuk, emat, wuv,
            cosq, sinq, cosk, sink, *, n_lat_tiles, dims):
    bsz, s, d = xt.shape
    d_a, d_b = dims[1], dims[2]
    d_c = wuv.shape[1]
    hp = H_C * LANES
    const2 = lambda b, i: (0, 0)
    const3 = lambda b, i: (0, 0, 0)
    tile = lambda b, i: (b, i, 0)
    full = lambda a: pl.BlockSpec(a.shape, const2 if a.ndim == 2 else const3)
    return pl.pallas_call(
        functools.partial(_premix_kernel, dims=dims),
        grid=(bsz, s // TM),
        in_specs=[pl.BlockSpec((1, TM, d), tile),
                  pl.BlockSpec((1, 1, mods.shape[-1]), _tile_mod_index(n_lat_tiles, bsz)),
                  full(g1), full(win), full(lng), full(lnb), full(ws), full(bs),
                  full(qn), full(wuq), full(wuqp), full(kvn), full(wuk), full(emat), full(wuv),
                  pl.BlockSpec((TM, LANES), lambda b, i: (i, 0)),
                  pl.BlockSpec((TM, LANES), lambda b, i: (i, 0)),
                  pl.BlockSpec((TM, cosk.shape[1]), lambda b, i: (i, 0)),
                  pl.BlockSpec((TM, sink.shape[1]), lambda b, i: (i, 0))],
        out_specs=[pl.BlockSpec((1, TM, d_a), tile), pl.BlockSpec((1, TM, d_b), tile),
                   pl.BlockSpec((1, TM, d_b), tile), pl.BlockSpec((1, TM, hp), tile),
                   pl.BlockSpec((1, TM, hp), tile), pl.BlockSpec((1, TM, d_c), tile)],
        out_shape=[jax.ShapeDtypeStruct((bsz, s, d_a), F32),
                   jax.ShapeDtypeStruct((bsz, s, d_b), F32),
                   jax.ShapeDtypeStruct((bsz, s, d_b), F32),
                   jax.ShapeDtypeStruct((bsz, s, hp), BF16),
                   jax.ShapeDtypeStruct((bsz, s, hp), BF16),
                   jax.ShapeDtypeStruct((bsz, s, d_c), BF16)],
        compiler_params=_cparams(("arbitrary", "arbitrary")),
        name="premix",
    )(xt, mods, g1, win, lng, lnb, ws, bs, qn, wuq, wuqp, kvn, wuk, emat, wuv,
      cosq, sinq, cosk, sink)


def _scan_tile(a, b, carry, reverse):
    t = a.shape[0]
    rows = lax.broadcasted_iota(jnp.int32, a.shape, 0)
    s = 1
    while s < t:
        if reverse:
            a_sh = pltpu.roll(a, t - s, 0)
            b_sh = pltpu.roll(b, t - s, 0)
            ok = rows < t - s
        else:
            a_sh = pltpu.roll(a, s, 0)
            b_sh = pltpu.roll(b, s, 0)
            ok = rows >= s
        b = b + a * jnp.where(ok, b_sh, 0.0)
        a = a * jnp.where(ok, a_sh, 1.0)
        s *= 2
    h = b + a * carry
    return h, (h[0:1] if reverse else h[t - 1:t])


def _lru_kernel(xb_ref, cw_ref, cb_ref, wg_ref, bg_ref, lam_ref, out_ref, xc_ref, *, n_lat, n_ctx):
    t = T_SCAN
    n_tiles = n_lat + n_ctx
    s_total = n_tiles * t
    d_b = xb_ref.shape[-1]
    w = cw_ref[...]
    cb = cb_ref[...]

    def conv_body(j, _):
        t0 = pl.multiple_of(j * t, t)
        is_ctx = j >= n_lat
        seq_lo = jnp.where(is_ctx, n_lat * t, 0)
        seq_hi = jnp.where(is_ctx, s_total, n_lat * t)
        cur = xb_ref[0, pl.ds(t0, t), :]
        p0 = pl.multiple_of(jnp.maximum(t0 - 8, 0), 8)
        n0 = pl.multiple_of(jnp.minimum(t0 + t, s_total - 8), 8)
        prev = jnp.where(t0 > seq_lo, xb_ref[0, pl.ds(p0, 8), :], 0.0)
        nxt = jnp.where(t0 + t < seq_hi, xb_ref[0, pl.ds(n0, 8), :], 0.0)
        ext = jnp.concatenate([prev, cur, nxt], axis=0)
        n_ext = t + 16
        xm2 = pltpu.roll(ext, 2, 0)[8:8 + t]
        xm1 = pltpu.roll(ext, 1, 0)[8:8 + t]
        xp1 = pltpu.roll(ext, n_ext - 1, 0)[8:8 + t]
        xc_ref[pl.ds(t0, t), :] = (w[0:1] * xm2 + w[1:2] * xm1 + w[2:3] * cur + w[3:4] * xp1 + cb)
        return 0

    lax.fori_loop(0, n_tiles, conv_body, 0)

    lam = lam_ref[...]
    neg = -lam
    softplus = jnp.maximum(neg, 0.0) + jnp.log1p(jnp.exp(-jnp.abs(neg)))

    def direction(dr, reverse):
        sp = softplus[dr:dr + 1]
        wg = wg_ref[:, dr * 2 * d_b:(dr + 1) * 2 * d_b]
        bg = bg_ref[:, dr * 2 * d_b:(dr + 1) * 2 * d_b]

        def body(j, carry):
            if reverse:
                idx = jnp.where(j < n_ctx, n_tiles - 1 - j, n_lat - 1 - (j - n_ctx))
            else:
                idx = jnp.where(j < n_ctx, n_lat + j, j - n_ctx)
            t0 = pl.multiple_of(idx * t, t)
            xc = xc_ref[pl.ds(t0, t), :]
            g = jnp.dot(xc.astype(BF16), wg, preferred_element_type=F32) + bg
            r = jax.nn.sigmoid(g[:, 0:d_b])
            ig = jax.nn.sigmoid(g[:, d_b:2 * d_b])
            log_a = (-LRU_C * r) * sp
            a = jnp.exp(log_a)
            bv = jnp.sqrt(-jnp.tanh(log_a) * (a * a + 1.0)) * (ig * xc)
            h, carry = _scan_tile(a, bv, carry, reverse)
            if reverse:
                out_ref[0, pl.ds(t0, t), :] = out_ref[0, pl.ds(t0, t), :] + h
            else:
                out_ref[0, pl.ds(t0, t), :] = h
            return carry

        lax.fori_loop(0, n_tiles, body, jnp.zeros((1, d_b), F32))

    direction(0, False)
    direction(1, True)


def _lru(xb, cw, cb, wg, bg, lam, *, n_lat, n_ctx):
    bsz, s, d_b = xb.shape
    const2 = lambda b: (0, 0)
    full = lambda a: pl.BlockSpec(a.shape, const2)
    return pl.pallas_call(
        functools.partial(_lru_kernel, n_lat=n_lat, n_ctx=n_ctx),
        grid=(bsz,),
        in_specs=[pl.BlockSpec((1, s, d_b), lambda b: (b, 0, 0)),
                  full(cw), full(cb), full(wg), full(bg), full(lam)],
        out_specs=pl.BlockSpec((1, s, d_b), lambda b: (b, 0, 0)),
        out_shape=jax.ShapeDtypeStruct((bsz, s, d_b), F32),
        scratch_shapes=[pltpu.VMEM((s, d_b), F32)],
        compiler_params=_cparams(("arbitrary",)),
        name="rglru",
    )(xb, cw, cb, wg, bg, lam)


def _attn_kernel(q_ref, k_ref, v_ref, o_ref, *, n_lat_tiles, l_lat, s_total, scale, d_v):
    i = pl.program_id(2)
    tq = q_ref.shape[1]

    def compute(k_lo, k_len):
        outs = []
        for hh in range(2):
            qh = q_ref[0, :, hh * LANES:(hh + 1) * LANES]
            kh = k_ref[0, k_lo:k_lo + k_len, hh * LANES:(hh + 1) * LANES]
            s = lax.dot_general(qh, kh, (((1,), (1,)), ((), ())),
                                preferred_element_type=F32) * scale
            m = jnp.max(s, axis=-1, keepdims=True)
            p = jnp.exp(s - m)
            den = jnp.sum(p, axis=-1, keepdims=True)
            pv = jnp.dot(p.astype(BF16), v_ref[0, k_lo:k_lo + k_len, :],
                         preferred_element_type=F32)
            outs.append(pv / den)
        lane = lax.broadcasted_iota(jnp.int32, (tq, LANES), 1)
        o_ref[0] = jnp.where(lane < d_v, outs[0], outs[1]).astype(o_ref.dtype)

    @pl.when(i < n_lat_tiles)
    def _():
        compute(0, s_total)

    @pl.when(i >= n_lat_tiles)
    def _():
        compute(l_lat, s_total - l_lat)


def _attention(q, k, v, *, l_lat, d_qk):
    bsz, s, hp = q.shape
    d_c = v.shape[-1]
    d_v = d_c // H_C
    n_pairs = H_C // 2
    kern = functools.partial(_attn_kernel, n_lat_tiles=l_lat // TM, l_lat=l_lat, s_total=s,
                             scale=float(d_qk) ** -0.5, d_v=d_v)
    return pl.pallas_call(
        kern,
        grid=(bsz, n_pairs, s // TM),
        in_specs=[pl.BlockSpec((1, TM, 2 * LANES), lambda b, h, i: (b, i, h)),
                  pl.BlockSpec((1, s, 2 * LANES), lambda b, h, i: (b, 0, h)),
                  pl.BlockSpec((1, s, 2 * d_v), lambda b, h, i: (b, 0, h))],
        out_specs=pl.BlockSpec((1, TM, 2 * d_v), lambda b, h, i: (b, i, h)),
        out_shape=jax.ShapeDtypeStruct((bsz, s, d_c), BF16),
        compiler_params=_cparams(("arbitrary", "arbitrary", "arbitrary")),
        name="attention",
    )(q, k, v)


def _postmix_kernel(x_ref, mod_ref, ya_ref, gg_ref, hs_ref, yc_ref, wout_ref, g2_ref, *rest,
                    d, n_experts, route):
    if route:
        wr_ref, x1_ref, h2_ref, re_ref, rg_ref = rest
    else:
        x1_ref, h2_ref = rest
    gate1 = mod_ref[0, :, 2 * d:3 * d]
    shift2 = mod_ref[0, :, 3 * d:4 * d]
    scale2 = mod_ref[0, :, 4 * d:5 * d]
    y = jnp.concatenate([ya_ref[0].astype(BF16), (gg_ref[0] * hs_ref[0]).astype(BF16), yc_ref[0]],
                        axis=1)
    x1 = x_ref[0] + gate1 * jnp.dot(y, wout_ref[...], preferred_element_type=F32)
    x1_ref[0] = x1
    h2 = _rms(x1, g2_ref[...]) * (1.0 + scale2) + shift2
    h2_ref[0] = h2.astype(h2_ref.dtype)
    if route:
        logits = jnp.dot(h2, wr_ref[...], preferred_element_type=F32)
        lane = lax.broadcasted_iota(jnp.int32, logits.shape, 1)
        neg_inf = jnp.float32(-jnp.inf)
        lg = jnp.where(lane < n_experts, logits, neg_inf)
        m1 = jnp.max(lg, axis=-1, keepdims=True)
        i1 = jnp.min(jnp.where(lg == m1, lane, LANES), axis=-1, keepdims=True)
        lg2 = jnp.where(lane == i1, neg_inf, lg)
        m2 = jnp.max(lg2, axis=-1, keepdims=True)
        i2 = jnp.min(jnp.where(lg2 == m2, lane, LANES), axis=-1, keepdims=True)
        e = jnp.exp(m2 - m1)
        den = 1.0 + e
        re_ref[0] = jnp.where(lane == 0, i1, i2)
        rg_ref[0] = jnp.where(lane == 0, 1.0 / den, e / den)


def _postmix(xt, mods, ya, gg, hs, yc, wout, g2, wr, *, n_lat_tiles, n_experts):
    bsz, s, d = xt.shape
    route = wr is not None
    tile = lambda b, i: (b, i, 0)
    const2 = lambda b, i: (0, 0)
    full = lambda a: pl.BlockSpec(a.shape, const2)
    in_specs = [pl.BlockSpec((1, TM, d), tile),
                pl.BlockSpec((1, 1, mods.shape[-1]), _tile_mod_index(n_lat_tiles, bsz)),
                pl.BlockSpec((1, TM, ya.shape[-1]), tile), pl.BlockSpec((1, TM, gg.shape[-1]), tile),
                pl.BlockSpec((1, TM, hs.shape[-1]), tile), pl.BlockSpec((1, TM, yc.shape[-1]), tile),
                full(wout), full(g2)]
    args = [xt, mods, ya, gg, hs, yc, wout, g2]
    out_specs = [pl.BlockSpec((1, TM, d), tile), pl.BlockSpec((1, TM, d), tile)]
    out_shape = [jax.ShapeDtypeStruct((bsz, s, d), F32),
                 jax.ShapeDtypeStruct((bsz, s, d), F32 if route else BF16)]
    if route:
        in_specs.append(full(wr))
        args.append(wr)
        out_specs += [pl.BlockSpec((1, TM, LANES), tile), pl.BlockSpec((1, TM, LANES), tile)]
        out_shape += [jax.ShapeDtypeStruct((bsz, s, LANES), jnp.int32),
                      jax.ShapeDtypeStruct((bsz, s, LANES), F32)]
    return pl.pallas_call(
        functools.partial(_postmix_kernel, d=d, n_experts=n_experts, route=route),
        grid=(bsz, s // TM),
        in_specs=in_specs, out_specs=out_specs, out_shape=out_shape,
        compiler_params=_cparams(("arbitrary", "arbitrary")),
        name="postmix_route" if route else "postmix",
    )(*args)


def _ffn_kernel(x1_ref, h2_ref, mod_ref, w1_ref, w3_ref, w2_ref, o_ref, *, d, fc):
    h = h2_ref[0]
    d_ff = w1_ref.shape[1]
    acc = jnp.zeros((h.shape[0], d), F32)
    for c in range(d_ff // fc):
        a = jnp.dot(h, w1_ref[:, c * fc:(c + 1) * fc], preferred_element_type=F32)
        b = jnp.dot(h, w3_ref[:, c * fc:(c + 1) * fc], preferred_element_type=F32)
        act = (a * jax.nn.sigmoid(a) * b).astype(BF16)
        acc = acc + jnp.dot(act, w2_ref[c * fc:(c + 1) * fc, :], preferred_element_type=F32)
    gate2 = mod_ref[0, :, 5 * d:6 * d]
    o_ref[0] = x1_ref[0] + gate2 * acc


def _ffn(x1, h2, mods, w1, w3, w2, *, n_lat_tiles):
    bsz, s, d = x1.shape
    tile = lambda b, i: (b, i, 0)
    const2 = lambda b, i: (0, 0)
    full = lambda a: pl.BlockSpec(a.shape, const2)
    return pl.pallas_call(
        functools.partial(_ffn_kernel, d=d, fc=256),
        grid=(bsz, s // TM),
        in_specs=[pl.BlockSpec((1, TM, d), tile), pl.BlockSpec((1, TM, d), tile),
                  pl.BlockSpec((1, 1, mods.shape[-1]), _tile_mod_index(n_lat_tiles, bsz)),
                  full(w1), full(w3), full(w2)],
        out_specs=pl.BlockSpec((1, TM, d), tile),
        out_shape=jax.ShapeDtypeStruct((bsz, s, d), F32),
        compiler_params=_cparams(("arbitrary", "arbitrary")),
        name="ffn",
    )(x1, h2, mods, w1, w3, w2)


def _row_gather_copy(src_hbm, dst_ref, sem, src_row, dst_row):
    return pltpu.make_async_copy(src_hbm.at[pl.ds(src_row, 1)], dst_ref.at[pl.ds(dst_row, 1)], sem)


def _dispatch_kernel(tok_ref, h_hbm, o_ref, sem):
    n = o_ref.shape[0]

    def start(j, _):
        _row_gather_copy(h_hbm, o_ref, sem, tok_ref[0, 0, j], j).start()
        return 0

    def wait(j, _):
        _row_gather_copy(h_hbm, o_ref, sem, 0, j).wait()
        return 0

    lax.fori_loop(0, n, start, 0)
    lax.fori_loop(0, n, wait, 0)


def _dispatch(tok_pad, h2_flat):
    n_blk = tok_pad.shape[0]
    d = h2_flat.shape[1]
    return pl.pallas_call(
        _dispatch_kernel,
        grid=(n_blk,),
        in_specs=[pl.BlockSpec((1, 1, TMB), lambda r: (r, 0, 0), memory_space=pltpu.SMEM),
                  pl.BlockSpec(memory_space=pl.ANY)],
        out_specs=pl.BlockSpec((TMB, d), lambda r: (r, 0)),
        out_shape=jax.ShapeDtypeStruct((n_blk * TMB, d), h2_flat.dtype),
        scratch_shapes=[pltpu.SemaphoreType.DMA(())],
        compiler_params=_cparams(("arbitrary",)),
        name="moe_dispatch",
    )(tok_pad, h2_flat)


def _experts_kernel(be_ref, nu_ref, xs_ref, gp_ref, w1_ref, w3_ref, w2_ref, o_ref, xb_ref, acc_ref):
    r = pl.program_id(0)
    f = pl.program_id(1)
    nf = pl.num_programs(1)

    @pl.when(r < nu_ref[0])
    def _():
        @pl.when(f == 0)
        def _():
            xb_ref[...] = xs_ref[...].astype(BF16)
            acc_ref[...] = jnp.zeros_like(acc_ref)

        xb = xb_ref[...]
        a = jnp.dot(xb, w1_ref[0].astype(BF16), preferred_element_type=F32)
        b = jnp.dot(xb, w3_ref[0].astype(BF16), preferred_element_type=F32)
        act = (a * jax.nn.sigmoid(a) * b).astype(BF16)
        acc_ref[...] += jnp.dot(act, w2_ref[0].astype(BF16), preferred_element_type=F32)

        @pl.when(f == nf - 1)
        def _():
            o_ref[...] = acc_ref[...] * gp_ref[...]

    @pl.when(jnp.logical_and(r >= nu_ref[0], f == nf - 1))
    def _():
        o_ref[...] = jnp.zeros_like(o_ref)


def _experts(blk_e, n_used, xs, gate_pad, w1, w3, w2):
    n_pad, d = xs.shape
    n_blk = n_pad // TMB
    d_ff = w1.shape[-1]
    nf = d_ff // TF

    def w_col(r, f, be, nu):
        live = r < nu[0]
        return (be[r], 0, jnp.where(live, f, nf - 1))

    def w_row(r, f, be, nu):
        live = r < nu[0]
        return (be[r], jnp.where(live, f, nf - 1), 0)

    grid_spec = pltpu.PrefetchScalarGridSpec(
        num_scalar_prefetch=2,
        grid=(n_blk, nf),
        in_specs=[pl.BlockSpec((TMB, d), lambda r, f, be, nu: (r, 0)),
                  pl.BlockSpec((TMB, 1), lambda r, f, be, nu: (r, 0)),
                  pl.BlockSpec((1, d, TF), w_col),
                  pl.BlockSpec((1, d, TF), w_col),
                  pl.BlockSpec((1, TF, d), w_row)],
        out_specs=pl.BlockSpec((TMB, d), lambda r, f, be, nu: (r, 0)),
        scratch_shapes=[pltpu.VMEM((TMB, d), BF16), pltpu.VMEM((TMB, d), F32)])
    return pl.pallas_call(
        _experts_kernel,
        grid_spec=grid_spec,
        out_shape=jax.ShapeDtypeStruct((n_pad, d), F32),
        compiler_params=_cparams(("arbitrary", "arbitrary")),
        name="moe_experts",
    )(blk_e, n_used, xs, gate_pad, w1, w3, w2)


def _combine_kernel(pos_ref, x1_ref, mod_ref, yp_hbm, *rest, d, final):
    if final:
        gf_ref, o_ref, buf_ref, sem = rest
    else:
        o_ref, buf_ref, sem = rest
    n = x1_ref.shape[1]

    def start(j, _):
        for kk in range(TOP_K):
            _row_gather_copy(yp_hbm, buf_ref.at[kk], sem, pos_ref[0, 0, TOP_K * j + kk], j).start()
        return 0

    def wait(j, _):
        for kk in range(TOP_K):
            _row_gather_copy(yp_hbm, buf_ref.at[kk], sem, 0, j).wait()
        return 0

    lax.fori_loop(0, n, start, 0)
    lax.fori_loop(0, n, wait, 0)
    gate2 = mod_ref[0, :, 5 * d:6 * d]
    y = buf_ref[0]
    for kk in range(1, TOP_K):
        y = y + buf_ref[kk]
    x2 = x1_ref[0] + gate2 * y
    if final:
        x2 = _rms(x2, gf_ref[...])
    o_ref[0] = x2


def _combine(pos, x1, mods, yp, gf, *, n_lat_tiles, n_tiles):
    bsz, s, d = x1.shape
    final = gf is not None
    tiles_per_seq = s // TM
    tile = lambda b, i: (b, i, 0)
    in_specs = [pl.BlockSpec((1, 1, TOP_K * TM), lambda b, i: (b * tiles_per_seq + i, 0, 0),
                             memory_space=pltpu.SMEM),
                pl.BlockSpec((1, TM, d), tile),
                pl.BlockSpec((1, 1, mods.shape[-1]), _tile_mod_index(n_lat_tiles, bsz)),
                pl.BlockSpec(memory_space=pl.ANY)]
    args = [pos, x1, mods, yp]
    if final:
        in_specs.append(pl.BlockSpec(gf.shape, lambda b, i: (0, 0)))
        args.append(gf)
    return pl.pallas_call(
        functools.partial(_combine_kernel, d=d, final=final),
        grid=(bsz, n_tiles),
        in_specs=in_specs,
        out_specs=pl.BlockSpec((1, TM, d), tile),
        out_shape=jax.ShapeDtypeStruct((bsz, n_tiles * TM, d), F32),
        scratch_shapes=[pltpu.VMEM((TOP_K, TM, d), F32), pltpu.SemaphoreType.DMA(())],
        compiler_params=_cparams(("arbitrary", "arbitrary")),
        name="moe_combine_final" if final else "moe_combine",
    )(*args)


def _route_tables(route_e, route_g, n_experts):
    bsz, s, _ = route_e.shape
    n_assign = bsz * s * TOP_K
    flat_e = route_e[..., :TOP_K].reshape(n_assign)
    flat_g = route_g[..., :TOP_K].reshape(n_assign)
    onehot = (flat_e[:, None] == jnp.arange(n_experts, dtype=jnp.int32)[None, :]).astype(jnp.int32)
    csum = jnp.cumsum(onehot, axis=0)
    rank = jnp.sum(csum * onehot, axis=1) - 1
    counts = csum[-1]
    padded = (counts + TMB - 1) // TMB * TMB
    pad_ends = jnp.cumsum(padded)
    pad_starts = pad_ends - padded
    dest = pad_starts[flat_e] + rank
    n_blk = (n_assign + TMB - 1) // TMB + n_experts
    n_pad = n_blk * TMB
    tok_flat = jnp.arange(n_assign, dtype=jnp.int32) // TOP_K
    tok_pad = jnp.zeros((n_pad,), jnp.int32).at[dest].set(tok_flat)
    gate_pad = jnp.zeros((n_pad,), F32).at[dest].set(flat_g)
    blk_e = jnp.minimum(
        jnp.searchsorted(pad_ends, jnp.arange(n_blk, dtype=jnp.int32) * TMB, side='right'),
        n_experts - 1).astype(jnp.int32)
    n_used = (pad_ends[-1] // TMB).astype(jnp.int32).reshape(1)
    pos = dest.astype(jnp.int32).reshape(bsz * s // TM, 1, TOP_K * TM)
    return tok_pad.reshape(n_blk, 1, TMB), gate_pad.reshape(n_pad, 1), blk_e, n_used, pos


def _final_kernel(x_ref, g_ref, o_ref):
    o_ref[0] = _rms(x_ref[0], g_ref[...])


def _final_norm(xt, g, *, n_tiles):
    bsz, s, d = xt.shape
    tile = lambda b, i: (b, i, 0)
    return pl.pallas_call(
        _final_kernel,
        grid=(bsz, n_tiles),
        in_specs=[pl.BlockSpec((1, TM, d), tile), pl.BlockSpec(g.shape, lambda b, i: (0, 0))],
        out_specs=pl.BlockSpec((1, TM, d), tile),
        out_shape=jax.ShapeDtypeStruct((bsz, n_tiles * TM, d), F32),
        compiler_params=_cparams(("arbitrary", "arbitrary")),
        name="final_norm",
    )(xt, g)


def _rope_partner(r):
    return jnp.concatenate([-r[..., 8:16], r[..., 0:8], -r[..., 24:32], r[..., 16:24]], axis=-1)


def _rope_tables(l_lat, s_total, d_rope, d_nope):
    t = jnp.arange(l_lat, dtype=jnp.int32)
    row = (t // GRID_W).astype(F32)
    col = (t % GRID_W).astype(F32)
    half = d_rope // 2
    inv_freq = ROPE_BASE ** (-jnp.arange(0, half, 2, dtype=F32) / half)
    ang_r = row[:, None] * inv_freq
    ang_c = col[:, None] * inv_freq
    cos = jnp.concatenate([jnp.cos(ang_r), jnp.cos(ang_r), jnp.cos(ang_c), jnp.cos(ang_c)], axis=1)
    sin = jnp.concatenate([jnp.sin(ang_r), jnp.sin(ang_r), jnp.sin(ang_c), jnp.sin(ang_c)], axis=1)
    n_ctx = s_total - l_lat
    cosk = jnp.concatenate([cos, jnp.ones((n_ctx, d_rope), F32)], axis=0)
    sink = jnp.concatenate([sin, jnp.zeros((n_ctx, d_rope), F32)], axis=0)
    pad = LANES - d_nope - d_rope
    cosq = jnp.concatenate([jnp.ones((s_total, d_nope), F32), cosk, jnp.zeros((s_total, pad), F32)], axis=1)
    sinq = jnp.concatenate([jnp.zeros((s_total, d_nope), F32), sink, jnp.zeros((s_total, pad), F32)], axis=1)
    return cosq, sinq, cosk, sink


def _block_diag(w):
    h, a, b = w.shape
    eye = jnp.eye(h, dtype=w.dtype)
    return (eye[:, None, :, None] * w[:, :, None, :]).reshape(h * a, h * b)


def kernel(x, c, ctx, c_ctx, w_mod, b_mod, norm1_g, norm2_g, w_in, w_out, sgu_ln_g, sgu_ln_b, sgu_w, sgu_b, conv_w, conv_b, lru_w_a, lru_b_a, lru_w_x, lru_b_x, lru_lam, mla_q_norm, mla_w_uq, mla_kv_norm, mla_w_uk, mla_w_uv, ffn_w1, ffn_w3, ffn_w2, moe_router, moe_w1, moe_w3, moe_w2, final_norm_g):
    bsz, l_lat, d = x.shape
    l_ctx = ctx.shape[1]
    s_total = l_lat + l_ctx
    depth = w_mod.shape[0]
    d_a = sgu_ln_g.shape[-1]
    h_a, chunk = sgu_w.shape[1], sgu_w.shape[2]
    d_b = conv_w.shape[-1]
    q_lora = mla_q_norm.shape[-1]
    kv_lora = mla_kv_norm.shape[-1]
    d_c = mla_w_uv.shape[-1]
    d_v = d_c // H_C
    d_nope = mla_w_uk.shape[-1] // H_C
    d_qk = mla_w_uq.shape[-1] // H_C
    d_rope = d_qk - d_nope
    n_experts = moe_router.shape[-1]
    assert l_lat % TM == 0 and l_ctx % TM == 0 and TM % chunk == 0 and TM == T_SCAN
    assert d_qk <= LANES and 2 * d_v == LANES and H_C % 2 == 0 and d_rope == 32
    n_lat_tiles = l_lat // TM
    n_ctx_tiles = l_ctx // TM
    dims = (d, d_a, d_b, q_lora, kv_lora, d_rope, h_a, chunk)

    xt = jnp.concatenate([x, ctx], axis=1)

    n_rows = (bsz + 1 + 7) // 8 * 8
    cond = jnp.zeros((n_rows, d), F32).at[:bsz].set(c).at[bsz].set(c_ctx)
    mods_all = _modulation(cond, w_mod.astype(BF16), b_mod[:, None, :])
    mods_all = mods_all[:, :bsz + 1, None, :]

    cosq, sinq, cosk, sink = _rope_tables(l_lat, s_total, d_rope, d_nope)
    head_pad = LANES - d_qk
    e_head = jnp.concatenate([jnp.zeros((d_rope, d_nope), F32), jnp.eye(d_rope, dtype=F32),
                              jnp.zeros((d_rope, head_pad), F32)], axis=1)
    emat = jnp.tile(e_head, (1, H_C)).astype(BF16)

    out = None
    for l in range(depth):
        last = l == depth - 1
        mods = mods_all[l]
        o_m = 2 * d_a + 2 * d_b
        o_r = o_m + q_lora + kv_lora
        w_rope = w_in[l][:, o_r:o_r + d_rope]
        n_in = (o_r + 2 * d_rope + LANES - 1) // LANES * LANES
        win = jnp.concatenate([w_in[l][:, :o_r + d_rope], _rope_partner(w_rope),
                               jnp.zeros((d, n_in - o_r - 2 * d_rope), F32)], axis=1).astype(BF16)
        wq = mla_w_uq[l].reshape(q_lora, H_C, d_qk)
        zq = jnp.zeros((q_lora, H_C, head_pad), F32)
        wuq = jnp.concatenate([wq, zq], axis=-1).reshape(q_lora, H_C * LANES).astype(BF16)
        wuqp = jnp.concatenate([jnp.zeros((q_lora, H_C, d_nope), F32), _rope_partner(wq[..., d_nope:]), zq],
                               axis=-1).reshape(q_lora, H_C * LANES).astype(BF16)
        wk = mla_w_uk[l].reshape(kv_lora, H_C, d_nope)
        wuk = jnp.concatenate([wk, jnp.zeros((kv_lora, H_C, LANES - d_nope), F32)],
                              axis=-1).reshape(kv_lora, H_C * LANES).astype(BF16)
        bs_full = jnp.repeat(sgu_b[l].T, d_a // h_a, axis=1)

        ya, gg, xb, q, k, v = _premix(
            xt, mods, norm1_g[l][None], win, sgu_ln_g[l][None], sgu_ln_b[l][None],
            sgu_w[l].astype(BF16), bs_full, mla_q_norm[l][None], wuq, wuqp, mla_kv_norm[l][None],
            wuk, emat, mla_w_uv[l].astype(BF16), cosq, sinq, cosk, sink,
            n_lat_tiles=n_lat_tiles, dims=dims)

        wg = jnp.concatenate([_block_diag(lru_w_a[l, 0]), _block_diag(lru_w_x[l, 0]),
                              _block_diag(lru_w_a[l, 1]), _block_diag(lru_w_x[l, 1])], axis=1).astype(BF16)
        bg = jnp.concatenate([lru_b_a[l, 0], lru_b_x[l, 0], lru_b_a[l, 1], lru_b_x[l, 1]])[None]
        hs = _lru(xb, conv_w[l], conv_b[l][None], wg, bg, lru_lam[l], n_lat=n_lat_tiles, n_ctx=n_ctx_tiles)

        yc = _attention(q, k, v, l_lat=l_lat, d_qk=d_qk)

        if l % 2 == 0:
            i = l // 2
            x1, h2 = _postmix(xt, mods, ya, gg, hs, yc, w_out[l].astype(BF16), norm2_g[l][None], None,
                              n_lat_tiles=n_lat_tiles, n_experts=n_experts)
            xt = _ffn(x1, h2, mods, ffn_w1[i].astype(BF16), ffn_w3[i].astype(BF16), ffn_w2[i].astype(BF16),
                      n_lat_tiles=n_lat_tiles)
            if last:
                out = _final_norm(xt, final_norm_g[None], n_tiles=n_lat_tiles)
        else:
            i = l // 2
            wr = jnp.concatenate([moe_router[i], jnp.zeros((d, LANES - n_experts), F32)], axis=1)
            x1, h2, route_e, route_g = _postmix(xt, mods, ya, gg, hs, yc, w_out[l].astype(BF16),
                                                norm2_g[l][None], wr,
                                                n_lat_tiles=n_lat_tiles, n_experts=n_experts)
            tok_pad, gate_pad, blk_e, n_used, pos = _route_tables(route_e, route_g, n_experts)
            xs = _dispatch(tok_pad, h2.reshape(bsz * s_total, d))
            yp = _experts(blk_e, n_used, xs, gate_pad, moe_w1[i], moe_w3[i], moe_w2[i])
            if last:
                out = _combine(pos, x1, mods, yp, final_norm_g[None],
                               n_lat_tiles=n_lat_tiles, n_tiles=n_lat_tiles)
            else:
                xt = _combine(pos, x1, mods, yp, None, n_lat_tiles=n_lat_tiles,
                              n_tiles=n_lat_tiles + n_ctx_tiles)
    return out
```

```python
import functools

import jax
import jax.numpy as jnp
from jax import lax
from jax.experimental import pallas as pl
from jax.experimental.pallas import tpu as pltpu

F32 = jnp.float32
BF16 = jnp.bfloat16

EPS = 1e-6
GRID_W = 64
ROPE_BASE = 10000.0
LRU_C = 8.0
H_C = 8
TOP_K = 2
LANES = 128
TM = 256
T_SCAN = 256
KC = 1024
TMB = 1024
TF = 512
VMEM_LIMIT = 56 * 1024 * 1024


def _cparams(sem):
    return pltpu.CompilerParams(dimension_semantics=sem, vmem_limit_bytes=VMEM_LIMIT)


def _rms(x, g):
    return x * lax.rsqrt(jnp.mean(x * x, axis=-1, keepdims=True) + EPS) * g


def _mod_kernel(c_ref, w_ref, b_ref, o_ref):
    c = c_ref[...]
    a = (c * jax.nn.sigmoid(c)).astype(BF16)
    o_ref[0] = jnp.dot(a, w_ref[0], preferred_element_type=F32) + b_ref[0]


def _modulation(cond, w_mod, b_mod):
    depth, d, n = w_mod.shape
    r = cond.shape[0]
    tn = 1024
    return pl.pallas_call(
        _mod_kernel,
        grid=(depth, n // tn),
        in_specs=[pl.BlockSpec((r, d), lambda l, j: (0, 0)),
                  pl.BlockSpec((1, d, tn), lambda l, j: (l, 0, j)),
                  pl.BlockSpec((1, 1, tn), lambda l, j: (l, 0, j))],
        out_specs=pl.BlockSpec((1, r, tn), lambda l, j: (l, 0, j)),
        out_shape=jax.ShapeDtypeStruct((depth, r, n), F32),
        compiler_params=_cparams(("arbitrary", "arbitrary")),
        name="modulation",
    )(cond, w_mod, b_mod)


def _premix_kernel(x_ref, mod_ref, g_ref, win_ref, lng_ref, lnb_ref, ws_ref, bs_ref,
                   qn_ref, wuq_ref, wuqp_ref, kvn_ref, wuk_ref, e_ref, wuv_ref, vone_ref,
                   cq_ref, sq_ref, ck_ref, sk_ref,
                   ya_ref, gg_ref, xb_ref, q_ref, k_ref, v_ref, *, dims):
    d, d_a, d_b, q_lora, kv_lora, d_rope, h_a, chunk = dims
    x = x_ref[0]
    shift = mod_ref[0, :, 0:d]
    scale = mod_ref[0, :, d:2 * d]
    h = (_rms(x, g_ref[...]) * (1.0 + scale) + shift).astype(BF16)
    z = jnp.dot(h, win_ref[...], preferred_element_type=F32)

    o = 0
    u = jax.nn.gelu(z[:, o:o + d_a])
    v = jax.nn.gelu(z[:, o + d_a:o + 2 * d_a])
    mu = jnp.mean(v, axis=-1, keepdims=True)
    vc = v - mu
    var = jnp.mean(vc * vc, axis=-1, keepdims=True)
    vn = (vc * lax.rsqrt(var + EPS) * lng_ref[...] + lnb_ref[...]).astype(BF16)
    dh_a = d_a // h_a
    tm = x.shape[0]
    head_of_lane = lax.broadcasted_iota(jnp.int32, (chunk, d_a), 1) // dh_a
    for c in range(tm // chunk):
        vch = vn[c * chunk:(c + 1) * chunk]
        s = jnp.dot(ws_ref[0], vch, preferred_element_type=F32)
        for hd in range(1, h_a):
            s = jnp.where(head_of_lane == hd,
                          jnp.dot(ws_ref[hd], vch, preferred_element_type=F32), s)
        s = s + bs_ref[...]
        ya_ref[0, c * chunk:(c + 1) * chunk, :] = u[c * chunk:(c + 1) * chunk] * s

    o = 2 * d_a
    gg_ref[0] = jax.nn.gelu(z[:, o:o + d_b])
    xb_ref[0] = z[:, o + d_b:o + 2 * d_b]

    o = 2 * d_a + 2 * d_b
    cq = _rms(z[:, o:o + q_lora], qn_ref[...]).astype(BF16)
    qa = jnp.dot(cq, wuq_ref[...], preferred_element_type=F32)
    qb = jnp.dot(cq, wuqp_ref[...], preferred_element_type=F32)
    cos_q = jnp.concatenate([cq_ref[...]] * H_C, axis=1)
    sin_q = jnp.concatenate([sq_ref[...]] * H_C, axis=1)
    q_ref[0] = (qa * cos_q + qb * sin_q).astype(BF16)
    o += q_lora
    ckv = _rms(z[:, o:o + kv_lora], kvn_ref[...]).astype(BF16)
    o += kv_lora
    zr = z[:, o:o + d_rope]
    zrp = z[:, o + d_rope:o + 2 * d_rope]
    kr = (zr * ck_ref[...] + zrp * sk_ref[...]).astype(BF16)
    kn = jnp.dot(ckv, wuk_ref[...], preferred_element_type=F32)
    k_ref[0] = (kn + jnp.dot(kr, e_ref[...], preferred_element_type=F32)).astype(BF16)
    v_ref[0] = (jnp.dot(ckv, wuv_ref[...], preferred_element_type=F32) + vone_ref[...]).astype(BF16)


def _tile_mod_index(n_lat_tiles, n_batch):
    def index(b, i):
        return (jnp.where(i < n_lat_tiles, b, n_batch), 0, 0)
    return index


def _premix(xt, mods, g1, win, lng, lnb, ws, bs, qn, wuq, wuqp, kvn, wuk, emat, wuv, vone,
            cosq, sinq, cosk, sink, *, n_lat_tiles, dims):
    bsz, s, d = xt.shape
    d_a, d_b = dims[1], dims[2]
    hp = H_C * LANES
    const2 = lambda b, i: (0, 0)
    const3 = lambda b, i: (0, 0, 0)
    tile = lambda b, i: (b, i, 0)
    full = lambda a: pl.BlockSpec(a.shape, const2 if a.ndim == 2 else const3)
    return pl.pallas_call(
        functools.partial(_premix_kernel, dims=dims),
        grid=(bsz, s // TM),
        in_specs=[pl.BlockSpec((1, TM, d), tile),
                  pl.BlockSpec((1, 1, mods.shape[-1]), _tile_mod_index(n_lat_tiles, bsz)),
                  full(g1), full(win), full(lng), full(lnb), full(ws), full(bs),
                  full(qn), full(wuq), full(wuqp), full(kvn), full(wuk), full(emat), full(wuv), full(vone),
                  pl.BlockSpec((TM, LANES), lambda b, i: (i, 0)),
                  pl.BlockSpec((TM, LANES), lambda b, i: (i, 0)),
                  pl.BlockSpec((TM, cosk.shape[1]), lambda b, i: (i, 0)),
                  pl.BlockSpec((TM, sink.shape[1]), lambda b, i: (i, 0))],
        out_specs=[pl.BlockSpec((1, TM, d_a), tile), pl.BlockSpec((1, TM, d_b), tile),
                   pl.BlockSpec((1, TM, d_b), tile), pl.BlockSpec((1, TM, hp), tile),
                   pl.BlockSpec((1, TM, hp), tile), pl.BlockSpec((1, TM, hp), tile)],
        out_shape=[jax.ShapeDtypeStruct((bsz, s, d_a), F32),
                   jax.ShapeDtypeStruct((bsz, s, d_b), F32),
                   jax.ShapeDtypeStruct((bsz, s, d_b), F32),
                   jax.ShapeDtypeStruct((bsz, s, hp), BF16),
                   jax.ShapeDtypeStruct((bsz, s, hp), BF16),
                   jax.ShapeDtypeStruct((bsz, s, hp), BF16)],
        compiler_params=_cparams(("arbitrary", "arbitrary")),
        name="premix",
    )(xt, mods, g1, win, lng, lnb, ws, bs, qn, wuq, wuqp, kvn, wuk, emat, wuv, vone,
      cosq, sinq, cosk, sink)


def _scan_tile(a, b, carry, reverse):
    t = a.shape[0]
    rows = lax.broadcasted_iota(jnp.int32, a.shape, 0)
    s = 1
    while s < t:
        if reverse:
            a_sh = pltpu.roll(a, t - s, 0)
            b_sh = pltpu.roll(b, t - s, 0)
            ok = rows < t - s
        else:
            a_sh = pltpu.roll(a, s, 0)
            b_sh = pltpu.roll(b, s, 0)
            ok = rows >= s
        b = b + a * jnp.where(ok, b_sh, 0.0)
        a = a * jnp.where(ok, a_sh, 1.0)
        s *= 2
    h = b + a * carry
    return h, (h[0:1] if reverse else h[t - 1:t])


def _lru_kernel(xb_ref, cw_ref, cb_ref, wg_ref, bg_ref, lam_ref, out_ref, xc_ref, *, n_lat, n_ctx):
    t = T_SCAN
    n_tiles = n_lat + n_ctx
    s_total = n_tiles * t
    d_b = xb_ref.shape[-1]
    w = cw_ref[...]
    cb = cb_ref[...]

    def conv_body(j, _):
        t0 = pl.multiple_of(j * t, t)
        is_ctx = j >= n_lat
        seq_lo = jnp.where(is_ctx, n_lat * t, 0)
        seq_hi = jnp.where(is_ctx, s_total, n_lat * t)
        cur = xb_ref[0, pl.ds(t0, t), :]
        p0 = pl.multiple_of(jnp.maximum(t0 - 8, 0), 8)
        n0 = pl.multiple_of(jnp.minimum(t0 + t, s_total - 8), 8)
        prev = jnp.where(t0 > seq_lo, xb_ref[0, pl.ds(p0, 8), :], 0.0)
        nxt = jnp.where(t0 + t < seq_hi, xb_ref[0, pl.ds(n0, 8), :], 0.0)
        ext = jnp.concatenate([prev, cur, nxt], axis=0)
        n_ext = t + 16
        xm2 = pltpu.roll(ext, 2, 0)[8:8 + t]
        xm1 = pltpu.roll(ext, 1, 0)[8:8 + t]
        xp1 = pltpu.roll(ext, n_ext - 1, 0)[8:8 + t]
        xc_ref[pl.ds(t0, t), :] = (w[0:1] * xm2 + w[1:2] * xm1 + w[2:3] * cur + w[3:4] * xp1 + cb)
        return 0

    lax.fori_loop(0, n_tiles, conv_body, 0)

    lam = lam_ref[...]
    neg = -lam
    softplus = jnp.maximum(neg, 0.0) + jnp.log1p(jnp.exp(-jnp.abs(neg)))

    def direction(dr, reverse):
        sp = softplus[dr:dr + 1]
        wg = wg_ref[:, dr * 2 * d_b:(dr + 1) * 2 * d_b]
        bg = bg_ref[:, dr * 2 * d_b:(dr + 1) * 2 * d_b]

        def body(j, carry):
            if reverse:
                idx = jnp.where(j < n_ctx, n_tiles - 1 - j, n_lat - 1 - (j - n_ctx))
            else:
                idx = jnp.where(j < n_ctx, n_lat + j, j - n_ctx)
            t0 = pl.multiple_of(idx * t, t)
            xc = xc_ref[pl.ds(t0, t), :]
            g = jnp.dot(xc.astype(BF16), wg, preferred_element_type=F32) + bg
            r = jax.nn.sigmoid(g[:, 0:d_b])
            ig = jax.nn.sigmoid(g[:, d_b:2 * d_b])
            log_a = (-LRU_C * r) * sp
            a = jnp.exp(log_a)
            bv = jnp.sqrt(-jnp.tanh(log_a) * (a * a + 1.0)) * (ig * xc)
            h, carry = _scan_tile(a, bv, carry, reverse)
            if reverse:
                out_ref[0, pl.ds(t0, t), :] = out_ref[0, pl.ds(t0, t), :] + h
            else:
                out_ref[0, pl.ds(t0, t), :] = h
            return carry

        lax.fori_loop(0, n_tiles, body, jnp.zeros((1, d_b), F32))

    direction(0, False)
    direction(1, True)


def _lru(xb, cw, cb, wg, bg, lam, *, n_lat, n_ctx):
    bsz, s, d_b = xb.shape
    const2 = lambda b: (0, 0)
    full = lambda a: pl.BlockSpec(a.shape, const2)
    return pl.pallas_call(
        functools.partial(_lru_kernel, n_lat=n_lat, n_ctx=n_ctx),
        grid=(bsz,),
        in_specs=[pl.BlockSpec((1, s, d_b), lambda b: (b, 0, 0)),
                  full(cw), full(cb), full(wg), full(bg), full(lam)],
        out_specs=pl.BlockSpec((1, s, d_b), lambda b: (b, 0, 0)),
        out_shape=jax.ShapeDtypeStruct((bsz, s, d_b), F32),
        scratch_shapes=[pltpu.VMEM((s, d_b), F32)],
        compiler_params=_cparams(("arbitrary",)),
        name="rglru",
    )(xb, cw, cb, wg, bg, lam)


def _attn_kernel(q_ref, k_ref, v_ref, o_ref, *, n_lat_tiles, chunks_lat, chunks_ctx, d_v):
    i = pl.program_id(2)
    tq = q_ref.shape[1]

    def compute(chunks):
        outs = []
        for hh in range(2):
            qh = q_ref[0, :, hh * LANES:(hh + 1) * LANES]
            m = acc = None
            for lo, n in chunks:
                kh = k_ref[0, lo:lo + n, hh * LANES:(hh + 1) * LANES]
                s = lax.dot_general(qh, kh, (((1,), (1,)), ((), ())), preferred_element_type=F32)
                mc = jnp.max(s, axis=-1, keepdims=True)
                m_new = mc if m is None else jnp.maximum(m, mc)
                p = jnp.exp2(s - m_new).astype(BF16)
                pv = jnp.dot(p, v_ref[0, lo:lo + n, hh * LANES:(hh + 1) * LANES],
                             preferred_element_type=F32)
                acc = pv if m is None else jnp.exp2(m - m_new) * acc + pv
                m = m_new
            outs.append(acc / acc[:, d_v:d_v + 1])
        lane = lax.broadcasted_iota(jnp.int32, (tq, LANES), 1)
        o_ref[0] = jnp.where(lane < d_v, outs[0], pltpu.roll(outs[1], d_v, 1)).astype(o_ref.dtype)

    @pl.when(i < n_lat_tiles)
    def _():
        compute(chunks_lat + chunks_ctx)

    @pl.when(i >= n_lat_tiles)
    def _():
        compute(chunks_ctx)


def _key_chunks(lo, hi, size):
    return tuple((a, min(size, hi - a)) for a in range(lo, hi, size))


def _attention(q, k, v, *, l_lat, d_v):
    bsz, s, hp = q.shape
    d_c = H_C * d_v
    n_pairs = H_C // 2
    kern = functools.partial(_attn_kernel, n_lat_tiles=l_lat // TM,
                             chunks_lat=_key_chunks(0, l_lat, KC), chunks_ctx=_key_chunks(l_lat, s, KC),
                             d_v=d_v)
    return pl.pallas_call(
        kern,
        grid=(bsz, n_pairs, s // TM),
        in_specs=[pl.BlockSpec((1, TM, 2 * LANES), lambda b, h, i: (b, i, h)),
                  pl.BlockSpec((1, s, 2 * LANES), lambda b, h, i: (b, 0, h)),
                  pl.BlockSpec((1, s, 2 * LANES), lambda b, h, i: (b, 0, h))],
        out_specs=pl.BlockSpec((1, TM, 2 * d_v), lambda b, h, i: (b, i, h)),
        out_shape=jax.ShapeDtypeStruct((bsz, s, d_c), BF16),
        compiler_params=_cparams(("arbitrary", "arbitrary", "arbitrary")),
        name="attention",
    )(q, k, v)


def _postmix_kernel(x_ref, mod_ref, ya_ref, gg_ref, hs_ref, yc_ref, wout_ref, g2_ref, *rest,
                    d, n_experts, route):
    if route:
        wr_ref, x1_ref, h2_ref, re_ref, rg_ref = rest
    else:
        x1_ref, h2_ref = rest
    gate1 = mod_ref[0, :, 2 * d:3 * d]
    shift2 = mod_ref[0, :, 3 * d:4 * d]
    scale2 = mod_ref[0, :, 4 * d:5 * d]
    y = jnp.concatenate([ya_ref[0].astype(BF16), (gg_ref[0] * hs_ref[0]).astype(BF16), yc_ref[0]],
                        axis=1)
    x1 = x_ref[0] + gate1 * jnp.dot(y, wout_ref[...], preferred_element_type=F32)
    x1_ref[0] = x1
    h2 = _rms(x1, g2_ref[...]) * (1.0 + scale2) + shift2
    h2_ref[0] = h2.astype(h2_ref.dtype)
    if route:
        logits = jnp.dot(h2, wr_ref[...], preferred_element_type=F32)
        lane = lax.broadcasted_iota(jnp.int32, logits.shape, 1)
        neg_inf = jnp.float32(-jnp.inf)
        lg = jnp.where(lane < n_experts, logits, neg_inf)
        m1 = jnp.max(lg, axis=-1, keepdims=True)
        i1 = jnp.min(jnp.where(lg == m1, lane, LANES), axis=-1, keepdims=True)
        lg2 = jnp.where(lane == i1, neg_inf, lg)
        m2 = jnp.max(lg2, axis=-1, keepdims=True)
        i2 = jnp.min(jnp.where(lg2 == m2, lane, LANES), axis=-1, keepdims=True)
        e = jnp.exp(m2 - m1)
        den = 1.0 + e
        re_ref[0] = jnp.where(lane == 0, i1, i2)
        rg_ref[0] = jnp.where(lane == 0, 1.0 / den, e / den)


def _postmix(xt, mods, ya, gg, hs, yc, wout, g2, wr, *, n_lat_tiles, n_experts):
    bsz, s, d = xt.shape
    route = wr is not None
    tile = lambda b, i: (b, i, 0)
    const2 = lambda b, i: (0, 0)
    full = lambda a: pl.BlockSpec(a.shape, const2)
    in_specs = [pl.BlockSpec((1, TM, d), tile),
                pl.BlockSpec((1, 1, mods.shape[-1]), _tile_mod_index(n_lat_tiles, bsz)),
                pl.BlockSpec((1, TM, ya.shape[-1]), tile), pl.BlockSpec((1, TM, gg.shape[-1]), tile),
                pl.BlockSpec((1, TM, hs.shape[-1]), tile), pl.BlockSpec((1, TM, yc.shape[-1]), tile),
                full(wout), full(g2)]
    args = [xt, mods, ya, gg, hs, yc, wout, g2]
    out_specs = [pl.BlockSpec((1, TM, d), tile), pl.BlockSpec((1, TM, d), tile)]
    out_shape = [jax.ShapeDtypeStruct((bsz, s, d), F32),
                 jax.ShapeDtypeStruct((bsz, s, d), F32 if route else BF16)]
    if route:
        in_specs.append(full(wr))
        args.append(wr)
        out_specs += [pl.BlockSpec((1, TM, LANES), tile), pl.BlockSpec((1, TM, LANES), tile)]
        out_shape += [jax.ShapeDtypeStruct((bsz, s, LANES), jnp.int32),
                      jax.ShapeDtypeStruct((bsz, s, LANES), F32)]
    return pl.pallas_call(
        functools.partial(_postmix_kernel, d=d, n_experts=n_experts, route=route),
        grid=(bsz, s // TM),
        in_specs=in_specs, out_specs=out_specs, out_shape=out_shape,
        compiler_params=_cparams(("arbitrary", "arbitrary")),
        name="postmix_route" if route else "postmix",
    )(*args)


def _ffn_kernel(x1_ref, h2_ref, mod_ref, w1_ref, w3_ref, w2_ref, o_ref, *, d, fc):
    h = h2_ref[0]
    d_ff = w1_ref.shape[1]
    acc = jnp.zeros((h.shape[0], d), F32)
    for c in range(d_ff // fc):
        a = jnp.dot(h, w1_ref[:, c * fc:(c + 1) * fc], preferred_element_type=F32)
        b = jnp.dot(h, w3_ref[:, c * fc:(c + 1) * fc], preferred_element_type=F32)
        act = (a * jax.nn.sigmoid(a) * b).astype(BF16)
        acc = acc + jnp.dot(act, w2_ref[c * fc:(c + 1) * fc, :], preferred_element_type=F32)
    gate2 = mod_ref[0, :, 5 * d:6 * d]
    o_ref[0] = x1_ref[0] + gate2 * acc


def _ffn(x1, h2, mods, w1, w3, w2, *, n_lat_tiles):
    bsz, s, d = x1.shape
    tile = lambda b, i: (b, i, 0)
    const2 = lambda b, i: (0, 0)
    full = lambda a: pl.BlockSpec(a.shape, const2)
    return pl.pallas_call(
        functools.partial(_ffn_kernel, d=d, fc=256),
        grid=(bsz, s // TM),
        in_specs=[pl.BlockSpec((1, TM, d), tile), pl.BlockSpec((1, TM, d), tile),
                  pl.BlockSpec((1, 1, mods.shape[-1]), _tile_mod_index(n_lat_tiles, bsz)),
                  full(w1), full(w3), full(w2)],
        out_specs=pl.BlockSpec((1, TM, d), tile),
        out_shape=jax.ShapeDtypeStruct((bsz, s, d), F32),
        compiler_params=_cparams(("arbitrary", "arbitrary")),
        name="ffn",
    )(x1, h2, mods, w1, w3, w2)


DMA_UNROLL = 8


def _row_copy(src_ref, dst_ref, sem, src_row, dst_row):
    return pltpu.make_async_copy(src_ref.at[pl.ds(src_row, 1)], dst_ref.at[pl.ds(dst_row, 1)], sem)


def _dispatch_kernel(pos_ref, h_ref, xs_in_hbm, xs_hbm, sem):
    del xs_in_hbm
    n = h_ref.shape[1]

    def start(j, _):
        for kk in range(TOP_K):
            _row_copy(h_ref.at[0], xs_hbm, sem, j, pos_ref[0, 0, TOP_K * j + kk]).start()
        return 0

    def wait(j, _):
        for kk in range(TOP_K):
            _row_copy(h_ref.at[0], xs_hbm, sem, j, 0).wait()
        return 0

    lax.fori_loop(0, n, start, 0, unroll=DMA_UNROLL)
    lax.fori_loop(0, n, wait, 0, unroll=DMA_UNROLL)


def _dispatch(pos, h2, n_pad, *, n_tiles):
    bsz, s, d = h2.shape
    xs0 = jnp.zeros((n_pad, d), h2.dtype)
    return pl.pallas_call(
        _dispatch_kernel,
        grid=(bsz, n_tiles),
        in_specs=[pl.BlockSpec((1, 1, TOP_K * TM), lambda b, i: (b * n_tiles + i, 0, 0),
                               memory_space=pltpu.SMEM),
                  pl.BlockSpec((1, TM, d), lambda b, i: (b, i, 0)),
                  pl.BlockSpec(memory_space=pl.ANY)],
        out_specs=pl.BlockSpec(memory_space=pl.ANY),
        out_shape=jax.ShapeDtypeStruct((n_pad, d), h2.dtype),
        scratch_shapes=[pltpu.SemaphoreType.DMA(())],
        input_output_aliases={2: 0},
        compiler_params=_cparams(("arbitrary", "arbitrary")),
        name="moe_dispatch",
    )(pos, h2, xs0)


def _experts_kernel(be_ref, nu_ref, xs_ref, w1_ref, w3_ref, w2_ref, o_ref, xb_ref, acc_ref):
    r = pl.program_id(0)
    f = pl.program_id(1)
    nf = pl.num_programs(1)

    @pl.when(r < nu_ref[0])
    def _():
        @pl.when(f == 0)
        def _():
            xb_ref[...] = xs_ref[...].astype(BF16)
            acc_ref[...] = jnp.zeros_like(acc_ref)

        xb = xb_ref[...]
        a = jnp.dot(xb, w1_ref[0].astype(BF16), preferred_element_type=F32)
        b = jnp.dot(xb, w3_ref[0].astype(BF16), preferred_element_type=F32)
        act = (a * jax.nn.sigmoid(a) * b).astype(BF16)
        acc_ref[...] += jnp.dot(act, w2_ref[0].astype(BF16), preferred_element_type=F32)

        @pl.when(f == nf - 1)
        def _():
            o_ref[...] = acc_ref[...]

    @pl.when(jnp.logical_and(r >= nu_ref[0], f == nf - 1))
    def _():
        o_ref[...] = jnp.zeros_like(o_ref)


def _experts(blk_e, n_used, xs, w1, w3, w2):
    n_pad, d = xs.shape
    n_blk = n_pad // TMB
    d_ff = w1.shape[-1]
    nf = d_ff // TF

    def w_col(r, f, be, nu):
        live = r < nu[0]
        return (be[r], 0, jnp.where(live, f, nf - 1))

    def w_row(r, f, be, nu):
        live = r < nu[0]
        return (be[r], jnp.where(live, f, nf - 1), 0)

    grid_spec = pltpu.PrefetchScalarGridSpec(
        num_scalar_prefetch=2,
        grid=(n_blk, nf),
        in_specs=[pl.BlockSpec((TMB, d), lambda r, f, be, nu: (r, 0)),
                  pl.BlockSpec((1, d, TF), w_col),
                  pl.BlockSpec((1, d, TF), w_col),
                  pl.BlockSpec((1, TF, d), w_row)],
        out_specs=pl.BlockSpec((TMB, d), lambda r, f, be, nu: (r, 0)),
        scratch_shapes=[pltpu.VMEM((TMB, d), BF16), pltpu.VMEM((TMB, d), F32)])
    return pl.pallas_call(
        _experts_kernel,
        grid_spec=grid_spec,
        out_shape=jax.ShapeDtypeStruct((n_pad, d), F32),
        compiler_params=_cparams(("arbitrary", "arbitrary")),
        name="moe_experts",
    )(blk_e, n_used, xs, w1, w3, w2)


def _combine_kernel(pos_ref, x1_ref, mod_ref, rg_ref, yp_hbm, *rest, d, final):
    if final:
        gf_ref, o_ref, buf_ref, sem = rest
    else:
        o_ref, buf_ref, sem = rest
    n = x1_ref.shape[1]

    def start(j, _):
        for kk in range(TOP_K):
            _row_copy(yp_hbm, buf_ref.at[kk], sem, pos_ref[0, 0, TOP_K * j + kk], j).start()
        return 0

    def wait(j, _):
        for kk in range(TOP_K):
            _row_copy(yp_hbm, buf_ref.at[kk], sem, 0, j).wait()
        return 0

    lax.fori_loop(0, n, start, 0, unroll=DMA_UNROLL)
    lax.fori_loop(0, n, wait, 0, unroll=DMA_UNROLL)
    gate2 = mod_ref[0, :, 5 * d:6 * d]
    rg = rg_ref[0]
    y = rg[:, 0:1] * buf_ref[0]
    for kk in range(1, TOP_K):
        y = y + rg[:, kk:kk + 1] * buf_ref[kk]
    x2 = x1_ref[0] + gate2 * y
    if final:
        x2 = _rms(x2, gf_ref[...])
    o_ref[0] = x2


def _combine(pos, x1, mods, route_g, yp, gf, *, n_lat_tiles, n_tiles):
    bsz, s, d = x1.shape
    final = gf is not None
    tile = lambda b, i: (b, i, 0)
    in_specs = [pl.BlockSpec((1, 1, TOP_K * TM), lambda b, i: (b * n_tiles + i, 0, 0),
                             memory_space=pltpu.SMEM),
                pl.BlockSpec((1, TM, d), tile),
                pl.BlockSpec((1, 1, mods.shape[-1]), _tile_mod_index(n_lat_tiles, bsz)),
                pl.BlockSpec((1, TM, LANES), tile),
                pl.BlockSpec(memory_space=pl.ANY)]
    args = [pos, x1, mods, route_g, yp]
    if final:
        in_specs.append(pl.BlockSpec(gf.shape, lambda b, i: (0, 0)))
        args.append(gf)
    return pl.pallas_call(
        functools.partial(_combine_kernel, d=d, final=final),
        grid=(bsz, n_tiles),
        in_specs=in_specs,
        out_specs=pl.BlockSpec((1, TM, d), tile),
        out_shape=jax.ShapeDtypeStruct((bsz, n_tiles * TM, d), F32),
        scratch_shapes=[pltpu.VMEM((TOP_K, TM, d), F32), pltpu.SemaphoreType.DMA(())],
        compiler_params=_cparams(("arbitrary", "arbitrary")),
        name="moe_combine_final" if final else "moe_combine",
    )(*args)


def _route_tables(route_e, n_experts, s_eff):
    bsz = route_e.shape[0]
    s = s_eff
    n_assign = bsz * s * TOP_K
    flat_e = route_e[:, :s, :TOP_K].reshape(n_assign)
    onehot = (flat_e[:, None] == jnp.arange(n_experts, dtype=jnp.int32)[None, :]).astype(jnp.int32)
    csum = jnp.cumsum(onehot, axis=0)
    rank = jnp.sum(csum * onehot, axis=1) - 1
    counts = csum[-1]
    padded = (counts + TMB - 1) // TMB * TMB
    pad_ends = jnp.cumsum(padded)
    pad_starts = pad_ends - padded
    dest = pad_starts[flat_e] + rank
    n_blk = (n_assign + TMB - 1) // TMB + n_experts
    n_pad = n_blk * TMB
    blk_e = jnp.minimum(
        jnp.searchsorted(pad_ends, jnp.arange(n_blk, dtype=jnp.int32) * TMB, side='right'),
        n_experts - 1).astype(jnp.int32)
    n_used = (pad_ends[-1] // TMB).astype(jnp.int32).reshape(1)
    pos = dest.astype(jnp.int32).reshape(bsz * s // TM, 1, TOP_K * TM)
    return blk_e, n_used, pos, n_pad


def _final_kernel(x_ref, g_ref, o_ref):
    o_ref[0] = _rms(x_ref[0], g_ref[...])


def _final_norm(xt, g, *, n_tiles):
    bsz, s, d = xt.shape
    tile = lambda b, i: (b, i, 0)
    return pl.pallas_call(
        _final_kernel,
        grid=(bsz, n_tiles),
        in_specs=[pl.BlockSpec((1, TM, d), tile), pl.BlockSpec(g.shape, lambda b, i: (0, 0))],
        out_specs=pl.BlockSpec((1, TM, d), tile),
        out_shape=jax.ShapeDtypeStruct((bsz, n_tiles * TM, d), F32),
        compiler_params=_cparams(("arbitrary", "arbitrary")),
        name="final_norm",
    )(xt, g)


def _rope_partner(r):
    return jnp.concatenate([-r[..., 8:16], r[..., 0:8], -r[..., 24:32], r[..., 16:24]], axis=-1)


def _rope_tables(l_lat, s_total, d_rope, d_nope):
    t = jnp.arange(l_lat, dtype=jnp.int32)
    row = (t // GRID_W).astype(F32)
    col = (t % GRID_W).astype(F32)
    half = d_rope // 2
    inv_freq = ROPE_BASE ** (-jnp.arange(0, half, 2, dtype=F32) / half)
    ang_r = row[:, None] * inv_freq
    ang_c = col[:, None] * inv_freq
    cos = jnp.concatenate([jnp.cos(ang_r), jnp.cos(ang_r), jnp.cos(ang_c), jnp.cos(ang_c)], axis=1)
    sin = jnp.concatenate([jnp.sin(ang_r), jnp.sin(ang_r), jnp.sin(ang_c), jnp.sin(ang_c)], axis=1)
    n_ctx = s_total - l_lat
    cosk = jnp.concatenate([cos, jnp.ones((n_ctx, d_rope), F32)], axis=0)
    sink = jnp.concatenate([sin, jnp.zeros((n_ctx, d_rope), F32)], axis=0)
    pad = LANES - d_nope - d_rope
    cosq = jnp.concatenate([jnp.ones((s_total, d_nope), F32), cosk, jnp.zeros((s_total, pad), F32)], axis=1)
    sinq = jnp.concatenate([jnp.zeros((s_total, d_nope), F32), sink, jnp.zeros((s_total, pad), F32)], axis=1)
    return cosq, sinq, cosk, sink


def _block_diag(w):
    h, a, b = w.shape
    eye = jnp.eye(h, dtype=w.dtype)
    return (eye[:, None, :, None] * w[:, :, None, :]).reshape(h * a, h * b)


def kernel(x, c, ctx, c_ctx, w_mod, b_mod, norm1_g, norm2_g, w_in, w_out, sgu_ln_g, sgu_ln_b, sgu_w, sgu_b, conv_w, conv_b, lru_w_a, lru_b_a, lru_w_x, lru_b_x, lru_lam, mla_q_norm, mla_w_uq, mla_kv_norm, mla_w_uk, mla_w_uv, ffn_w1, ffn_w3, ffn_w2, moe_router, moe_w1, moe_w3, moe_w2, final_norm_g):
    bsz, l_lat, d = x.shape
    l_ctx = ctx.shape[1]
    s_total = l_lat + l_ctx
    depth = w_mod.shape[0]
    d_a = sgu_ln_g.shape[-1]
    h_a, chunk = sgu_w.shape[1], sgu_w.shape[2]
    d_b = conv_w.shape[-1]
    q_lora = mla_q_norm.shape[-1]
    kv_lora = mla_kv_norm.shape[-1]
    d_c = mla_w_uv.shape[-1]
    d_v = d_c // H_C
    d_nope = mla_w_uk.shape[-1] // H_C
    d_qk = mla_w_uq.shape[-1] // H_C
    d_rope = d_qk - d_nope
    n_experts = moe_router.shape[-1]
    assert l_lat % TM == 0 and l_ctx % TM == 0 and TM % chunk == 0 and TM == T_SCAN
    assert d_qk <= LANES and 2 * d_v == LANES and H_C % 2 == 0 and d_rope == 32
    n_lat_tiles = l_lat // TM
    n_ctx_tiles = l_ctx // TM
    dims = (d, d_a, d_b, q_lora, kv_lora, d_rope, h_a, chunk)

    xt = jnp.concatenate([x, ctx], axis=1)

    n_rows = (bsz + 1 + 7) // 8 * 8
    cond = jnp.zeros((n_rows, d), F32).at[:bsz].set(c).at[bsz].set(c_ctx)
    mods_all = _modulation(cond, w_mod.astype(BF16), b_mod[:, None, :])
    mods_all = mods_all[:, :bsz + 1, None, :]

    cosq, sinq, cosk, sink = _rope_tables(l_lat, s_total, d_rope, d_nope)
    q_scale = float(d_qk) ** -0.5 * 1.4426950408889634
    cosq, sinq = cosq * q_scale, sinq * q_scale
    head_pad = LANES - d_qk
    vone = jnp.tile((jnp.arange(LANES) == d_v).astype(F32)[None], (1, H_C))
    e_head = jnp.concatenate([jnp.zeros((d_rope, d_nope), F32), jnp.eye(d_rope, dtype=F32),
                              jnp.zeros((d_rope, head_pad), F32)], axis=1)
    emat = jnp.tile(e_head, (1, H_C)).astype(BF16)

    out = None
    for l in range(depth):
        last = l == depth - 1
        mods = mods_all[l]
        o_m = 2 * d_a + 2 * d_b
        o_r = o_m + q_lora + kv_lora
        w_rope = w_in[l][:, o_r:o_r + d_rope]
        n_in = (o_r + 2 * d_rope + LANES - 1) // LANES * LANES
        win = jnp.concatenate([w_in[l][:, :o_r + d_rope], _rope_partner(w_rope),
                               jnp.zeros((d, n_in - o_r - 2 * d_rope), F32)], axis=1).astype(BF16)
        wq = mla_w_uq[l].reshape(q_lora, H_C, d_qk)
        zq = jnp.zeros((q_lora, H_C, head_pad), F32)
        wuq = jnp.concatenate([wq, zq], axis=-1).reshape(q_lora, H_C * LANES).astype(BF16)
        wuqp = jnp.concatenate([jnp.zeros((q_lora, H_C, d_nope), F32), _rope_partner(wq[..., d_nope:]), zq],
                               axis=-1).reshape(q_lora, H_C * LANES).astype(BF16)
        wk = mla_w_uk[l].reshape(kv_lora, H_C, d_nope)
        wuk = jnp.concatenate([wk, jnp.zeros((kv_lora, H_C, LANES - d_nope), F32)],
                              axis=-1).reshape(kv_lora, H_C * LANES).astype(BF16)
        wv = mla_w_uv[l].reshape(kv_lora, H_C, d_v)
        wuv = jnp.concatenate([wv, jnp.zeros((kv_lora, H_C, LANES - d_v), F32)],
                              axis=-1).reshape(kv_lora, H_C * LANES).astype(BF16)
        bs_full = jnp.repeat(sgu_b[l].T, d_a // h_a, axis=1)

        ya, gg, xb, q, k, v = _premix(
            xt, mods, norm1_g[l][None], win, sgu_ln_g[l][None], sgu_ln_b[l][None],
            sgu_w[l].astype(BF16), bs_full, mla_q_norm[l][None], wuq, wuqp, mla_kv_norm[l][None],
            wuk, emat, wuv, vone, cosq, sinq, cosk, sink,
            n_lat_tiles=n_lat_tiles, dims=dims)

        wg = jnp.concatenate([_block_diag(lru_w_a[l, 0]), _block_diag(lru_w_x[l, 0]),
                              _block_diag(lru_w_a[l, 1]), _block_diag(lru_w_x[l, 1])], axis=1).astype(BF16)
        bg = jnp.concatenate([lru_b_a[l, 0], lru_b_x[l, 0], lru_b_a[l, 1], lru_b_x[l, 1]])[None]
        hs = _lru(xb, conv_w[l], conv_b[l][None], wg, bg, lru_lam[l], n_lat=n_lat_tiles, n_ctx=n_ctx_tiles)

        yc = _attention(q, k, v, l_lat=l_lat, d_v=d_v)

        if l % 2 == 0:
            i = l // 2
            x1, h2 = _postmix(xt, mods, ya, gg, hs, yc, w_out[l].astype(BF16), norm2_g[l][None], None,
                              n_lat_tiles=n_lat_tiles, n_experts=n_experts)
            xt = _ffn(x1, h2, mods, ffn_w1[i].astype(BF16), ffn_w3[i].astype(BF16), ffn_w2[i].astype(BF16),
                      n_lat_tiles=n_lat_tiles)
            if last:
                out = _final_norm(xt, final_norm_g[None], n_tiles=n_lat_tiles)
        else:
            i = l // 2
            wr = jnp.concatenate([moe_router[i], jnp.zeros((d, LANES - n_experts), F32)], axis=1)
            x1, h2, route_e, route_g = _postmix(xt, mods, ya, gg, hs, yc, w_out[l].astype(BF16),
                                                norm2_g[l][None], wr,
                                                n_lat_tiles=n_lat_tiles, n_experts=n_experts)
            n_tiles = n_lat_tiles if last else n_lat_tiles + n_ctx_tiles
            blk_e, n_used, pos, n_pad = _route_tables(route_e, n_experts, n_tiles * TM)
            xs = _dispatch(pos, h2, n_pad, n_tiles=n_tiles)
            yp = _experts(blk_e, n_used, xs, moe_w1[i], moe_w3[i], moe_w2[i])
            out_or_xt = _combine(pos, x1, mods, route_g, yp, final_norm_g[None] if last else None,
                                 n_lat_tiles=n_lat_tiles, n_tiles=n_tiles)
            if last:
                out = out_or_xt
            else:
                xt = out_or_xt
    return out
```

```python
import functools

import jax
import jax.numpy as jnp
from jax import lax
from jax.experimental import pallas as pl
from jax.experimental.pallas import tpu as pltpu

F32 = jnp.float32
BF16 = jnp.bfloat16

EPS = 1e-6
GRID_W = 64
ROPE_BASE = 10000.0
LRU_C = 8.0
H_C = 8
TOP_K = 2
LANES = 128
TM = 256
T_SCAN = 256
KC = 1024
TMB = 1024
TF = 512
VMEM_LIMIT = 56 * 1024 * 1024


def _cparams(sem):
    return pltpu.CompilerParams(dimension_semantics=sem, vmem_limit_bytes=VMEM_LIMIT)


def _rms(x, g):
    return x * lax.rsqrt(jnp.mean(x * x, axis=-1, keepdims=True) + EPS) * g


def _mod_kernel(c_ref, w_ref, b_ref, o_ref):
    c = c_ref[...]
    a = (c * jax.nn.sigmoid(c)).astype(BF16)
    o_ref[0] = jnp.dot(a, w_ref[0], preferred_element_type=F32) + b_ref[0]


def _modulation(cond, w_mod, b_mod):
    depth, d, n = w_mod.shape
    r = cond.shape[0]
    tn = 1024
    return pl.pallas_call(
        _mod_kernel,
        grid=(depth, n // tn),
        in_specs=[pl.BlockSpec((r, d), lambda l, j: (0, 0)),
                  pl.BlockSpec((1, d, tn), lambda l, j: (l, 0, j)),
                  pl.BlockSpec((1, 1, tn), lambda l, j: (l, 0, j))],
        out_specs=pl.BlockSpec((1, r, tn), lambda l, j: (l, 0, j)),
        out_shape=jax.ShapeDtypeStruct((depth, r, n), F32),
        compiler_params=_cparams(("arbitrary", "arbitrary")),
        name="modulation",
    )(cond, w_mod, b_mod)


def _premix_kernel(x_ref, mod_ref, g_ref, win_ref, lng_ref, lnb_ref, ws_ref, bs_ref,
                   qn_ref, wuq_ref, wuqp_ref, kvn_ref, wuk_ref, e_ref, wuv_ref, vone_ref,
                   cq_ref, sq_ref, ck_ref, sk_ref,
                   ya_ref, gg_ref, xb_ref, q_ref, k_ref, v_ref, *, dims):
    d, d_a, d_b, q_lora, kv_lora, d_rope, h_a, chunk = dims
    x = x_ref[0]
    shift = mod_ref[0, :, 0:d]
    scale = mod_ref[0, :, d:2 * d]
    h = (_rms(x, g_ref[...]) * (1.0 + scale) + shift).astype(BF16)
    z = jnp.dot(h, win_ref[...], preferred_element_type=F32)

    o = 0
    u = jax.nn.gelu(z[:, o:o + d_a])
    v = jax.nn.gelu(z[:, o + d_a:o + 2 * d_a])
    mu = jnp.mean(v, axis=-1, keepdims=True)
    vc = v - mu
    var = jnp.mean(vc * vc, axis=-1, keepdims=True)
    vn = (vc * lax.rsqrt(var + EPS) * lng_ref[...] + lnb_ref[...]).astype(BF16)
    dh_a = d_a // h_a
    tm = x.shape[0]
    head_of_lane = lax.broadcasted_iota(jnp.int32, (chunk, d_a), 1) // dh_a
    for c in range(tm // chunk):
        vch = vn[c * chunk:(c + 1) * chunk]
        s = jnp.dot(ws_ref[0], vch, preferred_element_type=F32)
        for hd in range(1, h_a):
            s = jnp.where(head_of_lane == hd,
                          jnp.dot(ws_ref[hd], vch, preferred_element_type=F32), s)
        s = s + bs_ref[...]
        ya_ref[0, c * chunk:(c + 1) * chunk, :] = u[c * chunk:(c + 1) * chunk] * s

    o = 2 * d_a
    gg_ref[0] = jax.nn.gelu(z[:, o:o + d_b])
    xb_ref[0] = z[:, o + d_b:o + 2 * d_b]

    o = 2 * d_a + 2 * d_b
    cq = _rms(z[:, o:o + q_lora], qn_ref[...]).astype(BF16)
    qa = jnp.dot(cq, wuq_ref[...], preferred_element_type=F32)
    qb = jnp.dot(cq, wuqp_ref[...], preferred_element_type=F32)
    cos_q = jnp.concatenate([cq_ref[...]] * H_C, axis=1)
    sin_q = jnp.concatenate([sq_ref[...]] * H_C, axis=1)
    q_ref[0] = (qa * cos_q + qb * sin_q).astype(BF16)
    o += q_lora
    ckv = _rms(z[:, o:o + kv_lora], kvn_ref[...]).astype(BF16)
    o += kv_lora
    zr = z[:, o:o + d_rope]
    zrp = z[:, o + d_rope:o + 2 * d_rope]
    kr = (zr * ck_ref[...] + zrp * sk_ref[...]).astype(BF16)
    kn = jnp.dot(ckv, wuk_ref[...], preferred_element_type=F32)
    k_ref[0] = (kn + jnp.dot(kr, e_ref[...], preferred_element_type=F32)).astype(BF16)
    v_ref[0] = (jnp.dot(ckv, wuv_ref[...], preferred_element_type=F32) + vone_ref[...]).astype(BF16)


def _tile_mod_index(n_lat_tiles, n_batch):
    def index(b, i):
        return (jnp.where(i < n_lat_tiles, b, n_batch), 0, 0)
    return index


def _premix(xt, mods, g1, win, lng, lnb, ws, bs, qn, wuq, wuqp, kvn, wuk, emat, wuv, vone,
            cosq, sinq, cosk, sink, *, n_lat_tiles, dims):
    bsz, s, d = xt.shape
    d_a, d_b = dims[1], dims[2]
    hp = H_C * LANES
    const2 = lambda b, i: (0, 0)
    const3 = lambda b, i: (0, 0, 0)
    tile = lambda b, i: (b, i, 0)
    full = lambda a: pl.BlockSpec(a.shape, const2 if a.ndim == 2 else const3)
    return pl.pallas_call(
        functools.partial(_premix_kernel, dims=dims),
        grid=(bsz, s // TM),
        in_specs=[pl.BlockSpec((1, TM, d), tile),
                  pl.BlockSpec((1, 1, mods.shape[-1]), _tile_mod_index(n_lat_tiles, bsz)),
                  full(g1), full(win), full(lng), full(lnb), full(ws), full(bs),
                  full(qn), full(wuq), full(wuqp), full(kvn), full(wuk), full(emat), full(wuv), full(vone),
                  pl.BlockSpec((TM, LANES), lambda b, i: (i, 0)),
                  pl.BlockSpec((TM, LANES), lambda b, i: (i, 0)),
                  pl.BlockSpec((TM, cosk.shape[1]), lambda b, i: (i, 0)),
                  pl.BlockSpec((TM, sink.shape[1]), lambda b, i: (i, 0))],
        out_specs=[pl.BlockSpec((1, TM, d_a), tile), pl.BlockSpec((1, TM, d_b), tile),
                   pl.BlockSpec((1, TM, d_b), tile), pl.BlockSpec((1, TM, hp), tile),
                   pl.BlockSpec((1, TM, hp), tile), pl.BlockSpec((1, TM, hp), tile)],
        out_shape=[jax.ShapeDtypeStruct((bsz, s, d_a), F32),
                   jax.ShapeDtypeStruct((bsz, s, d_b), F32),
                   jax.ShapeDtypeStruct((bsz, s, d_b), F32),
                   jax.ShapeDtypeStruct((bsz, s, hp), BF16),
                   jax.ShapeDtypeStruct((bsz, s, hp), BF16),
                   jax.ShapeDtypeStruct((bsz, s, hp), BF16)],
        compiler_params=_cparams(("arbitrary", "arbitrary")),
        name="premix",
    )(xt, mods, g1, win, lng, lnb, ws, bs, qn, wuq, wuqp, kvn, wuk, emat, wuv, vone,
      cosq, sinq, cosk, sink)


def _scan_tile(a, b, carry, reverse):
    t = a.shape[0]
    rows = lax.broadcasted_iota(jnp.int32, a.shape, 0)
    s = 1
    while s < t:
        if reverse:
            a_sh = pltpu.roll(a, t - s, 0)
            b_sh = pltpu.roll(b, t - s, 0)
            ok = rows < t - s
        else:
            a_sh = pltpu.roll(a, s, 0)
            b_sh = pltpu.roll(b, s, 0)
            ok = rows >= s
        b = b + a * jnp.where(ok, b_sh, 0.0)
        a = a * jnp.where(ok, a_sh, 1.0)
        s *= 2
    h = b + a * carry
    return h, (h[0:1] if reverse else h[t - 1:t])


def _lru_kernel(xb_ref, cw_ref, cb_ref, wg_ref, bg_ref, lam_ref, out_ref, xc_ref, *, n_lat, n_ctx):
    t = T_SCAN
    n_tiles = n_lat + n_ctx
    s_total = n_tiles * t
    d_b = xb_ref.shape[-1]
    w = cw_ref[...]
    cb = cb_ref[...]

    def conv_body(j, _):
        t0 = pl.multiple_of(j * t, t)
        is_ctx = j >= n_lat
        seq_lo = jnp.where(is_ctx, n_lat * t, 0)
        seq_hi = jnp.where(is_ctx, s_total, n_lat * t)
        cur = xb_ref[0, pl.ds(t0, t), :]
        p0 = pl.multiple_of(jnp.maximum(t0 - 8, 0), 8)
        n0 = pl.multiple_of(jnp.minimum(t0 + t, s_total - 8), 8)
        prev = jnp.where(t0 > seq_lo, xb_ref[0, pl.ds(p0, 8), :], 0.0)
        nxt = jnp.where(t0 + t < seq_hi, xb_ref[0, pl.ds(n0, 8), :], 0.0)
        ext = jnp.concatenate([prev, cur, nxt], axis=0)
        n_ext = t + 16
        xm2 = pltpu.roll(ext, 2, 0)[8:8 + t]
        xm1 = pltpu.roll(ext, 1, 0)[8:8 + t]
        xp1 = pltpu.roll(ext, n_ext - 1, 0)[8:8 + t]
        xc_ref[pl.ds(t0, t), :] = (w[0:1] * xm2 + w[1:2] * xm1 + w[2:3] * cur + w[3:4] * xp1 + cb)
        return 0

    lax.fori_loop(0, n_tiles, conv_body, 0)

    lam = lam_ref[...]
    neg = -lam
    softplus = jnp.maximum(neg, 0.0) + jnp.log1p(jnp.exp(-jnp.abs(neg)))

    def direction(dr, reverse):
        sp = softplus[dr:dr + 1]
        wg = wg_ref[:, dr * 2 * d_b:(dr + 1) * 2 * d_b]
        bg = bg_ref[:, dr * 2 * d_b:(dr + 1) * 2 * d_b]

        def body(j, carry):
            if reverse:
                idx = jnp.where(j < n_ctx, n_tiles - 1 - j, n_lat - 1 - (j - n_ctx))
            else:
                idx = jnp.where(j < n_ctx, n_lat + j, j - n_ctx)
            t0 = pl.multiple_of(idx * t, t)
            xc = xc_ref[pl.ds(t0, t), :]
            g = jnp.dot(xc.astype(BF16), wg, preferred_element_type=F32) + bg
            r = jax.nn.sigmoid(g[:, 0:d_b])
            ig = jax.nn.sigmoid(g[:, d_b:2 * d_b])
            log_a = (-LRU_C * r) * sp
            a = jnp.exp(log_a)
            bv = jnp.sqrt(-jnp.tanh(log_a) * (a * a + 1.0)) * (ig * xc)
            h, carry = _scan_tile(a, bv, carry, reverse)
            if reverse:
                out_ref[0, pl.ds(t0, t), :] = out_ref[0, pl.ds(t0, t), :] + h
            else:
                out_ref[0, pl.ds(t0, t), :] = h
            return carry

        lax.fori_loop(0, n_tiles, body, jnp.zeros((1, d_b), F32))

    direction(0, False)
    direction(1, True)


def _lru(xb, cw, cb, wg, bg, lam, *, n_lat, n_ctx):
    bsz, s, d_b = xb.shape
    const2 = lambda b: (0, 0)
    full = lambda a: pl.BlockSpec(a.shape, const2)
    return pl.pallas_call(
        functools.partial(_lru_kernel, n_lat=n_lat, n_ctx=n_ctx),
        grid=(bsz,),
        in_specs=[pl.BlockSpec((1, s, d_b), lambda b: (b, 0, 0)),
                  full(cw), full(cb), full(wg), full(bg), full(lam)],
        out_specs=pl.BlockSpec((1, s, d_b), lambda b: (b, 0, 0)),
        out_shape=jax.ShapeDtypeStruct((bsz, s, d_b), F32),
        scratch_shapes=[pltpu.VMEM((s, d_b), F32)],
        compiler_params=_cparams(("arbitrary",)),
        name="rglru",
    )(xb, cw, cb, wg, bg, lam)


def _attn_kernel(q_ref, k_ref, v_ref, o_ref, *, n_lat_tiles, chunks_lat, chunks_ctx, d_v):
    i = pl.program_id(2)
    tq = q_ref.shape[1]

    def compute(chunks):
        units = [(hh, lo, n) for hh in range(2) for lo, n in chunks]

        def scores(hh, lo, n):
            qh = q_ref[0, :, hh * LANES:(hh + 1) * LANES]
            kh = k_ref[0, lo:lo + n, hh * LANES:(hh + 1) * LANES]
            return lax.dot_general(qh, kh, (((1,), (1,)), ((), ())), preferred_element_type=F32)

        outs = []
        m = acc = None
        s_next = scores(*units[0])
        for ui, (hh, lo, n) in enumerate(units):
            s = s_next
            if ui + 1 < len(units):
                s_next = scores(*units[ui + 1])
            if lo == chunks[0][0]:
                m = acc = None
            mc = jnp.max(s, axis=-1, keepdims=True)
            m_new = mc if m is None else jnp.maximum(m, mc)
            p = jnp.exp2(s - m_new).astype(BF16)
            pv = jnp.dot(p, v_ref[0, lo:lo + n, hh * LANES:(hh + 1) * LANES],
                         preferred_element_type=F32)
            acc = pv if m is None else jnp.exp2(m - m_new) * acc + pv
            m = m_new
            if lo == chunks[-1][0]:
                outs.append(acc / acc[:, d_v:d_v + 1])
        lane = lax.broadcasted_iota(jnp.int32, (tq, LANES), 1)
        o_ref[0] = jnp.where(lane < d_v, outs[0], pltpu.roll(outs[1], d_v, 1)).astype(o_ref.dtype)

    @pl.when(i < n_lat_tiles)
    def _():
        compute(chunks_lat + chunks_ctx)

    @pl.when(i >= n_lat_tiles)
    def _():
        compute(chunks_ctx)


def _key_chunks(lo, hi, size):
    return tuple((a, min(size, hi - a)) for a in range(lo, hi, size))


def _attention(q, k, v, *, l_lat, d_v):
    bsz, s, hp = q.shape
    d_c = H_C * d_v
    n_pairs = H_C // 2
    kern = functools.partial(_attn_kernel, n_lat_tiles=l_lat // TM,
                             chunks_lat=_key_chunks(0, l_lat, KC), chunks_ctx=_key_chunks(l_lat, s, KC),
                             d_v=d_v)
    return pl.pallas_call(
        kern,
        grid=(bsz, n_pairs, s // TM),
        in_specs=[pl.BlockSpec((1, TM, 2 * LANES), lambda b, h, i: (b, i, h)),
                  pl.BlockSpec((1, s, 2 * LANES), lambda b, h, i: (b, 0, h)),
                  pl.BlockSpec((1, s, 2 * LANES), lambda b, h, i: (b, 0, h))],
        out_specs=pl.BlockSpec((1, TM, 2 * d_v), lambda b, h, i: (b, i, h)),
        out_shape=jax.ShapeDtypeStruct((bsz, s, d_c), BF16),
        compiler_params=_cparams(("arbitrary", "arbitrary", "arbitrary")),
        name="attention",
    )(q, k, v)


def _postmix_kernel(x_ref, mod_ref, ya_ref, gg_ref, hs_ref, yc_ref, wout_ref, g2_ref, *rest,
                    d, n_experts, route):
    if route:
        wr_ref, x1_ref, h2_ref, re_ref, rg_ref = rest
    else:
        x1_ref, h2_ref = rest
    gate1 = mod_ref[0, :, 2 * d:3 * d]
    shift2 = mod_ref[0, :, 3 * d:4 * d]
    scale2 = mod_ref[0, :, 4 * d:5 * d]
    y = jnp.concatenate([ya_ref[0].astype(BF16), (gg_ref[0] * hs_ref[0]).astype(BF16), yc_ref[0]],
                        axis=1)
    x1 = x_ref[0] + gate1 * jnp.dot(y, wout_ref[...], preferred_element_type=F32)
    x1_ref[0] = x1
    h2 = _rms(x1, g2_ref[...]) * (1.0 + scale2) + shift2
    h2_ref[0] = h2.astype(h2_ref.dtype)
    if route:
        logits = jnp.dot(h2, wr_ref[...], preferred_element_type=F32)
        lane = lax.broadcasted_iota(jnp.int32, logits.shape, 1)
        neg_inf = jnp.float32(-jnp.inf)
        lg = jnp.where(lane < n_experts, logits, neg_inf)
        m1 = jnp.max(lg, axis=-1, keepdims=True)
        i1 = jnp.min(jnp.where(lg == m1, lane, LANES), axis=-1, keepdims=True)
        lg2 = jnp.where(lane == i1, neg_inf, lg)
        m2 = jnp.max(lg2, axis=-1, keepdims=True)
        i2 = jnp.min(jnp.where(lg2 == m2, lane, LANES), axis=-1, keepdims=True)
        e = jnp.exp(m2 - m1)
        den = 1.0 + e
        re_ref[0] = jnp.where(lane == 0, i1, i2)
        rg_ref[0] = jnp.where(lane == 0, 1.0 / den, e / den)


def _postmix(xt, mods, ya, gg, hs, yc, wout, g2, wr, *, n_lat_tiles, n_experts):
    bsz, s, d = xt.shape
    route = wr is not None
    tile = lambda b, i: (b, i, 0)
    const2 = lambda b, i: (0, 0)
    full = lambda a: pl.BlockSpec(a.shape, const2)
    in_specs = [pl.BlockSpec((1, TM, d), tile),
                pl.BlockSpec((1, 1, mods.shape[-1]), _tile_mod_index(n_lat_tiles, bsz)),
                pl.BlockSpec((1, TM, ya.shape[-1]), tile), pl.BlockSpec((1, TM, gg.shape[-1]), tile),
                pl.BlockSpec((1, TM, hs.shape[-1]), tile), pl.BlockSpec((1, TM, yc.shape[-1]), tile),
                full(wout), full(g2)]
    args = [xt, mods, ya, gg, hs, yc, wout, g2]
    out_specs = [pl.BlockSpec((1, TM, d), tile), pl.BlockSpec((1, TM, d), tile)]
    out_shape = [jax.ShapeDtypeStruct((bsz, s, d), F32),
                 jax.ShapeDtypeStruct((bsz, s, d), F32 if route else BF16)]
    if route:
        in_specs.append(full(wr))
        args.append(wr)
        out_specs += [pl.BlockSpec((1, TM, LANES), tile), pl.BlockSpec((1, TM, LANES), tile)]
        out_shape += [jax.ShapeDtypeStruct((bsz, s, LANES), jnp.int32),
                      jax.ShapeDtypeStruct((bsz, s, LANES), F32)]
    return pl.pallas_call(
        functools.partial(_postmix_kernel, d=d, n_experts=n_experts, route=route),
        grid=(bsz, s // TM),
        in_specs=in_specs, out_specs=out_specs, out_shape=out_shape,
        compiler_params=_cparams(("arbitrary", "arbitrary")),
        name="postmix_route" if route else "postmix",
    )(*args)


def _ffn_kernel(x1_ref, h2_ref, mod_ref, w1_ref, w3_ref, w2_ref, o_ref, *, d, fc):
    h = h2_ref[0]
    d_ff = w1_ref.shape[1]
    acc = jnp.zeros((h.shape[0], d), F32)
    for c in range(d_ff // fc):
        a = jnp.dot(h, w1_ref[:, c * fc:(c + 1) * fc], preferred_element_type=F32)
        b = jnp.dot(h, w3_ref[:, c * fc:(c + 1) * fc], preferred_element_type=F32)
        act = (a * jax.nn.sigmoid(a) * b).astype(BF16)
        acc = acc + jnp.dot(act, w2_ref[c * fc:(c + 1) * fc, :], preferred_element_type=F32)
    gate2 = mod_ref[0, :, 5 * d:6 * d]
    o_ref[0] = x1_ref[0] + gate2 * acc


def _ffn(x1, h2, mods, w1, w3, w2, *, n_lat_tiles):
    bsz, s, d = x1.shape
    tile = lambda b, i: (b, i, 0)
    const2 = lambda b, i: (0, 0)
    full = lambda a: pl.BlockSpec(a.shape, const2)
    return pl.pallas_call(
        functools.partial(_ffn_kernel, d=d, fc=256),
        grid=(bsz, s // TM),
        in_specs=[pl.BlockSpec((1, TM, d), tile), pl.BlockSpec((1, TM, d), tile),
                  pl.BlockSpec((1, 1, mods.shape[-1]), _tile_mod_index(n_lat_tiles, bsz)),
                  full(w1), full(w3), full(w2)],
        out_specs=pl.BlockSpec((1, TM, d), tile),
        out_shape=jax.ShapeDtypeStruct((bsz, s, d), F32),
        compiler_params=_cparams(("arbitrary", "arbitrary")),
        name="ffn",
    )(x1, h2, mods, w1, w3, w2)


DMA_UNROLL = 8


def _row_copy(src_ref, dst_ref, sem, src_row, dst_row):
    return pltpu.make_async_copy(src_ref.at[pl.ds(src_row, 1)], dst_ref.at[pl.ds(dst_row, 1)], sem)


def _dispatch_kernel(pos_ref, h_ref, xs_in_hbm, xs_hbm, sem):
    del xs_in_hbm
    n = h_ref.shape[1]

    def start(j, _):
        for kk in range(TOP_K):
            _row_copy(h_ref.at[0], xs_hbm, sem, j, pos_ref[0, 0, TOP_K * j + kk]).start()
        return 0

    def wait(j, _):
        for kk in range(TOP_K):
            _row_copy(h_ref.at[0], xs_hbm, sem, j, 0).wait()
        return 0

    lax.fori_loop(0, n, start, 0, unroll=DMA_UNROLL)
    lax.fori_loop(0, n, wait, 0, unroll=DMA_UNROLL)


def _dispatch(pos, h2, n_pad, *, n_tiles):
    bsz, s, d = h2.shape
    xs0 = jnp.zeros((n_pad, d), h2.dtype)
    return pl.pallas_call(
        _dispatch_kernel,
        grid=(bsz, n_tiles),
        in_specs=[pl.BlockSpec((1, 1, TOP_K * TM), lambda b, i: (b * n_tiles + i, 0, 0),
                               memory_space=pltpu.SMEM),
                  pl.BlockSpec((1, TM, d), lambda b, i: (b, i, 0)),
                  pl.BlockSpec(memory_space=pl.ANY)],
        out_specs=pl.BlockSpec(memory_space=pl.ANY),
        out_shape=jax.ShapeDtypeStruct((n_pad, d), h2.dtype),
        scratch_shapes=[pltpu.SemaphoreType.DMA(())],
        input_output_aliases={2: 0},
        compiler_params=_cparams(("arbitrary", "arbitrary")),
        name="moe_dispatch",
    )(pos, h2, xs0)


def _experts_kernel(be_ref, nu_ref, xs_ref, w1_ref, w3_ref, w2_ref, o_ref, xb_ref, acc_ref):
    r = pl.program_id(0)
    f = pl.program_id(1)
    nf = pl.num_programs(1)

    @pl.when(r < nu_ref[0])
    def _():
        @pl.when(f == 0)
        def _():
            xb_ref[...] = xs_ref[...].astype(BF16)
            acc_ref[...] = jnp.zeros_like(acc_ref)

        xb = xb_ref[...]
        a = jnp.dot(xb, w1_ref[0].astype(BF16), preferred_element_type=F32)
        b = jnp.dot(xb, w3_ref[0].astype(BF16), preferred_element_type=F32)
        act = (a * jax.nn.sigmoid(a) * b).astype(BF16)
        acc_ref[...] += jnp.dot(act, w2_ref[0].astype(BF16), preferred_element_type=F32)

        @pl.when(f == nf - 1)
        def _():
            o_ref[...] = acc_ref[...]

    @pl.when(jnp.logical_and(r >= nu_ref[0], f == nf - 1))
    def _():
        o_ref[...] = jnp.zeros_like(o_ref)


def _experts(blk_e, n_used, xs, w1, w3, w2):
    n_pad, d = xs.shape
    n_blk = n_pad // TMB
    d_ff = w1.shape[-1]
    nf = d_ff // TF

    def w_col(r, f, be, nu):
        live = r < nu[0]
        return (be[r], 0, jnp.where(live, f, nf - 1))

    def w_row(r, f, be, nu):
        live = r < nu[0]
        return (be[r], jnp.where(live, f, nf - 1), 0)

    grid_spec = pltpu.PrefetchScalarGridSpec(
        num_scalar_prefetch=2,
        grid=(n_blk, nf),
        in_specs=[pl.BlockSpec((TMB, d), lambda r, f, be, nu: (r, 0)),
                  pl.BlockSpec((1, d, TF), w_col),
                  pl.BlockSpec((1, d, TF), w_col),
                  pl.BlockSpec((1, TF, d), w_row)],
        out_specs=pl.BlockSpec((TMB, d), lambda r, f, be, nu: (r, 0)),
        scratch_shapes=[pltpu.VMEM((TMB, d), BF16), pltpu.VMEM((TMB, d), F32)])
    return pl.pallas_call(
        _experts_kernel,
        grid_spec=grid_spec,
        out_shape=jax.ShapeDtypeStruct((n_pad, d), F32),
        compiler_params=_cparams(("arbitrary", "arbitrary")),
        name="moe_experts",
    )(blk_e, n_used, xs, w1, w3, w2)


def _combine_kernel(pos_ref, x1_ref, mod_ref, rg_ref, yp_hbm, *rest, d, final):
    if final:
        gf_ref, o_ref, buf_ref, sem = rest
    else:
        o_ref, buf_ref, sem = rest
    n = x1_ref.shape[1]

    def start(j, _):
        for kk in range(TOP_K):
            _row_copy(yp_hbm, buf_ref.at[kk], sem, pos_ref[0, 0, TOP_K * j + kk], j).start()
        return 0

    def wait(j, _):
        for kk in range(TOP_K):
            _row_copy(yp_hbm, buf_ref.at[kk], sem, 0, j).wait()
        return 0

    lax.fori_loop(0, n, start, 0, unroll=DMA_UNROLL)
    lax.fori_loop(0, n, wait, 0, unroll=DMA_UNROLL)
    gate2 = mod_ref[0, :, 5 * d:6 * d]
    rg = rg_ref[0]
    y = rg[:, 0:1] * buf_ref[0]
    for kk in range(1, TOP_K):
        y = y + rg[:, kk:kk + 1] * buf_ref[kk]
    x2 = x1_ref[0] + gate2 * y
    if final:
        x2 = _rms(x2, gf_ref[...])
    o_ref[0] = x2


def _combine(pos, x1, mods, route_g, yp, gf, *, n_lat_tiles, n_tiles):
    bsz, s, d = x1.shape
    final = gf is not None
    tile = lambda b, i: (b, i, 0)
    in_specs = [pl.BlockSpec((1, 1, TOP_K * TM), lambda b, i: (b * n_tiles + i, 0, 0),
                             memory_space=pltpu.SMEM),
                pl.BlockSpec((1, TM, d), tile),
                pl.BlockSpec((1, 1, mods.shape[-1]), _tile_mod_index(n_lat_tiles, bsz)),
                pl.BlockSpec((1, TM, LANES), tile),
                pl.BlockSpec(memory_space=pl.ANY)]
    args = [pos, x1, mods, route_g, yp]
    if final:
        in_specs.append(pl.BlockSpec(gf.shape, lambda b, i: (0, 0)))
        args.append(gf)
    return pl.pallas_call(
        functools.partial(_combine_kernel, d=d, final=final),
        grid=(bsz, n_tiles),
        in_specs=in_specs,
        out_specs=pl.BlockSpec((1, TM, d), tile),
        out_shape=jax.ShapeDtypeStruct((bsz, n_tiles * TM, d), F32),
        scratch_shapes=[pltpu.VMEM((TOP_K, TM, d), F32), pltpu.SemaphoreType.DMA(())],
        compiler_params=_cparams(("arbitrary", "arbitrary")),
        name="moe_combine_final" if final else "moe_combine",
    )(*args)


def _route_tables(route_e, n_experts, s_eff):
    bsz = route_e.shape[0]
    s = s_eff
    n_assign = bsz * s * TOP_K
    flat_e = route_e[:, :s, :TOP_K].reshape(n_assign)
    onehot = (flat_e[:, None] == jnp.arange(n_experts, dtype=jnp.int32)[None, :]).astype(jnp.int32)
    csum = jnp.cumsum(onehot, axis=0)
    rank = jnp.sum(csum * onehot, axis=1) - 1
    counts = csum[-1]
    padded = (counts + TMB - 1) // TMB * TMB
    pad_ends = jnp.cumsum(padded)
    pad_starts = pad_ends - padded
    dest = pad_starts[flat_e] + rank
    n_blk = (n_assign + TMB - 1) // TMB + n_experts
    n_pad = n_blk * TMB
    blk_e = jnp.minimum(
        jnp.searchsorted(pad_ends, jnp.arange(n_blk, dtype=jnp.int32) * TMB, side='right'),
        n_experts - 1).astype(jnp.int32)
    n_used = (pad_ends[-1] // TMB).astype(jnp.int32).reshape(1)
    pos = dest.astype(jnp.int32).reshape(bsz * s // TM, 1, TOP_K * TM)
    return blk_e, n_used, pos, n_pad


def _final_kernel(x_ref, g_ref, o_ref):
    o_ref[0] = _rms(x_ref[0], g_ref[...])


def _final_norm(xt, g, *, n_tiles):
    bsz, s, d = xt.shape
    tile = lambda b, i: (b, i, 0)
    return pl.pallas_call(
        _final_kernel,
        grid=(bsz, n_tiles),
        in_specs=[pl.BlockSpec((1, TM, d), tile), pl.BlockSpec(g.shape, lambda b, i: (0, 0))],
        out_specs=pl.BlockSpec((1, TM, d), tile),
        out_shape=jax.ShapeDtypeStruct((bsz, n_tiles * TM, d), F32),
        compiler_params=_cparams(("arbitrary", "arbitrary")),
        name="final_norm",
    )(xt, g)


def _rope_partner(r):
    return jnp.concatenate([-r[..., 8:16], r[..., 0:8], -r[..., 24:32], r[..., 16:24]], axis=-1)


def _rope_tables(l_lat, s_total, d_rope, d_nope):
    t = jnp.arange(l_lat, dtype=jnp.int32)
    row = (t // GRID_W).astype(F32)
    col = (t % GRID_W).astype(F32)
    half = d_rope // 2
    inv_freq = ROPE_BASE ** (-jnp.arange(0, half, 2, dtype=F32) / half)
    ang_r = row[:, None] * inv_freq
    ang_c = col[:, None] * inv_freq
    cos = jnp.concatenate([jnp.cos(ang_r), jnp.cos(ang_r), jnp.cos(ang_c), jnp.cos(ang_c)], axis=1)
    sin = jnp.concatenate([jnp.sin(ang_r), jnp.sin(ang_r), jnp.sin(ang_c), jnp.sin(ang_c)], axis=1)
    n_ctx = s_total - l_lat
    cosk = jnp.concatenate([cos, jnp.ones((n_ctx, d_rope), F32)], axis=0)
    sink = jnp.concatenate([sin, jnp.zeros((n_ctx, d_rope), F32)], axis=0)
    pad = LANES - d_nope - d_rope
    cosq = jnp.concatenate([jnp.ones((s_total, d_nope), F32), cosk, jnp.zeros((s_total, pad), F32)], axis=1)
    sinq = jnp.concatenate([jnp.zeros((s_total, d_nope), F32), sink, jnp.zeros((s_total, pad), F32)], axis=1)
    return cosq, sinq, cosk, sink


def _block_diag(w):
    h, a, b = w.shape
    eye = jnp.eye(h, dtype=w.dtype)
    return (eye[:, None, :, None] * w[:, :, None, :]).reshape(h * a, h * b)


def kernel(x, c, ctx, c_ctx, w_mod, b_mod, norm1_g, norm2_g, w_in, w_out, sgu_ln_g, sgu_ln_b, sgu_w, sgu_b, conv_w, conv_b, lru_w_a, lru_b_a, lru_w_x, lru_b_x, lru_lam, mla_q_norm, mla_w_uq, mla_kv_norm, mla_w_uk, mla_w_uv, ffn_w1, ffn_w3, ffn_w2, moe_router, moe_w1, moe_w3, moe_w2, final_norm_g):
    bsz, l_lat, d = x.shape
    l_ctx = ctx.shape[1]
    s_total = l_lat + l_ctx
    depth = w_mod.shape[0]
    d_a = sgu_ln_g.shape[-1]
    h_a, chunk = sgu_w.shape[1], sgu_w.shape[2]
    d_b = conv_w.shape[-1]
    q_lora = mla_q_norm.shape[-1]
    kv_lora = mla_kv_norm.shape[-1]
    d_c = mla_w_uv.shape[-1]
    d_v = d_c // H_C
    d_nope = mla_w_uk.shape[-1] // H_C
    d_qk = mla_w_uq.shape[-1] // H_C
    d_rope = d_qk - d_nope
    n_experts = moe_router.shape[-1]
    assert l_lat % TM == 0 and l_ctx % TM == 0 and TM % chunk == 0 and TM == T_SCAN
    assert d_qk <= LANES and 2 * d_v == LANES and H_C % 2 == 0 and d_rope == 32
    n_lat_tiles = l_lat // TM
    n_ctx_tiles = l_ctx // TM
    dims = (d, d_a, d_b, q_lora, kv_lora, d_rope, h_a, chunk)

    xt = jnp.concatenate([x, ctx], axis=1)

    n_rows = (bsz + 1 + 7) // 8 * 8
    cond = jnp.zeros((n_rows, d), F32).at[:bsz].set(c).at[bsz].set(c_ctx)
    mods_all = _modulation(cond, w_mod.astype(BF16), b_mod[:, None, :])
    mods_all = mods_all[:, :bsz + 1, None, :]

    cosq, sinq, cosk, sink = _rope_tables(l_lat, s_total, d_rope, d_nope)
    q_scale = float(d_qk) ** -0.5 * 1.4426950408889634
    cosq, sinq = cosq * q_scale, sinq * q_scale
    head_pad = LANES - d_qk
    vone = jnp.tile((jnp.arange(LANES) == d_v).astype(F32)[None], (1, H_C))
    e_head = jnp.concatenate([jnp.zeros((d_rope, d_nope), F32), jnp.eye(d_rope, dtype=F32),
                              jnp.zeros((d_rope, head_pad), F32)], axis=1)
    emat = jnp.tile(e_head, (1, H_C)).astype(BF16)

    out = None
    for l in range(depth):
        last = l == depth - 1
        mods = mods_all[l]
        o_m = 2 * d_a + 2 * d_b
        o_r = o_m + q_lora + kv_lora
        w_rope = w_in[l][:, o_r:o_r + d_rope]
        n_in = (o_r + 2 * d_rope + LANES - 1) // LANES * LANES
        win = jnp.concatenate([w_in[l][:, :o_r + d_rope], _rope_partner(w_rope),
                               jnp.zeros((d, n_in - o_r - 2 * d_rope), F32)], axis=1).astype(BF16)
        wq = mla_w_uq[l].reshape(q_lora, H_C, d_qk)
        zq = jnp.zeros((q_lora, H_C, head_pad), F32)
        wuq = jnp.concatenate([wq, zq], axis=-1).reshape(q_lora, H_C * LANES).astype(BF16)
        wuqp = jnp.concatenate([jnp.zeros((q_lora, H_C, d_nope), F32), _rope_partner(wq[..., d_nope:]), zq],
                               axis=-1).reshape(q_lora, H_C * LANES).astype(BF16)
        wk = mla_w_uk[l].reshape(kv_lora, H_C, d_nope)
        wuk = jnp.concatenate([wk, jnp.zeros((kv_lora, H_C, LANES - d_nope), F32)],
                              axis=-1).reshape(kv_lora, H_C * LANES).astype(BF16)
        wv = mla_w_uv[l].reshape(kv_lora, H_C, d_v)
        wuv = jnp.concatenate([wv, jnp.zeros((kv_lora, H_C, LANES - d_v), F32)],
                              axis=-1).reshape(kv_lora, H_C * LANES).astype(BF16)
        bs_full = jnp.repeat(sgu_b[l].T, d_a // h_a, axis=1)

        ya, gg, xb, q, k, v = _premix(
            xt, mods, norm1_g[l][None], win, sgu_ln_g[l][None], sgu_ln_b[l][None],
            sgu_w[l].astype(BF16), bs_full, mla_q_norm[l][None], wuq, wuqp, mla_kv_norm[l][None],
            wuk, emat, wuv, vone, cosq, sinq, cosk, sink,
            n_lat_tiles=n_lat_tiles, dims=dims)

        wg = jnp.concatenate([_block_diag(lru_w_a[l, 0]), _block_diag(lru_w_x[l, 0]),
                              _block_diag(lru_w_a[l, 1]), _block_diag(lru_w_x[l, 1])], axis=1).astype(BF16)
        bg = jnp.concatenate([lru_b_a[l, 0], lru_b_x[l, 0], lru_b_a[l, 1], lru_b_x[l, 1]])[None]
        hs = _lru(xb, conv_w[l], conv_b[l][None], wg, bg, lru_lam[l], n_lat=n_lat_tiles, n_ctx=n_ctx_tiles)

        yc = _attention(q, k, v, l_lat=l_lat, d_v=d_v)

        if l % 2 == 0:
            i = l // 2
            x1, h2 = _postmix(xt, mods, ya, gg, hs, yc, w_out[l].astype(BF16), norm2_g[l][None], None,
                              n_lat_tiles=n_lat_tiles, n_experts=n_experts)
            xt = _ffn(x1, h2, mods, ffn_w1[i].astype(BF16), ffn_w3[i].astype(BF16), ffn_w2[i].astype(BF16),
                      n_lat_tiles=n_lat_tiles)
            if last:
                out = _final_norm(xt, final_norm_g[None], n_tiles=n_lat_tiles)
        else:
            i = l // 2
            wr = jnp.concatenate([moe_router[i], jnp.zeros((d, LANES - n_experts), F32)], axis=1)
            x1, h2, route_e, route_g = _postmix(xt, mods, ya, gg, hs, yc, w_out[l].astype(BF16),
                                                norm2_g[l][None], wr,
                                                n_lat_tiles=n_lat_tiles, n_experts=n_experts)
            n_tiles = n_lat_tiles if last else n_lat_tiles + n_ctx_tiles
            blk_e, n_used, pos, n_pad = _route_tables(route_e, n_experts, n_tiles * TM)
            xs = _dispatch(pos, h2, n_pad, n_tiles=n_tiles)
            yp = _experts(blk_e, n_used, xs, moe_w1[i], moe_w3[i], moe_w2[i])
            out_or_xt = _combine(pos, x1, mods, route_g, yp, final_norm_g[None] if last else None,
                                 n_lat_tiles=n_lat_tiles, n_tiles=n_tiles)
            if last:
                out = out_or_xt
            else:
                xt = out_or_xt
    return out
```

```python
import functools

import jax
import jax.numpy as jnp
from jax import lax
from jax.experimental import pallas as pl
from jax.experimental.pallas import tpu as pltpu

F32 = jnp.float32
BF16 = jnp.bfloat16

EPS = 1e-6
GRID_W = 64
ROPE_BASE = 10000.0
LRU_C = 8.0
H_C = 8
TOP_K = 2
LANES = 128
TM = 256
T_SCAN = 256
KC = 1024
NH = 4
TMB = 1024
TF = 512
VMEM_LIMIT = 56 * 1024 * 1024


def _cparams(sem):
    return pltpu.CompilerParams(dimension_semantics=sem, vmem_limit_bytes=VMEM_LIMIT)


def _rms(x, g):
    return x * lax.rsqrt(jnp.mean(x * x, axis=-1, keepdims=True) + EPS) * g


def _mod_kernel(c_ref, w_ref, b_ref, o_ref):
    c = c_ref[...]
    a = (c * jax.nn.sigmoid(c)).astype(BF16)
    o_ref[0] = jnp.dot(a, w_ref[0], preferred_element_type=F32) + b_ref[0]


def _modulation(cond, w_mod, b_mod):
    depth, d, n = w_mod.shape
    r = cond.shape[0]
    tn = 1024
    return pl.pallas_call(
        _mod_kernel,
        grid=(depth, n // tn),
        in_specs=[pl.BlockSpec((r, d), lambda l, j: (0, 0)),
                  pl.BlockSpec((1, d, tn), lambda l, j: (l, 0, j)),
                  pl.BlockSpec((1, 1, tn), lambda l, j: (l, 0, j))],
        out_specs=pl.BlockSpec((1, r, tn), lambda l, j: (l, 0, j)),
        out_shape=jax.ShapeDtypeStruct((depth, r, n), F32),
        compiler_params=_cparams(("arbitrary", "arbitrary")),
        name="modulation",
    )(cond, w_mod, b_mod)


def _premix_kernel(x_ref, mod_ref, g_ref, win_ref, lng_ref, lnb_ref, ws_ref, bs_ref,
                   qn_ref, wuq_ref, wuqp_ref, kvn_ref, wuk_ref, e_ref, wuv_ref, vone_ref,
                   cq_ref, sq_ref, ck_ref, sk_ref,
                   ya_ref, gg_ref, xb_ref, q_ref, k_ref, v_ref, *, dims):
    d, d_a, d_b, q_lora, kv_lora, d_rope, h_a, chunk = dims
    x = x_ref[0]
    shift = mod_ref[0, :, 0:d]
    scale = mod_ref[0, :, d:2 * d]
    h = (_rms(x, g_ref[...]) * (1.0 + scale) + shift).astype(BF16)
    z = jnp.dot(h, win_ref[...], preferred_element_type=F32)

    o = 0
    u = jax.nn.gelu(z[:, o:o + d_a])
    v = jax.nn.gelu(z[:, o + d_a:o + 2 * d_a])
    mu = jnp.mean(v, axis=-1, keepdims=True)
    vc = v - mu
    var = jnp.mean(vc * vc, axis=-1, keepdims=True)
    vn = (vc * lax.rsqrt(var + EPS) * lng_ref[...] + lnb_ref[...]).astype(BF16)
    dh_a = d_a // h_a
    tm = x.shape[0]
    head_of_lane = lax.broadcasted_iota(jnp.int32, (chunk, d_a), 1) // dh_a
    for c in range(tm // chunk):
        vch = vn[c * chunk:(c + 1) * chunk]
        s = jnp.dot(ws_ref[0], vch, preferred_element_type=F32)
        for hd in range(1, h_a):
            s = jnp.where(head_of_lane == hd,
                          jnp.dot(ws_ref[hd], vch, preferred_element_type=F32), s)
        s = s + bs_ref[...]
        ya_ref[0, c * chunk:(c + 1) * chunk, :] = u[c * chunk:(c + 1) * chunk] * s

    o = 2 * d_a
    gg_ref[0] = jax.nn.gelu(z[:, o:o + d_b])
    xb_ref[0] = z[:, o + d_b:o + 2 * d_b]

    o = 2 * d_a + 2 * d_b
    cq = _rms(z[:, o:o + q_lora], qn_ref[...]).astype(BF16)
    qa = jnp.dot(cq, wuq_ref[...], preferred_element_type=F32)
    qb = jnp.dot(cq, wuqp_ref[...], preferred_element_type=F32)
    cos_q = jnp.concatenate([cq_ref[...]] * H_C, axis=1)
    sin_q = jnp.concatenate([sq_ref[...]] * H_C, axis=1)
    q_ref[0] = (qa * cos_q + qb * sin_q).astype(BF16)
    o += q_lora
    ckv = _rms(z[:, o:o + kv_lora], kvn_ref[...]).astype(BF16)
    o += kv_lora
    zr = z[:, o:o + d_rope]
    zrp = z[:, o + d_rope:o + 2 * d_rope]
    kr = (zr * ck_ref[...] + zrp * sk_ref[...]).astype(BF16)
    kn = jnp.dot(ckv, wuk_ref[...], preferred_element_type=F32)
    k_ref[0] = (kn + jnp.dot(kr, e_ref[...], preferred_element_type=F32)).astype(BF16)
    v_ref[0] = (jnp.dot(ckv, wuv_ref[...], preferred_element_type=F32) + vone_ref[...]).astype(BF16)


def _tile_mod_index(n_lat_tiles, n_batch):
    def index(b, i):
        return (jnp.where(i < n_lat_tiles, b, n_batch), 0, 0)
    return index


def _premix(xt, mods, g1, win, lng, lnb, ws, bs, qn, wuq, wuqp, kvn, wuk, emat, wuv, vone,
            cosq, sinq, cosk, sink, *, n_lat_tiles, dims):
    bsz, s, d = xt.shape
    d_a, d_b = dims[1], dims[2]
    hp = H_C * LANES
    const2 = lambda b, i: (0, 0)
    const3 = lambda b, i: (0, 0, 0)
    tile = lambda b, i: (b, i, 0)
    full = lambda a: pl.BlockSpec(a.shape, const2 if a.ndim == 2 else const3)
    return pl.pallas_call(
        functools.partial(_premix_kernel, dims=dims),
        grid=(bsz, s // TM),
        in_specs=[pl.BlockSpec((1, TM, d), tile),
                  pl.BlockSpec((1, 1, mods.shape[-1]), _tile_mod_index(n_lat_tiles, bsz)),
                  full(g1), full(win), full(lng), full(lnb), full(ws), full(bs),
                  full(qn), full(wuq), full(wuqp), full(kvn), full(wuk), full(emat), full(wuv), full(vone),
                  pl.BlockSpec((TM, LANES), lambda b, i: (i, 0)),
                  pl.BlockSpec((TM, LANES), lambda b, i: (i, 0)),
                  pl.BlockSpec((TM, cosk.shape[1]), lambda b, i: (i, 0)),
                  pl.BlockSpec((TM, sink.shape[1]), lambda b, i: (i, 0))],
        out_specs=[pl.BlockSpec((1, TM, d_a), tile), pl.BlockSpec((1, TM, d_b), tile),
                   pl.BlockSpec((1, TM, d_b), tile), pl.BlockSpec((1, TM, hp), tile),
                   pl.BlockSpec((1, TM, hp), tile), pl.BlockSpec((1, TM, hp), tile)],
        out_shape=[jax.ShapeDtypeStruct((bsz, s, d_a), F32),
                   jax.ShapeDtypeStruct((bsz, s, d_b), F32),
                   jax.ShapeDtypeStruct((bsz, s, d_b), F32),
                   jax.ShapeDtypeStruct((bsz, s, hp), BF16),
                   jax.ShapeDtypeStruct((bsz, s, hp), BF16),
                   jax.ShapeDtypeStruct((bsz, s, hp), BF16)],
        compiler_params=_cparams(("arbitrary", "arbitrary")),
        name="premix",
    )(xt, mods, g1, win, lng, lnb, ws, bs, qn, wuq, wuqp, kvn, wuk, emat, wuv, vone,
      cosq, sinq, cosk, sink)


def _scan_tile(a, b, carry, reverse):
    t = a.shape[0]
    rows = lax.broadcasted_iota(jnp.int32, a.shape, 0)
    s = 1
    while s < t:
        if reverse:
            a_sh = pltpu.roll(a, t - s, 0)
            b_sh = pltpu.roll(b, t - s, 0)
            ok = rows < t - s
        else:
            a_sh = pltpu.roll(a, s, 0)
            b_sh = pltpu.roll(b, s, 0)
            ok = rows >= s
        b = b + a * jnp.where(ok, b_sh, 0.0)
        a = a * jnp.where(ok, a_sh, 1.0)
        s *= 2
    h = b + a * carry
    return h, (h[0:1] if reverse else h[t - 1:t])


def _lru_kernel(xb_ref, cw_ref, cb_ref, wg_ref, bg_ref, lam_ref, out_ref, xc_ref, *, n_lat, n_ctx):
    t = T_SCAN
    n_tiles = n_lat + n_ctx
    s_total = n_tiles * t
    d_b = xb_ref.shape[-1]
    w = cw_ref[...]
    cb = cb_ref[...]

    def conv_body(j, _):
        t0 = pl.multiple_of(j * t, t)
        is_ctx = j >= n_lat
        seq_lo = jnp.where(is_ctx, n_lat * t, 0)
        seq_hi = jnp.where(is_ctx, s_total, n_lat * t)
        cur = xb_ref[0, pl.ds(t0, t), :]
        p0 = pl.multiple_of(jnp.maximum(t0 - 8, 0), 8)
        n0 = pl.multiple_of(jnp.minimum(t0 + t, s_total - 8), 8)
        prev = jnp.where(t0 > seq_lo, xb_ref[0, pl.ds(p0, 8), :], 0.0)
        nxt = jnp.where(t0 + t < seq_hi, xb_ref[0, pl.ds(n0, 8), :], 0.0)
        ext = jnp.concatenate([prev, cur, nxt], axis=0)
        n_ext = t + 16
        xm2 = pltpu.roll(ext, 2, 0)[8:8 + t]
        xm1 = pltpu.roll(ext, 1, 0)[8:8 + t]
        xp1 = pltpu.roll(ext, n_ext - 1, 0)[8:8 + t]
        xc_ref[pl.ds(t0, t), :] = (w[0:1] * xm2 + w[1:2] * xm1 + w[2:3] * cur + w[3:4] * xp1 + cb)
        return 0

    lax.fori_loop(0, n_tiles, conv_body, 0)

    lam = lam_ref[...]
    neg = -lam
    softplus = jnp.maximum(neg, 0.0) + jnp.log1p(jnp.exp(-jnp.abs(neg)))

    def direction(dr, reverse):
        sp = softplus[dr:dr + 1]
        wg = wg_ref[:, dr * 2 * d_b:(dr + 1) * 2 * d_b]
        bg = bg_ref[:, dr * 2 * d_b:(dr + 1) * 2 * d_b]

        def body(j, carry):
            if reverse:
                idx = jnp.where(j < n_ctx, n_tiles - 1 - j, n_lat - 1 - (j - n_ctx))
            else:
                idx = jnp.where(j < n_ctx, n_lat + j, j - n_ctx)
            t0 = pl.multiple_of(idx * t, t)
            xc = xc_ref[pl.ds(t0, t), :]
            g = jnp.dot(xc.astype(BF16), wg, preferred_element_type=F32) + bg
            r = jax.nn.sigmoid(g[:, 0:d_b])
            ig = jax.nn.sigmoid(g[:, d_b:2 * d_b])
            log_a = (-LRU_C * r) * sp
            a = jnp.exp(log_a)
            bv = jnp.sqrt(-jnp.tanh(log_a) * (a * a + 1.0)) * (ig * xc)
            h, carry = _scan_tile(a, bv, carry, reverse)
            if reverse:
                out_ref[0, pl.ds(t0, t), :] = out_ref[0, pl.ds(t0, t), :] + h
            else:
                out_ref[0, pl.ds(t0, t), :] = h
            return carry

        lax.fori_loop(0, n_tiles, body, jnp.zeros((1, d_b), F32))

    direction(0, False)
    direction(1, True)


def _lru(xb, cw, cb, wg, bg, lam, *, n_lat, n_ctx):
    bsz, s, d_b = xb.shape
    const2 = lambda b: (0, 0)
    full = lambda a: pl.BlockSpec(a.shape, const2)
    return pl.pallas_call(
        functools.partial(_lru_kernel, n_lat=n_lat, n_ctx=n_ctx),
        grid=(bsz,),
        in_specs=[pl.BlockSpec((1, s, d_b), lambda b: (b, 0, 0)),
                  full(cw), full(cb), full(wg), full(bg), full(lam)],
        out_specs=pl.BlockSpec((1, s, d_b), lambda b: (b, 0, 0)),
        out_shape=jax.ShapeDtypeStruct((bsz, s, d_b), F32),
        scratch_shapes=[pltpu.VMEM((s, d_b), F32)],
        compiler_params=_cparams(("arbitrary",)),
        name="rglru",
    )(xb, cw, cb, wg, bg, lam)


def _attn_kernel(q_ref, k_ref, v_ref, o_ref, *, n_lat_tiles, chunks_lat, chunks_ctx, d_v):
    i = pl.program_id(2)
    tq = q_ref.shape[1]

    def compute(chunks):
        units = [(hh, lo, n) for hh in range(NH) for lo, n in chunks]

        def scores(hh, lo, n):
            qh = q_ref[0, :, hh * LANES:(hh + 1) * LANES]
            kh = k_ref[0, lo:lo + n, hh * LANES:(hh + 1) * LANES]
            return lax.dot_general(qh, kh, (((1,), (1,)), ((), ())), preferred_element_type=F32)

        outs = []
        m = acc = None
        s_next = scores(*units[0])
        for ui, (hh, lo, n) in enumerate(units):
            s = s_next
            if ui + 1 < len(units):
                s_next = scores(*units[ui + 1])
            if lo == chunks[0][0]:
                m = acc = None
            mc = jnp.max(s, axis=-1, keepdims=True)
            m_new = mc if m is None else jnp.maximum(m, mc)
            p = jnp.exp2(s - m_new).astype(BF16)
            pv = jnp.dot(p, v_ref[0, lo:lo + n, hh * LANES:(hh + 1) * LANES],
                         preferred_element_type=F32)
            acc = pv if m is None else jnp.exp2(m - m_new) * acc + pv
            m = m_new
            if lo == chunks[-1][0]:
                outs.append(acc / acc[:, d_v:d_v + 1])
        lane = lax.broadcasted_iota(jnp.int32, (tq, LANES), 1)
        for pp in range(NH // 2):
            pair = jnp.where(lane < d_v, outs[2 * pp], pltpu.roll(outs[2 * pp + 1], d_v, 1))
            o_ref[0, :, pp * LANES:(pp + 1) * LANES] = pair.astype(o_ref.dtype)

    @pl.when(i < n_lat_tiles)
    def _():
        compute(chunks_lat + chunks_ctx)

    @pl.when(i >= n_lat_tiles)
    def _():
        compute(chunks_ctx)


def _key_chunks(lo, hi, size):
    return tuple((a, min(size, hi - a)) for a in range(lo, hi, size))


def _attention(q, k, v, *, l_lat, d_v):
    bsz, s, hp = q.shape
    d_c = H_C * d_v
    kern = functools.partial(_attn_kernel, n_lat_tiles=l_lat // TM,
                             chunks_lat=_key_chunks(0, l_lat, KC), chunks_ctx=_key_chunks(l_lat, s, KC),
                             d_v=d_v)
    return pl.pallas_call(
        kern,
        grid=(bsz, H_C // NH, s // TM),
        in_specs=[pl.BlockSpec((1, TM, NH * LANES), lambda b, h, i: (b, i, h)),
                  pl.BlockSpec((1, s, NH * LANES), lambda b, h, i: (b, 0, h)),
                  pl.BlockSpec((1, s, NH * LANES), lambda b, h, i: (b, 0, h))],
        out_specs=pl.BlockSpec((1, TM, NH * d_v), lambda b, h, i: (b, i, h)),
        out_shape=jax.ShapeDtypeStruct((bsz, s, d_c), BF16),
        compiler_params=_cparams(("arbitrary", "arbitrary", "arbitrary")),
        name="attention",
    )(q, k, v)


def _postmix_kernel(x_ref, mod_ref, ya_ref, gg_ref, hs_ref, yc_ref, wout_ref, g2_ref, *rest,
                    d, n_experts, route):
    if route:
        wr_ref, x1_ref, h2_ref, re_ref, rg_ref = rest
    else:
        x1_ref, h2_ref = rest
    gate1 = mod_ref[0, :, 2 * d:3 * d]
    shift2 = mod_ref[0, :, 3 * d:4 * d]
    scale2 = mod_ref[0, :, 4 * d:5 * d]
    y = jnp.concatenate([ya_ref[0].astype(BF16), (gg_ref[0] * hs_ref[0]).astype(BF16), yc_ref[0]],
                        axis=1)
    x1 = x_ref[0] + gate1 * jnp.dot(y, wout_ref[...], preferred_element_type=F32)
    x1_ref[0] = x1
    h2 = _rms(x1, g2_ref[...]) * (1.0 + scale2) + shift2
    h2_ref[0] = h2.astype(h2_ref.dtype)
    if route:
        logits = jnp.dot(h2, wr_ref[...], preferred_element_type=F32)
        lane = lax.broadcasted_iota(jnp.int32, logits.shape, 1)
        neg_inf = jnp.float32(-jnp.inf)
        lg = jnp.where(lane < n_experts, logits, neg_inf)
        m1 = jnp.max(lg, axis=-1, keepdims=True)
        i1 = jnp.min(jnp.where(lg == m1, lane, LANES), axis=-1, keepdims=True)
        lg2 = jnp.where(lane == i1, neg_inf, lg)
        m2 = jnp.max(lg2, axis=-1, keepdims=True)
        i2 = jnp.min(jnp.where(lg2 == m2, lane, LANES), axis=-1, keepdims=True)
        e = jnp.exp(m2 - m1)
        den = 1.0 + e
        re_ref[0] = jnp.where(lane == 0, i1, i2)
        rg_ref[0] = jnp.where(lane == 0, 1.0 / den, e / den)


def _postmix(xt, mods, ya, gg, hs, yc, wout, g2, wr, *, n_lat_tiles, n_experts):
    bsz, s, d = xt.shape
    route = wr is not None
    tile = lambda b, i: (b, i, 0)
    const2 = lambda b, i: (0, 0)
    full = lambda a: pl.BlockSpec(a.shape, const2)
    in_specs = [pl.BlockSpec((1, TM, d), tile),
                pl.BlockSpec((1, 1, mods.shape[-1]), _tile_mod_index(n_lat_tiles, bsz)),
                pl.BlockSpec((1, TM, ya.shape[-1]), tile), pl.BlockSpec((1, TM, gg.shape[-1]), tile),
                pl.BlockSpec((1, TM, hs.shape[-1]), tile), pl.BlockSpec((1, TM, yc.shape[-1]), tile),
                full(wout), full(g2)]
    args = [xt, mods, ya, gg, hs, yc, wout, g2]
    out_specs = [pl.BlockSpec((1, TM, d), tile), pl.BlockSpec((1, TM, d), tile)]
    out_shape = [jax.ShapeDtypeStruct((bsz, s, d), F32),
                 jax.ShapeDtypeStruct((bsz, s, d), F32 if route else BF16)]
    if route:
        in_specs.append(full(wr))
        args.append(wr)
        out_specs += [pl.BlockSpec((1, TM, LANES), tile), pl.BlockSpec((1, TM, LANES), tile)]
        out_shape += [jax.ShapeDtypeStruct((bsz, s, LANES), jnp.int32),
                      jax.ShapeDtypeStruct((bsz, s, LANES), F32)]
    return pl.pallas_call(
        functools.partial(_postmix_kernel, d=d, n_experts=n_experts, route=route),
        grid=(bsz, s // TM),
        in_specs=in_specs, out_specs=out_specs, out_shape=out_shape,
        compiler_params=_cparams(("arbitrary", "arbitrary")),
        name="postmix_route" if route else "postmix",
    )(*args)


def _ffn_kernel(x1_ref, h2_ref, mod_ref, w1_ref, w3_ref, w2_ref, o_ref, *, d, fc):
    h = h2_ref[0]
    d_ff = w1_ref.shape[1]
    n_chunks = d_ff // fc

    def up(c):
        return (jnp.dot(h, w1_ref[:, c * fc:(c + 1) * fc], preferred_element_type=F32),
                jnp.dot(h, w3_ref[:, c * fc:(c + 1) * fc], preferred_element_type=F32))

    acc = jnp.zeros((h.shape[0], d), F32)
    a, b = up(0)
    for c in range(n_chunks):
        ac, bc = a, b
        if c + 1 < n_chunks:
            a, b = up(c + 1)
        act = (ac * jax.nn.sigmoid(ac) * bc).astype(BF16)
        acc = acc + jnp.dot(act, w2_ref[c * fc:(c + 1) * fc, :], preferred_element_type=F32)
    gate2 = mod_ref[0, :, 5 * d:6 * d]
    o_ref[0] = x1_ref[0] + gate2 * acc


def _ffn(x1, h2, mods, w1, w3, w2, *, n_lat_tiles):
    bsz, s, d = x1.shape
    tile = lambda b, i: (b, i, 0)
    const2 = lambda b, i: (0, 0)
    full = lambda a: pl.BlockSpec(a.shape, const2)
    return pl.pallas_call(
        functools.partial(_ffn_kernel, d=d, fc=256),
        grid=(bsz, s // TM),
        in_specs=[pl.BlockSpec((1, TM, d), tile), pl.BlockSpec((1, TM, d), tile),
                  pl.BlockSpec((1, 1, mods.shape[-1]), _tile_mod_index(n_lat_tiles, bsz)),
                  full(w1), full(w3), full(w2)],
        out_specs=pl.BlockSpec((1, TM, d), tile),
        out_shape=jax.ShapeDtypeStruct((bsz, s, d), F32),
        compiler_params=_cparams(("arbitrary", "arbitrary")),
        name="ffn",
    )(x1, h2, mods, w1, w3, w2)


DMA_UNROLL = 8


def _row_copy(src_ref, dst_ref, sem, src_row, dst_row):
    return pltpu.make_async_copy(src_ref.at[pl.ds(src_row, 1)], dst_ref.at[pl.ds(dst_row, 1)], sem)


def _dispatch_kernel(pos_ref, h_ref, xs_in_hbm, xs_hbm, sem):
    del xs_in_hbm
    n = h_ref.shape[1]

    def start(j, _):
        for kk in range(TOP_K):
            _row_copy(h_ref.at[0], xs_hbm, sem, j, pos_ref[0, 0, TOP_K * j + kk]).start()
        return 0

    def wait(j, _):
        for kk in range(TOP_K):
            _row_copy(h_ref.at[0], xs_hbm, sem, j, 0).wait()
        return 0

    lax.fori_loop(0, n, start, 0, unroll=DMA_UNROLL)
    lax.fori_loop(0, n, wait, 0, unroll=DMA_UNROLL)


def _dispatch(pos, h2, n_pad, *, n_tiles):
    bsz, s, d = h2.shape
    xs0 = jnp.zeros((n_pad, d), h2.dtype)
    return pl.pallas_call(
        _dispatch_kernel,
        grid=(bsz, n_tiles),
        in_specs=[pl.BlockSpec((1, 1, TOP_K * TM), lambda b, i: (b * n_tiles + i, 0, 0),
                               memory_space=pltpu.SMEM),
                  pl.BlockSpec((1, TM, d), lambda b, i: (b, i, 0)),
                  pl.BlockSpec(memory_space=pl.ANY)],
        out_specs=pl.BlockSpec(memory_space=pl.ANY),
        out_shape=jax.ShapeDtypeStruct((n_pad, d), h2.dtype),
        scratch_shapes=[pltpu.SemaphoreType.DMA(())],
        input_output_aliases={2: 0},
        compiler_params=_cparams(("arbitrary", "arbitrary")),
        name="moe_dispatch",
    )(pos, h2, xs0)


def _experts_kernel(be_ref, nu_ref, xs_ref, w1_ref, w3_ref, w2_ref, o_ref, xb_ref, acc_ref):
    r = pl.program_id(0)
    f = pl.program_id(1)
    nf = pl.num_programs(1)

    @pl.when(r < nu_ref[0])
    def _():
        @pl.when(f == 0)
        def _():
            xb_ref[...] = xs_ref[...].astype(BF16)
            acc_ref[...] = jnp.zeros_like(acc_ref)

        xb = xb_ref[...]
        a = jnp.dot(xb, w1_ref[0].astype(BF16), preferred_element_type=F32)
        b = jnp.dot(xb, w3_ref[0].astype(BF16), preferred_element_type=F32)
        act = (a * jax.nn.sigmoid(a) * b).astype(BF16)
        acc_ref[...] += jnp.dot(act, w2_ref[0].astype(BF16), preferred_element_type=F32)

        @pl.when(f == nf - 1)
        def _():
            o_ref[...] = acc_ref[...]

    @pl.when(jnp.logical_and(r >= nu_ref[0], f == nf - 1))
    def _():
        o_ref[...] = jnp.zeros_like(o_ref)


def _experts(blk_e, n_used, xs, w1, w3, w2):
    n_pad, d = xs.shape
    n_blk = n_pad // TMB
    d_ff = w1.shape[-1]
    nf = d_ff // TF

    def w_col(r, f, be, nu):
        live = r < nu[0]
        return (be[r], 0, jnp.where(live, f, nf - 1))

    def w_row(r, f, be, nu):
        live = r < nu[0]
        return (be[r], jnp.where(live, f, nf - 1), 0)

    grid_spec = pltpu.PrefetchScalarGridSpec(
        num_scalar_prefetch=2,
        grid=(n_blk, nf),
        in_specs=[pl.BlockSpec((TMB, d), lambda r, f, be, nu: (r, 0)),
                  pl.BlockSpec((1, d, TF), w_col),
                  pl.BlockSpec((1, d, TF), w_col),
                  pl.BlockSpec((1, TF, d), w_row)],
        out_specs=pl.BlockSpec((TMB, d), lambda r, f, be, nu: (r, 0)),
        scratch_shapes=[pltpu.VMEM((TMB, d), BF16), pltpu.VMEM((TMB, d), F32)])
    return pl.pallas_call(
        _experts_kernel,
        grid_spec=grid_spec,
        out_shape=jax.ShapeDtypeStruct((n_pad, d), F32),
        compiler_params=_cparams(("arbitrary", "arbitrary")),
        name="moe_experts",
    )(blk_e, n_used, xs, w1, w3, w2)


def _combine_kernel(pos_ref, x1_ref, mod_ref, rg_ref, yp_hbm, *rest, d, final):
    if final:
        gf_ref, o_ref, buf_ref, sem = rest
    else:
        o_ref, buf_ref, sem = rest
    n = x1_ref.shape[1]

    def start(j, _):
        for kk in range(TOP_K):
            _row_copy(yp_hbm, buf_ref.at[kk], sem, pos_ref[0, 0, TOP_K * j + kk], j).start()
        return 0

    def wait(j, _):
        for kk in range(TOP_K):
            _row_copy(yp_hbm, buf_ref.at[kk], sem, 0, j).wait()
        return 0

    lax.fori_loop(0, n, start, 0, unroll=DMA_UNROLL)
    lax.fori_loop(0, n, wait, 0, unroll=DMA_UNROLL)
    gate2 = mod_ref[0, :, 5 * d:6 * d]
    rg = rg_ref[0]
    y = rg[:, 0:1] * buf_ref[0]
    for kk in range(1, TOP_K):
        y = y + rg[:, kk:kk + 1] * buf_ref[kk]
    x2 = x1_ref[0] + gate2 * y
    if final:
        x2 = _rms(x2, gf_ref[...])
    o_ref[0] = x2


def _combine(pos, x1, mods, route_g, yp, gf, *, n_lat_tiles, n_tiles):
    bsz, s, d = x1.shape
    final = gf is not None
    tile = lambda b, i: (b, i, 0)
    in_specs = [pl.BlockSpec((1, 1, TOP_K * TM), lambda b, i: (b * n_tiles + i, 0, 0),
                             memory_space=pltpu.SMEM),
                pl.BlockSpec((1, TM, d), tile),
                pl.BlockSpec((1, 1, mods.shape[-1]), _tile_mod_index(n_lat_tiles, bsz)),
                pl.BlockSpec((1, TM, LANES), tile),
                pl.BlockSpec(memory_space=pl.ANY)]
    args = [pos, x1, mods, route_g, yp]
    if final:
        in_specs.append(pl.BlockSpec(gf.shape, lambda b, i: (0, 0)))
        args.append(gf)
    return pl.pallas_call(
        functools.partial(_combine_kernel, d=d, final=final),
        grid=(bsz, n_tiles),
        in_specs=in_specs,
        out_specs=pl.BlockSpec((1, TM, d), tile),
        out_shape=jax.ShapeDtypeStruct((bsz, n_tiles * TM, d), F32),
        scratch_shapes=[pltpu.VMEM((TOP_K, TM, d), F32), pltpu.SemaphoreType.DMA(())],
        compiler_params=_cparams(("arbitrary", "arbitrary")),
        name="moe_combine_final" if final else "moe_combine",
    )(*args)


def _route_tables(route_e, n_experts, s_eff):
    bsz = route_e.shape[0]
    s = s_eff
    n_assign = bsz * s * TOP_K
    flat_e = route_e[:, :s, :TOP_K].reshape(n_assign)
    onehot = (flat_e[:, None] == jnp.arange(n_experts, dtype=jnp.int32)[None, :]).astype(jnp.int32)
    csum = jnp.cumsum(onehot, axis=0)
    rank = jnp.sum(csum * onehot, axis=1) - 1
    counts = csum[-1]
    padded = (counts + TMB - 1) // TMB * TMB
    pad_ends = jnp.cumsum(padded)
    pad_starts = pad_ends - padded
    dest = pad_starts[flat_e] + rank
    n_blk = (n_assign + TMB - 1) // TMB + n_experts
    n_pad = n_blk * TMB
    blk_e = jnp.minimum(
        jnp.searchsorted(pad_ends, jnp.arange(n_blk, dtype=jnp.int32) * TMB, side='right'),
        n_experts - 1).astype(jnp.int32)
    n_used = (pad_ends[-1] // TMB).astype(jnp.int32).reshape(1)
    pos = dest.astype(jnp.int32).reshape(bsz * s // TM, 1, TOP_K * TM)
    return blk_e, n_used, pos, n_pad


def _final_kernel(x_ref, g_ref, o_ref):
    o_ref[0] = _rms(x_ref[0], g_ref[...])


def _final_norm(xt, g, *, n_tiles):
    bsz, s, d = xt.shape
    tile = lambda b, i: (b, i, 0)
    return pl.pallas_call(
        _final_kernel,
        grid=(bsz, n_tiles),
        in_specs=[pl.BlockSpec((1, TM, d), tile), pl.BlockSpec(g.shape, lambda b, i: (0, 0))],
        out_specs=pl.BlockSpec((1, TM, d), tile),
        out_shape=jax.ShapeDtypeStruct((bsz, n_tiles * TM, d), F32),
        compiler_params=_cparams(("arbitrary", "arbitrary")),
        name="final_norm",
    )(xt, g)


def _rope_partner(r):
    return jnp.concatenate([-r[..., 8:16], r[..., 0:8], -r[..., 24:32], r[..., 16:24]], axis=-1)


def _rope_tables(l_lat, s_total, d_rope, d_nope):
    t = jnp.arange(l_lat, dtype=jnp.int32)
    row = (t // GRID_W).astype(F32)
    col = (t % GRID_W).astype(F32)
    half = d_rope // 2
    inv_freq = ROPE_BASE ** (-jnp.arange(0, half, 2, dtype=F32) / half)
    ang_r = row[:, None] * inv_freq
    ang_c = col[:, None] * inv_freq
    cos = jnp.concatenate([jnp.cos(ang_r), jnp.cos(ang_r), jnp.cos(ang_c), jnp.cos(ang_c)], axis=1)
    sin = jnp.concatenate([jnp.sin(ang_r), jnp.sin(ang_r), jnp.sin(ang_c), jnp.sin(ang_c)], axis=1)
    n_ctx = s_total - l_lat
    cosk = jnp.concatenate([cos, jnp.ones((n_ctx, d_rope), F32)], axis=0)
    sink = jnp.concatenate([sin, jnp.zeros((n_ctx, d_rope), F32)], axis=0)
    pad = LANES - d_nope - d_rope
    cosq = jnp.concatenate([jnp.ones((s_total, d_nope), F32), cosk, jnp.zeros((s_total, pad), F32)], axis=1)
    sinq = jnp.concatenate([jnp.zeros((s_total, d_nope), F32), sink, jnp.zeros((s_total, pad), F32)], axis=1)
    return cosq, sinq, cosk, sink


def _block_diag(w):
    h, a, b = w.shape
    eye = jnp.eye(h, dtype=w.dtype)
    return (eye[:, None, :, None] * w[:, :, None, :]).reshape(h * a, h * b)


def kernel(x, c, ctx, c_ctx, w_mod, b_mod, norm1_g, norm2_g, w_in, w_out, sgu_ln_g, sgu_ln_b, sgu_w, sgu_b, conv_w, conv_b, lru_w_a, lru_b_a, lru_w_x, lru_b_x, lru_lam, mla_q_norm, mla_w_uq, mla_kv_norm, mla_w_uk, mla_w_uv, ffn_w1, ffn_w3, ffn_w2, moe_router, moe_w1, moe_w3, moe_w2, final_norm_g):
    bsz, l_lat, d = x.shape
    l_ctx = ctx.shape[1]
    s_total = l_lat + l_ctx
    depth = w_mod.shape[0]
    d_a = sgu_ln_g.shape[-1]
    h_a, chunk = sgu_w.shape[1], sgu_w.shape[2]
    d_b = conv_w.shape[-1]
    q_lora = mla_q_norm.shape[-1]
    kv_lora = mla_kv_norm.shape[-1]
    d_c = mla_w_uv.shape[-1]
    d_v = d_c // H_C
    d_nope = mla_w_uk.shape[-1] // H_C
    d_qk = mla_w_uq.shape[-1] // H_C
    d_rope = d_qk - d_nope
    n_experts = moe_router.shape[-1]
    assert l_lat % TM == 0 and l_ctx % TM == 0 and TM % chunk == 0 and TM == T_SCAN
    assert d_qk <= LANES and 2 * d_v == LANES and H_C % 2 == 0 and d_rope == 32
    n_lat_tiles = l_lat // TM
    n_ctx_tiles = l_ctx // TM
    dims = (d, d_a, d_b, q_lora, kv_lora, d_rope, h_a, chunk)

    xt = jnp.concatenate([x, ctx], axis=1)

    n_rows = (bsz + 1 + 7) // 8 * 8
    cond = jnp.zeros((n_rows, d), F32).at[:bsz].set(c).at[bsz].set(c_ctx)
    mods_all = _modulation(cond, w_mod.astype(BF16), b_mod[:, None, :])
    mods_all = mods_all[:, :bsz + 1, None, :]

    cosq, sinq, cosk, sink = _rope_tables(l_lat, s_total, d_rope, d_nope)
    q_scale = float(d_qk) ** -0.5 * 1.4426950408889634
    cosq, sinq = cosq * q_scale, sinq * q_scale
    head_pad = LANES - d_qk
    vone = jnp.tile((jnp.arange(LANES) == d_v).astype(F32)[None], (1, H_C))
    e_head = jnp.concatenate([jnp.zeros((d_rope, d_nope), F32), jnp.eye(d_rope, dtype=F32),
                              jnp.zeros((d_rope, head_pad), F32)], axis=1)
    emat = jnp.tile(e_head, (1, H_C)).astype(BF16)

    out = None
    for l in range(depth):
        last = l == depth - 1
        mods = mods_all[l]
        o_m = 2 * d_a + 2 * d_b
        o_r = o_m + q_lora + kv_lora
        w_rope = w_in[l][:, o_r:o_r + d_rope]
        n_in = (o_r + 2 * d_rope + LANES - 1) // LANES * LANES
        win = jnp.concatenate([w_in[l][:, :o_r + d_rope], _rope_partner(w_rope),
                               jnp.zeros((d, n_in - o_r - 2 * d_rope), F32)], axis=1).astype(BF16)
        wq = mla_w_uq[l].reshape(q_lora, H_C, d_qk)
        zq = jnp.zeros((q_lora, H_C, head_pad), F32)
        wuq = jnp.concatenate([wq, zq], axis=-1).reshape(q_lora, H_C * LANES).astype(BF16)
        wuqp = jnp.concatenate([jnp.zeros((q_lora, H_C, d_nope), F32), _rope_partner(wq[..., d_nope:]), zq],
                               axis=-1).reshape(q_lora, H_C * LANES).astype(BF16)
        wk = mla_w_uk[l].reshape(kv_lora, H_C, d_nope)
        wuk = jnp.concatenate([wk, jnp.zeros((kv_lora, H_C, LANES - d_nope), F32)],
                              axis=-1).reshape(kv_lora, H_C * LANES).astype(BF16)
        wv = mla_w_uv[l].reshape(kv_lora, H_C, d_v)
        wuv = jnp.concatenate([wv, jnp.zeros((kv_lora, H_C, LANES - d_v), F32)],
                              axis=-1).reshape(kv_lora, H_C * LANES).astype(BF16)
        bs_full = jnp.repeat(sgu_b[l].T, d_a // h_a, axis=1)

        ya, gg, xb, q, k, v = _premix(
            xt, mods, norm1_g[l][None], win, sgu_ln_g[l][None], sgu_ln_b[l][None],
            sgu_w[l].astype(BF16), bs_full, mla_q_norm[l][None], wuq, wuqp, mla_kv_norm[l][None],
            wuk, emat, wuv, vone, cosq, sinq, cosk, sink,
            n_lat_tiles=n_lat_tiles, dims=dims)

        wg = jnp.concatenate([_block_diag(lru_w_a[l, 0]), _block_diag(lru_w_x[l, 0]),
                              _block_diag(lru_w_a[l, 1]), _block_diag(lru_w_x[l, 1])], axis=1).astype(BF16)
        bg = jnp.concatenate([lru_b_a[l, 0], lru_b_x[l, 0], lru_b_a[l, 1], lru_b_x[l, 1]])[None]
        hs = _lru(xb, conv_w[l], conv_b[l][None], wg, bg, lru_lam[l], n_lat=n_lat_tiles, n_ctx=n_ctx_tiles)

        yc = _attention(q, k, v, l_lat=l_lat, d_v=d_v)

        if l % 2 == 0:
            i = l // 2
            x1, h2 = _postmix(xt, mods, ya, gg, hs, yc, w_out[l].astype(BF16), norm2_g[l][None], None,
                              n_lat_tiles=n_lat_tiles, n_experts=n_experts)
            xt = _ffn(x1, h2, mods, ffn_w1[i].astype(BF16), ffn_w3[i].astype(BF16), ffn_w2[i].astype(BF16),
                      n_lat_tiles=n_lat_tiles)
            if last:
                out = _final_norm(xt, final_norm_g[None], n_tiles=n_lat_tiles)
        else:
            i = l // 2
            wr = jnp.concatenate([moe_router[i], jnp.zeros((d, LANES - n_experts), F32)], axis=1)
            x1, h2, route_e, route_g = _postmix(xt, mods, ya, gg, hs, yc, w_out[l].astype(BF16),
                                                norm2_g[l][None], wr,
                                                n_lat_tiles=n_lat_tiles, n_experts=n_experts)
            n_tiles = n_lat_tiles if last else n_lat_tiles + n_ctx_tiles
            blk_e, n_used, pos, n_pad = _route_tables(route_e, n_experts, n_tiles * TM)
            xs = _dispatch(pos, h2, n_pad, n_tiles=n_tiles)
            yp = _experts(blk_e, n_used, xs, moe_w1[i], moe_w3[i], moe_w2[i])
            out_or_xt = _combine(pos, x1, mods, route_g, yp, final_norm_g[None] if last else None,
                                 n_lat_tiles=n_lat_tiles, n_tiles=n_tiles)
            if last:
                out = out_or_xt
            else:
                xt = out_or_xt
    return out
```

```python
import functools

import jax
import jax.numpy as jnp
from jax import lax
from jax.experimental import pallas as pl
from jax.experimental.pallas import tpu as pltpu

F32 = jnp.float32
BF16 = jnp.bfloat16

EPS = 1e-6
GRID_W = 64
ROPE_BASE = 10000.0
LRU_C = 8.0
H_C = 8
TOP_K = 2
LANES = 128
TM = 256
T_SCAN = 256
KC = 1024
NH = 4
TMB = 1024
TF = 512
VMEM_LIMIT = 56 * 1024 * 1024


def _cparams(sem):
    return pltpu.CompilerParams(dimension_semantics=sem, vmem_limit_bytes=VMEM_LIMIT)


def _rms(x, g):
    return x * lax.rsqrt(jnp.mean(x * x, axis=-1, keepdims=True) + EPS) * g


def _mod_kernel(c_ref, w_ref, b_ref, o_ref):
    c = c_ref[...]
    a = (c * jax.nn.sigmoid(c)).astype(BF16)
    o_ref[0] = jnp.dot(a, w_ref[0], preferred_element_type=F32) + b_ref[0]


def _modulation(cond, w_mod, b_mod):
    depth, d, n = w_mod.shape
    r = cond.shape[0]
    tn = 1024
    return pl.pallas_call(
        _mod_kernel,
        grid=(depth, n // tn),
        in_specs=[pl.BlockSpec((r, d), lambda l, j: (0, 0)),
                  pl.BlockSpec((1, d, tn), lambda l, j: (l, 0, j)),
                  pl.BlockSpec((1, 1, tn), lambda l, j: (l, 0, j))],
        out_specs=pl.BlockSpec((1, r, tn), lambda l, j: (l, 0, j)),
        out_shape=jax.ShapeDtypeStruct((depth, r, n), F32),
        compiler_params=_cparams(("arbitrary", "arbitrary")),
        name="modulation",
    )(cond, w_mod, b_mod)


def _premix_kernel(x_ref, mod_ref, g_ref, win_ref, lng_ref, lnb_ref, ws_ref, bs_ref,
                   qn_ref, wuq_ref, wuqp_ref, kvn_ref, wuk_ref, e_ref, wuv_ref, vone_ref,
                   cq_ref, sq_ref, ck_ref, sk_ref,
                   ya_ref, gg_ref, xb_ref, q_ref, k_ref, v_ref, *, dims):
    d, d_a, d_b, q_lora, kv_lora, d_rope, h_a, chunk = dims
    x = x_ref[0]
    shift = mod_ref[0, :, 0:d]
    scale = mod_ref[0, :, d:2 * d]
    h = (_rms(x, g_ref[...]) * (1.0 + scale) + shift).astype(BF16)
    z = jnp.dot(h, win_ref[...], preferred_element_type=F32)

    o = 0
    u = jax.nn.gelu(z[:, o:o + d_a])
    v = jax.nn.gelu(z[:, o + d_a:o + 2 * d_a])
    mu = jnp.mean(v, axis=-1, keepdims=True)
    vc = v - mu
    var = jnp.mean(vc * vc, axis=-1, keepdims=True)
    vn = (vc * lax.rsqrt(var + EPS) * lng_ref[...] + lnb_ref[...]).astype(BF16)
    dh_a = d_a // h_a
    tm = x.shape[0]
    head_of_lane = lax.broadcasted_iota(jnp.int32, (chunk, d_a), 1) // dh_a
    for c in range(tm // chunk):
        vch = vn[c * chunk:(c + 1) * chunk]
        s = jnp.dot(ws_ref[0], vch, preferred_element_type=F32)
        for hd in range(1, h_a):
            s = jnp.where(head_of_lane == hd,
                          jnp.dot(ws_ref[hd], vch, preferred_element_type=F32), s)
        s = s + bs_ref[...]
        ya_ref[0, c * chunk:(c + 1) * chunk, :] = u[c * chunk:(c + 1) * chunk] * s

    o = 2 * d_a
    gg_ref[0] = jax.nn.gelu(z[:, o:o + d_b])
    xb_ref[0] = z[:, o + d_b:o + 2 * d_b]

    o = 2 * d_a + 2 * d_b
    cq = _rms(z[:, o:o + q_lora], qn_ref[...]).astype(BF16)
    qa = jnp.dot(cq, wuq_ref[...], preferred_element_type=F32)
    qb = jnp.dot(cq, wuqp_ref[...], preferred_element_type=F32)
    cos_q = jnp.concatenate([cq_ref[...]] * H_C, axis=1)
    sin_q = jnp.concatenate([sq_ref[...]] * H_C, axis=1)
    q_ref[0] = (qa * cos_q + qb * sin_q).astype(BF16)
    o += q_lora
    ckv = _rms(z[:, o:o + kv_lora], kvn_ref[...]).astype(BF16)
    o += kv_lora
    zr = z[:, o:o + d_rope]
    zrp = z[:, o + d_rope:o + 2 * d_rope]
    kr = (zr * ck_ref[...] + zrp * sk_ref[...]).astype(BF16)
    kn = jnp.dot(ckv, wuk_ref[...], preferred_element_type=F32)
    k_ref[0] = (kn + jnp.dot(kr, e_ref[...], preferred_element_type=F32)).astype(BF16)
    v_ref[0] = (jnp.dot(ckv, wuv_ref[...], preferred_element_type=F32) + vone_ref[...]).astype(BF16)


def _tile_mod_index(n_lat_tiles, n_batch):
    def index(b, i):
        return (jnp.where(i < n_lat_tiles, b, n_batch), 0, 0)
    return index


def _premix(xt, mods, g1, win, lng, lnb, ws, bs, qn, wuq, wuqp, kvn, wuk, emat, wuv, vone,
            cosq, sinq, cosk, sink, *, n_lat_tiles, dims):
    bsz, s, d = xt.shape
    d_a, d_b = dims[1], dims[2]
    hp = H_C * LANES
    const2 = lambda b, i: (0, 0)
    const3 = lambda b, i: (0, 0, 0)
    tile = lambda b, i: (b, i, 0)
    full = lambda a: pl.BlockSpec(a.shape, const2 if a.ndim == 2 else const3)
    return pl.pallas_call(
        functools.partial(_premix_kernel, dims=dims),
        grid=(bsz, s // TM),
        in_specs=[pl.BlockSpec((1, TM, d), tile),
                  pl.BlockSpec((1, 1, mods.shape[-1]), _tile_mod_index(n_lat_tiles, bsz)),
                  full(g1), full(win), full(lng), full(lnb), full(ws), full(bs),
                  full(qn), full(wuq), full(wuqp), full(kvn), full(wuk), full(emat), full(wuv), full(vone),
                  pl.BlockSpec((TM, LANES), lambda b, i: (i, 0)),
                  pl.BlockSpec((TM, LANES), lambda b, i: (i, 0)),
                  pl.BlockSpec((TM, cosk.shape[1]), lambda b, i: (i, 0)),
                  pl.BlockSpec((TM, sink.shape[1]), lambda b, i: (i, 0))],
        out_specs=[pl.BlockSpec((1, TM, d_a), tile), pl.BlockSpec((1, TM, d_b), tile),
                   pl.BlockSpec((1, TM, d_b), tile), pl.BlockSpec((1, TM, hp), tile),
                   pl.BlockSpec((1, TM, hp), tile), pl.BlockSpec((1, TM, hp), tile)],
        out_shape=[jax.ShapeDtypeStruct((bsz, s, d_a), F32),
                   jax.ShapeDtypeStruct((bsz, s, d_b), F32),
                   jax.ShapeDtypeStruct((bsz, s, d_b), F32),
                   jax.ShapeDtypeStruct((bsz, s, hp), BF16),
                   jax.ShapeDtypeStruct((bsz, s, hp), BF16),
                   jax.ShapeDtypeStruct((bsz, s, hp), BF16)],
        compiler_params=_cparams(("arbitrary", "arbitrary")),
        name="premix",
    )(xt, mods, g1, win, lng, lnb, ws, bs, qn, wuq, wuqp, kvn, wuk, emat, wuv, vone,
      cosq, sinq, cosk, sink)


SUBLANES = 8


def _scan_tile(a, b, carry, reverse):
    t, c = a.shape
    g = SUBLANES
    n_groups = t // g
    a = a.reshape(n_groups, g, c)
    b = b.reshape(n_groups, g, c)
    rows = lax.broadcasted_iota(jnp.int32, a.shape, 1)
    s = 1
    while s < g:
        shift = g - s if reverse else s
        ok = rows < g - s if reverse else rows >= s
        a_sh = pltpu.roll(a, shift, 1)
        b_sh = pltpu.roll(b, shift, 1)
        b = b + a * jnp.where(ok, b_sh, 0.0)
        a = a * jnp.where(ok, a_sh, 1.0)
        s *= 2
    hs = [None] * n_groups
    for j in (range(n_groups - 1, -1, -1) if reverse else range(n_groups)):
        hj = b[j] + a[j] * carry
        hs[j] = hj
        carry = hj[0:1] if reverse else hj[g - 1:g]
    return jnp.concatenate(hs, axis=0), carry


def _lru_kernel(xb_ref, cw_ref, cb_ref, wg_ref, bg_ref, lam_ref, out_ref, xc_ref, *, n_lat, n_ctx):
    t = T_SCAN
    n_tiles = n_lat + n_ctx
    s_total = n_tiles * t
    d_b = xb_ref.shape[-1]
    w = cw_ref[...]
    cb = cb_ref[...]

    def conv_body(j, _):
        t0 = pl.multiple_of(j * t, t)
        is_ctx = j >= n_lat
        seq_lo = jnp.where(is_ctx, n_lat * t, 0)
        seq_hi = jnp.where(is_ctx, s_total, n_lat * t)
        cur = xb_ref[0, pl.ds(t0, t), :]
        p0 = pl.multiple_of(jnp.maximum(t0 - 8, 0), 8)
        n0 = pl.multiple_of(jnp.minimum(t0 + t, s_total - 8), 8)
        prev = jnp.where(t0 > seq_lo, xb_ref[0, pl.ds(p0, 8), :], 0.0)
        nxt = jnp.where(t0 + t < seq_hi, xb_ref[0, pl.ds(n0, 8), :], 0.0)
        ext = jnp.concatenate([prev, cur, nxt], axis=0)
        n_ext = t + 16
        xm2 = pltpu.roll(ext, 2, 0)[8:8 + t]
        xm1 = pltpu.roll(ext, 1, 0)[8:8 + t]
        xp1 = pltpu.roll(ext, n_ext - 1, 0)[8:8 + t]
        xc_ref[pl.ds(t0, t), :] = (w[0:1] * xm2 + w[1:2] * xm1 + w[2:3] * cur + w[3:4] * xp1 + cb)
        return 0

    lax.fori_loop(0, n_tiles, conv_body, 0)

    lam = lam_ref[...]
    neg = -lam
    softplus = jnp.maximum(neg, 0.0) + jnp.log1p(jnp.exp(-jnp.abs(neg)))

    def direction(dr, reverse):
        sp = softplus[dr:dr + 1]
        wg = wg_ref[:, dr * 2 * d_b:(dr + 1) * 2 * d_b]
        bg = bg_ref[:, dr * 2 * d_b:(dr + 1) * 2 * d_b]

        def body(j, carry):
            if reverse:
                idx = jnp.where(j < n_ctx, n_tiles - 1 - j, n_lat - 1 - (j - n_ctx))
            else:
                idx = jnp.where(j < n_ctx, n_lat + j, j - n_ctx)
            t0 = pl.multiple_of(idx * t, t)
            xc = xc_ref[pl.ds(t0, t), :]
            g = jnp.dot(xc.astype(BF16), wg, preferred_element_type=F32) + bg
            r = jax.nn.sigmoid(g[:, 0:d_b])
            ig = jax.nn.sigmoid(g[:, d_b:2 * d_b])
            log_a = (-LRU_C * r) * sp
            a = jnp.exp(log_a)
            bv = jnp.sqrt(-jnp.tanh(log_a) * (a * a + 1.0)) * (ig * xc)
            h, carry = _scan_tile(a, bv, carry, reverse)
            if reverse:
                out_ref[0, pl.ds(t0, t), :] = out_ref[0, pl.ds(t0, t), :] + h
            else:
                out_ref[0, pl.ds(t0, t), :] = h
            return carry

        lax.fori_loop(0, n_tiles, body, jnp.zeros((1, d_b), F32))

    direction(0, False)
    direction(1, True)


def _lru(xb, cw, cb, wg, bg, lam, *, n_lat, n_ctx):
    bsz, s, d_b = xb.shape
    const2 = lambda b: (0, 0)
    full = lambda a: pl.BlockSpec(a.shape, const2)
    return pl.pallas_call(
        functools.partial(_lru_kernel, n_lat=n_lat, n_ctx=n_ctx),
        grid=(bsz,),
        in_specs=[pl.BlockSpec((1, s, d_b), lambda b: (b, 0, 0)),
                  full(cw), full(cb), full(wg), full(bg), full(lam)],
        out_specs=pl.BlockSpec((1, s, d_b), lambda b: (b, 0, 0)),
        out_shape=jax.ShapeDtypeStruct((bsz, s, d_b), F32),
        scratch_shapes=[pltpu.VMEM((s, d_b), F32)],
        compiler_params=_cparams(("arbitrary",)),
        name="rglru",
    )(xb, cw, cb, wg, bg, lam)


def _attn_kernel(q_ref, k_ref, v_ref, o_ref, *, n_lat_tiles, chunks_lat, chunks_ctx, d_v):
    i = pl.program_id(2)
    tq = q_ref.shape[1]

    def compute(chunks):
        units = [(hh, lo, n) for hh in range(NH) for lo, n in chunks]

        def scores(hh, lo, n):
            qh = q_ref[0, :, hh * LANES:(hh + 1) * LANES]
            kh = k_ref[0, lo:lo + n, hh * LANES:(hh + 1) * LANES]
            return lax.dot_general(qh, kh, (((1,), (1,)), ((), ())), preferred_element_type=F32)

        outs = []
        m = acc = None
        s_next = scores(*units[0])
        for ui, (hh, lo, n) in enumerate(units):
            s = s_next
            if ui + 1 < len(units):
                s_next = scores(*units[ui + 1])
            if lo == chunks[0][0]:
                m = acc = None
            mc = jnp.max(s, axis=-1, keepdims=True)
            m_new = mc if m is None else jnp.maximum(m, mc)
            p = jnp.exp2(s - m_new).astype(BF16)
            pv = jnp.dot(p, v_ref[0, lo:lo + n, hh * LANES:(hh + 1) * LANES],
                         preferred_element_type=F32)
            acc = pv if m is None else jnp.exp2(m - m_new) * acc + pv
            m = m_new
            if lo == chunks[-1][0]:
                outs.append(acc / acc[:, d_v:d_v + 1])
        lane = lax.broadcasted_iota(jnp.int32, (tq, LANES), 1)
        for pp in range(NH // 2):
            pair = jnp.where(lane < d_v, outs[2 * pp], pltpu.roll(outs[2 * pp + 1], d_v, 1))
            o_ref[0, :, pp * LANES:(pp + 1) * LANES] = pair.astype(o_ref.dtype)

    @pl.when(i < n_lat_tiles)
    def _():
        compute(chunks_lat + chunks_ctx)

    @pl.when(i >= n_lat_tiles)
    def _():
        compute(chunks_ctx)


def _key_chunks(lo, hi, size):
    return tuple((a, min(size, hi - a)) for a in range(lo, hi, size))


def _attention(q, k, v, *, l_lat, d_v):
    bsz, s, hp = q.shape
    d_c = H_C * d_v
    kern = functools.partial(_attn_kernel, n_lat_tiles=l_lat // TM,
                             chunks_lat=_key_chunks(0, l_lat, KC), chunks_ctx=_key_chunks(l_lat, s, KC),
                             d_v=d_v)
    return pl.pallas_call(
        kern,
        grid=(bsz, H_C // NH, s // TM),
        in_specs=[pl.BlockSpec((1, TM, NH * LANES), lambda b, h, i: (b, i, h)),
                  pl.BlockSpec((1, s, NH * LANES), lambda b, h, i: (b, 0, h)),
                  pl.BlockSpec((1, s, NH * LANES), lambda b, h, i: (b, 0, h))],
        out_specs=pl.BlockSpec((1, TM, NH * d_v), lambda b, h, i: (b, i, h)),
        out_shape=jax.ShapeDtypeStruct((bsz, s, d_c), BF16),
        compiler_params=_cparams(("arbitrary", "arbitrary", "arbitrary")),
        name="attention",
    )(q, k, v)


def _postmix_kernel(x_ref, mod_ref, ya_ref, gg_ref, hs_ref, yc_ref, wout_ref, g2_ref, *rest,
                    d, n_experts, route):
    if route:
        wr_ref, x1_ref, h2_ref, re_ref, rg_ref = rest
    else:
        x1_ref, h2_ref = rest
    gate1 = mod_ref[0, :, 2 * d:3 * d]
    shift2 = mod_ref[0, :, 3 * d:4 * d]
    scale2 = mod_ref[0, :, 4 * d:5 * d]
    y = jnp.concatenate([ya_ref[0].astype(BF16), (gg_ref[0] * hs_ref[0]).astype(BF16), yc_ref[0]],
                        axis=1)
    x1 = x_ref[0] + gate1 * jnp.dot(y, wout_ref[...], preferred_element_type=F32)
    x1_ref[0] = x1
    h2 = _rms(x1, g2_ref[...]) * (1.0 + scale2) + shift2
    h2_ref[0] = h2.astype(h2_ref.dtype)
    if route:
        logits = jnp.dot(h2, wr_ref[...], preferred_element_type=F32)
        lane = lax.broadcasted_iota(jnp.int32, logits.shape, 1)
        neg_inf = jnp.float32(-jnp.inf)
        lg = jnp.where(lane < n_experts, logits, neg_inf)
        m1 = jnp.max(lg, axis=-1, keepdims=True)
        i1 = jnp.min(jnp.where(lg == m1, lane, LANES), axis=-1, keepdims=True)
        lg2 = jnp.where(lane == i1, neg_inf, lg)
        m2 = jnp.max(lg2, axis=-1, keepdims=True)
        i2 = jnp.min(jnp.where(lg2 == m2, lane, LANES), axis=-1, keepdims=True)
        e = jnp.exp(m2 - m1)
        den = 1.0 + e
        re_ref[0] = jnp.where(lane == 0, i1, i2)
        rg_ref[0] = jnp.where(lane == 0, 1.0 / den, e / den)


def _postmix(xt, mods, ya, gg, hs, yc, wout, g2, wr, *, n_lat_tiles, n_experts):
    bsz, s, d = xt.shape
    route = wr is not None
    tile = lambda b, i: (b, i, 0)
    const2 = lambda b, i: (0, 0)
    full = lambda a: pl.BlockSpec(a.shape, const2)
    in_specs = [pl.BlockSpec((1, TM, d), tile),
                pl.BlockSpec((1, 1, mods.shape[-1]), _tile_mod_index(n_lat_tiles, bsz)),
                pl.BlockSpec((1, TM, ya.shape[-1]), tile), pl.BlockSpec((1, TM, gg.shape[-1]), tile),
                pl.BlockSpec((1, TM, hs.shape[-1]), tile), pl.BlockSpec((1, TM, yc.shape[-1]), tile),
                full(wout), full(g2)]
    args = [xt, mods, ya, gg, hs, yc, wout, g2]
    out_specs = [pl.BlockSpec((1, TM, d), tile), pl.BlockSpec((1, TM, d), tile)]
    out_shape = [jax.ShapeDtypeStruct((bsz, s, d), F32),
                 jax.ShapeDtypeStruct((bsz, s, d), F32 if route else BF16)]
    if route:
        in_specs.append(full(wr))
        args.append(wr)
        out_specs += [pl.BlockSpec((1, TM, LANES), tile), pl.BlockSpec((1, TM, LANES), tile)]
        out_shape += [jax.ShapeDtypeStruct((bsz, s, LANES), jnp.int32),
                      jax.ShapeDtypeStruct((bsz, s, LANES), F32)]
    return pl.pallas_call(
        functools.partial(_postmix_kernel, d=d, n_experts=n_experts, route=route),
        grid=(bsz, s // TM),
        in_specs=in_specs, out_specs=out_specs, out_shape=out_shape,
        compiler_params=_cparams(("arbitrary", "arbitrary")),
        name="postmix_route" if route else "postmix",
    )(*args)


def _ffn_kernel(x1_ref, h2_ref, mod_ref, w1_ref, w3_ref, w2_ref, o_ref, *, d, fc):
    h = h2_ref[0]
    d_ff = w1_ref.shape[1]
    n_chunks = d_ff // fc

    def up(c):
        return (jnp.dot(h, w1_ref[:, c * fc:(c + 1) * fc], preferred_element_type=F32),
                jnp.dot(h, w3_ref[:, c * fc:(c + 1) * fc], preferred_element_type=F32))

    acc = jnp.zeros((h.shape[0], d), F32)
    a, b = up(0)
    for c in range(n_chunks):
        ac, bc = a, b
        if c + 1 < n_chunks:
            a, b = up(c + 1)
        act = (ac * jax.nn.sigmoid(ac) * bc).astype(BF16)
        acc = acc + jnp.dot(act, w2_ref[c * fc:(c + 1) * fc, :], preferred_element_type=F32)
    gate2 = mod_ref[0, :, 5 * d:6 * d]
    o_ref[0] = x1_ref[0] + gate2 * acc


def _ffn(x1, h2, mods, w1, w3, w2, *, n_lat_tiles):
    bsz, s, d = x1.shape
    tile = lambda b, i: (b, i, 0)
    const2 = lambda b, i: (0, 0)
    full = lambda a: pl.BlockSpec(a.shape, const2)
    return pl.pallas_call(
        functools.partial(_ffn_kernel, d=d, fc=256),
        grid=(bsz, s // TM),
        in_specs=[pl.BlockSpec((1, TM, d), tile), pl.BlockSpec((1, TM, d), tile),
                  pl.BlockSpec((1, 1, mods.shape[-1]), _tile_mod_index(n_lat_tiles, bsz)),
                  full(w1), full(w3), full(w2)],
        out_specs=pl.BlockSpec((1, TM, d), tile),
        out_shape=jax.ShapeDtypeStruct((bsz, s, d), F32),
        compiler_params=_cparams(("arbitrary", "arbitrary")),
        name="ffn",
    )(x1, h2, mods, w1, w3, w2)


DMA_UNROLL = 8


def _row_copy(src_ref, dst_ref, sem, src_row, dst_row):
    return pltpu.make_async_copy(src_ref.at[pl.ds(src_row, 1)], dst_ref.at[pl.ds(dst_row, 1)], sem)


def _dispatch_kernel(pos_ref, h_ref, xs_in_hbm, xs_hbm, sem):
    del xs_in_hbm
    n = h_ref.shape[1]

    def start(j, _):
        for kk in range(TOP_K):
            _row_copy(h_ref.at[0], xs_hbm, sem, j, pos_ref[0, 0, kk * n + j]).start()
        return 0

    def wait(j, _):
        for kk in range(TOP_K):
            _row_copy(h_ref.at[0], xs_hbm, sem, j, 0).wait()
        return 0

    lax.fori_loop(0, n, start, 0, unroll=DMA_UNROLL)
    lax.fori_loop(0, n, wait, 0, unroll=DMA_UNROLL)


def _dispatch(pos, h2, n_pad, *, n_tiles):
    bsz, s, d = h2.shape
    xs0 = jnp.zeros((n_pad, d), h2.dtype)
    return pl.pallas_call(
        _dispatch_kernel,
        grid=(bsz, n_tiles),
        in_specs=[pl.BlockSpec((1, 1, TOP_K * TM), lambda b, i: (b * n_tiles + i, 0, 0),
                               memory_space=pltpu.SMEM),
                  pl.BlockSpec((1, TM, d), lambda b, i: (b, i, 0)),
                  pl.BlockSpec(memory_space=pl.ANY)],
        out_specs=pl.BlockSpec(memory_space=pl.ANY),
        out_shape=jax.ShapeDtypeStruct((n_pad, d), h2.dtype),
        scratch_shapes=[pltpu.SemaphoreType.DMA(())],
        input_output_aliases={2: 0},
        compiler_params=_cparams(("arbitrary", "arbitrary")),
        name="moe_dispatch",
    )(pos, h2, xs0)


def _experts_kernel(be_ref, nu_ref, xs_ref, w1_ref, w3_ref, w2_ref, o_ref, xb_ref):
    r = pl.program_id(0)
    f = pl.program_id(1)
    nf = pl.num_programs(1)

    @pl.when(r < nu_ref[0])
    def _():
        @pl.when(f == 0)
        def _():
            xb_ref[...] = xs_ref[...].astype(BF16)

        xb = xb_ref[...]
        a = jnp.dot(xb, w1_ref[0].astype(BF16), preferred_element_type=F32)
        b = jnp.dot(xb, w3_ref[0].astype(BF16), preferred_element_type=F32)
        act = (a * jax.nn.sigmoid(a) * b).astype(BF16)
        y = jnp.dot(act, w2_ref[0].astype(BF16), preferred_element_type=F32)

        @pl.when(f == 0)
        def _():
            o_ref[...] = y

        @pl.when(f > 0)
        def _():
            o_ref[...] += y

    @pl.when(jnp.logical_and(r >= nu_ref[0], f == nf - 1))
    def _():
        o_ref[...] = jnp.zeros_like(o_ref)


def _experts(blk_e, n_used, xs, w1, w3, w2):
    n_pad, d = xs.shape
    n_blk = n_pad // TMB
    d_ff = w1.shape[-1]
    nf = d_ff // TF

    def w_col(r, f, be, nu):
        live = r < nu[0]
        return (be[r], 0, jnp.where(live, f, nf - 1))

    def w_row(r, f, be, nu):
        live = r < nu[0]
        return (be[r], jnp.where(live, f, nf - 1), 0)

    grid_spec = pltpu.PrefetchScalarGridSpec(
        num_scalar_prefetch=2,
        grid=(n_blk, nf),
        in_specs=[pl.BlockSpec((TMB, d), lambda r, f, be, nu: (r, 0)),
                  pl.BlockSpec((1, d, TF), w_col),
                  pl.BlockSpec((1, d, TF), w_col),
                  pl.BlockSpec((1, TF, d), w_row)],
        out_specs=pl.BlockSpec((TMB, d), lambda r, f, be, nu: (r, 0)),
        scratch_shapes=[pltpu.VMEM((TMB, d), BF16)])
    return pl.pallas_call(
        _experts_kernel,
        grid_spec=grid_spec,
        out_shape=jax.ShapeDtypeStruct((n_pad, d), F32),
        compiler_params=_cparams(("arbitrary", "arbitrary")),
        name="moe_experts",
    )(blk_e, n_used, xs, w1, w3, w2)


def _combine_kernel(pos_ref, x1_ref, mod_ref, rg_ref, yp_hbm, *rest, d, final):
    if final:
        gf_ref, o_ref, buf_ref, sem = rest
    else:
        o_ref, buf_ref, sem = rest
    n = x1_ref.shape[1]

    def start(j, _):
        for kk in range(TOP_K):
            _row_copy(yp_hbm, buf_ref.at[kk], sem, pos_ref[0, 0, kk * n + j], j).start()
        return 0

    def wait(j, _):
        for kk in range(TOP_K):
            _row_copy(yp_hbm, buf_ref.at[kk], sem, 0, j).wait()
        return 0

    lax.fori_loop(0, n, start, 0, unroll=DMA_UNROLL)
    lax.fori_loop(0, n, wait, 0, unroll=DMA_UNROLL)
    gate2 = mod_ref[0, :, 5 * d:6 * d]
    rg = rg_ref[0]
    y = rg[:, 0:1] * buf_ref[0]
    for kk in range(1, TOP_K):
        y = y + rg[:, kk:kk + 1] * buf_ref[kk]
    x2 = x1_ref[0] + gate2 * y
    if final:
        x2 = _rms(x2, gf_ref[...])
    o_ref[0] = x2


def _combine(pos, x1, mods, route_g, yp, gf, *, n_lat_tiles, n_tiles):
    bsz, s, d = x1.shape
    final = gf is not None
    tile = lambda b, i: (b, i, 0)
    in_specs = [pl.BlockSpec((1, 1, TOP_K * TM), lambda b, i: (b * n_tiles + i, 0, 0),
                             memory_space=pltpu.SMEM),
                pl.BlockSpec((1, TM, d), tile),
                pl.BlockSpec((1, 1, mods.shape[-1]), _tile_mod_index(n_lat_tiles, bsz)),
                pl.BlockSpec((1, TM, LANES), tile),
                pl.BlockSpec(memory_space=pl.ANY)]
    args = [pos, x1, mods, route_g, yp]
    if final:
        in_specs.append(pl.BlockSpec(gf.shape, lambda b, i: (0, 0)))
        args.append(gf)
    return pl.pallas_call(
        functools.partial(_combine_kernel, d=d, final=final),
        grid=(bsz, n_tiles),
        in_specs=in_specs,
        out_specs=pl.BlockSpec((1, TM, d), tile),
        out_shape=jax.ShapeDtypeStruct((bsz, n_tiles * TM, d), F32),
        scratch_shapes=[pltpu.VMEM((TOP_K, TM, d), F32), pltpu.SemaphoreType.DMA(())],
        compiler_params=_cparams(("arbitrary", "arbitrary")),
        name="moe_combine_final" if final else "moe_combine",
    )(*args)


def _route_tables(route_e, n_experts, s_eff):
    bsz = route_e.shape[0]
    s = s_eff
    n_assign = bsz * s * TOP_K
    lane = jnp.arange(route_e.shape[-1], dtype=jnp.int32)
    flat_e = jnp.stack([jnp.max(jnp.where(lane == k, route_e[:, :s], -1), axis=-1)
                        for k in range(TOP_K)]).reshape(n_assign)
    onehot = (flat_e[:, None] == jnp.arange(n_experts, dtype=jnp.int32)[None, :]).astype(jnp.int32)
    csum = jnp.cumsum(onehot, axis=0)
    rank = jnp.sum(csum * onehot, axis=1) - 1
    counts = csum[-1]
    padded = (counts + TMB - 1) // TMB * TMB
    pad_ends = jnp.cumsum(padded)
    pad_starts = pad_ends - padded
    dest = pad_starts[flat_e] + rank
    n_blk = (n_assign + TMB - 1) // TMB + n_experts
    n_pad = n_blk * TMB
    blk_e = jnp.minimum(
        jnp.searchsorted(pad_ends, jnp.arange(n_blk, dtype=jnp.int32) * TMB, side='right'),
        n_experts - 1).astype(jnp.int32)
    n_used = (pad_ends[-1] // TMB).astype(jnp.int32).reshape(1)
    n_tok_tiles = bsz * s // TM
    pos = dest.astype(jnp.int32).reshape(TOP_K, n_tok_tiles, TM).transpose(1, 0, 2)
    return blk_e, n_used, pos.reshape(n_tok_tiles, 1, TOP_K * TM), n_pad


def _final_kernel(x_ref, g_ref, o_ref):
    o_ref[0] = _rms(x_ref[0], g_ref[...])


def _final_norm(xt, g, *, n_tiles):
    bsz, s, d = xt.shape
    tile = lambda b, i: (b, i, 0)
    return pl.pallas_call(
        _final_kernel,
        grid=(bsz, n_tiles),
        in_specs=[pl.BlockSpec((1, TM, d), tile), pl.BlockSpec(g.shape, lambda b, i: (0, 0))],
        out_specs=pl.BlockSpec((1, TM, d), tile),
        out_shape=jax.ShapeDtypeStruct((bsz, n_tiles * TM, d), F32),
        compiler_params=_cparams(("arbitrary", "arbitrary")),
        name="final_norm",
    )(xt, g)


def _rope_partner(r):
    return jnp.concatenate([-r[..., 8:16], r[..., 0:8], -r[..., 24:32], r[..., 16:24]], axis=-1)


def _rope_tables(l_lat, s_total, d_rope, d_nope):
    t = jnp.arange(l_lat, dtype=jnp.int32)
    row = (t // GRID_W).astype(F32)
    col = (t % GRID_W).astype(F32)
    half = d_rope // 2
    inv_freq = ROPE_BASE ** (-jnp.arange(0, half, 2, dtype=F32) / half)
    ang_r = row[:, None] * inv_freq
    ang_c = col[:, None] * inv_freq
    cos = jnp.concatenate([jnp.cos(ang_r), jnp.cos(ang_r), jnp.cos(ang_c), jnp.cos(ang_c)], axis=1)
    sin = jnp.concatenate([jnp.sin(ang_r), jnp.sin(ang_r), jnp.sin(ang_c), jnp.sin(ang_c)], axis=1)
    n_ctx = s_total - l_lat
    cosk = jnp.concatenate([cos, jnp.ones((n_ctx, d_rope), F32)], axis=0)
    sink = jnp.concatenate([sin, jnp.zeros((n_ctx, d_rope), F32)], axis=0)
    pad = LANES - d_nope - d_rope
    cosq = jnp.concatenate([jnp.ones((s_total, d_nope), F32), cosk, jnp.zeros((s_total, pad), F32)], axis=1)
    sinq = jnp.concatenate([jnp.zeros((s_total, d_nope), F32), sink, jnp.zeros((s_total, pad), F32)], axis=1)
    return cosq, sinq, cosk, sink


def _block_diag(w):
    h, a, b = w.shape
    eye = jnp.eye(h, dtype=w.dtype)
    return (eye[:, None, :, None] * w[:, :, None, :]).reshape(h * a, h * b)


def kernel(x, c, ctx, c_ctx, w_mod, b_mod, norm1_g, norm2_g, w_in, w_out, sgu_ln_g, sgu_ln_b, sgu_w, sgu_b, conv_w, conv_b, lru_w_a, lru_b_a, lru_w_x, lru_b_x, lru_lam, mla_q_norm, mla_w_uq, mla_kv_norm, mla_w_uk, mla_w_uv, ffn_w1, ffn_w3, ffn_w2, moe_router, moe_w1, moe_w3, moe_w2, final_norm_g):
    bsz, l_lat, d = x.shape
    l_ctx = ctx.shape[1]
    s_total = l_lat + l_ctx
    depth = w_mod.shape[0]
    d_a = sgu_ln_g.shape[-1]
    h_a, chunk = sgu_w.shape[1], sgu_w.shape[2]
    d_b = conv_w.shape[-1]
    q_lora = mla_q_norm.shape[-1]
    kv_lora = mla_kv_norm.shape[-1]
    d_c = mla_w_uv.shape[-1]
    d_v = d_c // H_C
    d_nope = mla_w_uk.shape[-1] // H_C
    d_qk = mla_w_uq.shape[-1] // H_C
    d_rope = d_qk - d_nope
    n_experts = moe_router.shape[-1]
    assert l_lat % TM == 0 and l_ctx % TM == 0 and TM % chunk == 0 and TM == T_SCAN
    assert d_qk <= LANES and 2 * d_v == LANES and H_C % 2 == 0 and d_rope == 32
    n_lat_tiles = l_lat // TM
    n_ctx_tiles = l_ctx // TM
    dims = (d, d_a, d_b, q_lora, kv_lora, d_rope, h_a, chunk)

    xt = jnp.concatenate([x, ctx], axis=1)

    n_rows = (bsz + 1 + 7) // 8 * 8
    cond = jnp.zeros((n_rows, d), F32).at[:bsz].set(c).at[bsz].set(c_ctx)
    mods_all = _modulation(cond, w_mod.astype(BF16), b_mod[:, None, :])
    mods_all = mods_all[:, :bsz + 1, None, :]

    cosq, sinq, cosk, sink = _rope_tables(l_lat, s_total, d_rope, d_nope)
    q_scale = float(d_qk) ** -0.5 * 1.4426950408889634
    cosq, sinq = cosq * q_scale, sinq * q_scale
    head_pad = LANES - d_qk
    vone = jnp.tile((jnp.arange(LANES) == d_v).astype(F32)[None], (1, H_C))
    e_head = jnp.concatenate([jnp.zeros((d_rope, d_nope), F32), jnp.eye(d_rope, dtype=F32),
                              jnp.zeros((d_rope, head_pad), F32)], axis=1)
    emat = jnp.tile(e_head, (1, H_C)).astype(BF16)

    out = None
    for l in range(depth):
        last = l == depth - 1
        mods = mods_all[l]
        o_m = 2 * d_a + 2 * d_b
        o_r = o_m + q_lora + kv_lora
        w_rope = w_in[l][:, o_r:o_r + d_rope]
        n_in = (o_r + 2 * d_rope + LANES - 1) // LANES * LANES
        win = jnp.concatenate([w_in[l][:, :o_r + d_rope], _rope_partner(w_rope),
                               jnp.zeros((d, n_in - o_r - 2 * d_rope), F32)], axis=1).astype(BF16)
        wq = mla_w_uq[l].reshape(q_lora, H_C, d_qk)
        zq = jnp.zeros((q_lora, H_C, head_pad), F32)
        wuq = jnp.concatenate([wq, zq], axis=-1).reshape(q_lora, H_C * LANES).astype(BF16)
        wuqp = jnp.concatenate([jnp.zeros((q_lora, H_C, d_nope), F32), _rope_partner(wq[..., d_nope:]), zq],
                               axis=-1).reshape(q_lora, H_C * LANES).astype(BF16)
        wk = mla_w_uk[l].reshape(kv_lora, H_C, d_nope)
        wuk = jnp.concatenate([wk, jnp.zeros((kv_lora, H_C, LANES - d_nope), F32)],
                              axis=-1).reshape(kv_lora, H_C * LANES).astype(BF16)
        wv = mla_w_uv[l].reshape(kv_lora, H_C, d_v)
        wuv = jnp.concatenate([wv, jnp.zeros((kv_lora, H_C, LANES - d_v), F32)],
                              axis=-1).reshape(kv_lora, H_C * LANES).astype(BF16)
        bs_full = jnp.repeat(sgu_b[l].T, d_a // h_a, axis=1)

        ya, gg, xb, q, k, v = _premix(
            xt, mods, norm1_g[l][None], win, sgu_ln_g[l][None], sgu_ln_b[l][None],
            sgu_w[l].astype(BF16), bs_full, mla_q_norm[l][None], wuq, wuqp, mla_kv_norm[l][None],
            wuk, emat, wuv, vone, cosq, sinq, cosk, sink,
            n_lat_tiles=n_lat_tiles, dims=dims)

        wg = jnp.concatenate([_block_diag(lru_w_a[l, 0]), _block_diag(lru_w_x[l, 0]),
                              _block_diag(lru_w_a[l, 1]), _block_diag(lru_w_x[l, 1])], axis=1).astype(BF16)
        bg = jnp.concatenate([lru_b_a[l, 0], lru_b_x[l, 0], lru_b_a[l, 1], lru_b_x[l, 1]])[None]
        hs = _lru(xb, conv_w[l], conv_b[l][None], wg, bg, lru_lam[l], n_lat=n_lat_tiles, n_ctx=n_ctx_tiles)

        yc = _attention(q, k, v, l_lat=l_lat, d_v=d_v)

        if l % 2 == 0:
            i = l // 2
            x1, h2 = _postmix(xt, mods, ya, gg, hs, yc, w_out[l].astype(BF16), norm2_g[l][None], None,
                              n_lat_tiles=n_lat_tiles, n_experts=n_experts)
            xt = _ffn(x1, h2, mods, ffn_w1[i].astype(BF16), ffn_w3[i].astype(BF16), ffn_w2[i].astype(BF16),
                      n_lat_tiles=n_lat_tiles)
            if last:
                out = _final_norm(xt, final_norm_g[None], n_tiles=n_lat_tiles)
        else:
            i = l // 2
            wr = jnp.concatenate([moe_router[i], jnp.zeros((d, LANES - n_experts), F32)], axis=1)
            x1, h2, route_e, route_g = _postmix(xt, mods, ya, gg, hs, yc, w_out[l].astype(BF16),
                                                norm2_g[l][None], wr,
                                                n_lat_tiles=n_lat_tiles, n_experts=n_experts)
            n_tiles = n_lat_tiles if last else n_lat_tiles + n_ctx_tiles
            blk_e, n_used, pos, n_pad = _route_tables(route_e, n_experts, n_tiles * TM)
            xs = _dispatch(pos, h2, n_pad, n_tiles=n_tiles)
            yp = _experts(blk_e, n_used, xs, moe_w1[i], moe_w3[i], moe_w2[i])
            out_or_xt = _combine(pos, x1, mods, route_g, yp, final_norm_g[None] if last else None,
                                 n_lat_tiles=n_lat_tiles, n_tiles=n_tiles)
            if last:
                out = out_or_xt
            else:
                xt = out_or_xt
    return out
```

```python
import functools

import jax
import jax.numpy as jnp
from jax import lax
from jax.experimental import pallas as pl
from jax.experimental.pallas import tpu as pltpu

F32 = jnp.float32
BF16 = jnp.bfloat16

EPS = 1e-6
GRID_W = 64
ROPE_BASE = 10000.0
LRU_C = 8.0
H_C = 8
TOP_K = 2
LANES = 128
TM = 256
T_SCAN = 256
KC = 1024
NH = 4
TMB = 1024
TF = 512
VMEM_LIMIT = 56 * 1024 * 1024


def _cparams(sem):
    return pltpu.CompilerParams(dimension_semantics=sem, vmem_limit_bytes=VMEM_LIMIT)


def _rms(x, g):
    return x * lax.rsqrt(jnp.mean(x * x, axis=-1, keepdims=True) + EPS) * g


def _mod_kernel(c_ref, w_ref, b_ref, o_ref):
    c = c_ref[...]
    a = (c * jax.nn.sigmoid(c)).astype(BF16)
    o_ref[0] = jnp.dot(a, w_ref[0], preferred_element_type=F32) + b_ref[0]


def _modulation(cond, w_mod, b_mod):
    depth, d, n = w_mod.shape
    r = cond.shape[0]
    tn = 1024
    return pl.pallas_call(
        _mod_kernel,
        grid=(depth, n // tn),
        in_specs=[pl.BlockSpec((r, d), lambda l, j: (0, 0)),
                  pl.BlockSpec((1, d, tn), lambda l, j: (l, 0, j)),
                  pl.BlockSpec((1, 1, tn), lambda l, j: (l, 0, j))],
        out_specs=pl.BlockSpec((1, r, tn), lambda l, j: (l, 0, j)),
        out_shape=jax.ShapeDtypeStruct((depth, r, n), F32),
        compiler_params=_cparams(("arbitrary", "arbitrary")),
        name="modulation",
    )(cond, w_mod, b_mod)


def _premix_kernel(x_ref, mod_ref, g_ref, win_ref, lng_ref, lnb_ref, ws_ref, bs_ref,
                   qn_ref, wuq_ref, wuqp_ref, kvn_ref, wuk_ref, e_ref, wuv_ref, vone_ref,
                   cq_ref, sq_ref, ck_ref, sk_ref,
                   ya_ref, gg_ref, xb_ref, q_ref, k_ref, v_ref, *, dims):
    d, d_a, d_b, q_lora, kv_lora, d_rope, h_a, chunk = dims
    x = x_ref[0]
    shift = mod_ref[0, :, 0:d]
    scale = mod_ref[0, :, d:2 * d]
    h = (_rms(x, g_ref[...]) * (1.0 + scale) + shift).astype(BF16)
    z = jnp.dot(h, win_ref[...], preferred_element_type=F32)

    o = 0
    u = jax.nn.gelu(z[:, o:o + d_a])
    v = jax.nn.gelu(z[:, o + d_a:o + 2 * d_a])
    mu = jnp.mean(v, axis=-1, keepdims=True)
    vc = v - mu
    var = jnp.mean(vc * vc, axis=-1, keepdims=True)
    vn = (vc * lax.rsqrt(var + EPS) * lng_ref[...] + lnb_ref[...]).astype(BF16)
    dh_a = d_a // h_a
    tm = x.shape[0]
    head_of_lane = lax.broadcasted_iota(jnp.int32, (chunk, d_a), 1) // dh_a
    for c in range(tm // chunk):
        vch = vn[c * chunk:(c + 1) * chunk]
        s = jnp.dot(ws_ref[0], vch, preferred_element_type=F32)
        for hd in range(1, h_a):
            s = jnp.where(head_of_lane == hd,
                          jnp.dot(ws_ref[hd], vch, preferred_element_type=F32), s)
        s = s + bs_ref[...]
        ya_ref[0, c * chunk:(c + 1) * chunk, :] = (u[c * chunk:(c + 1) * chunk] * s).astype(BF16)

    o = 2 * d_a
    gg_ref[0] = jax.nn.gelu(z[:, o:o + d_b])
    xb_ref[0] = z[:, o + d_b:o + 2 * d_b]

    o = 2 * d_a + 2 * d_b
    cq = _rms(z[:, o:o + q_lora], qn_ref[...]).astype(BF16)
    qa = jnp.dot(cq, wuq_ref[...], preferred_element_type=F32)
    qb = jnp.dot(cq, wuqp_ref[...], preferred_element_type=F32)
    cos_q = jnp.concatenate([cq_ref[...]] * H_C, axis=1)
    sin_q = jnp.concatenate([sq_ref[...]] * H_C, axis=1)
    q_ref[0] = (qa * cos_q + qb * sin_q).astype(BF16)
    o += q_lora
    ckv = _rms(z[:, o:o + kv_lora], kvn_ref[...]).astype(BF16)
    o += kv_lora
    zr = z[:, o:o + d_rope]
    zrp = z[:, o + d_rope:o + 2 * d_rope]
    kr = (zr * ck_ref[...] + zrp * sk_ref[...]).astype(BF16)
    kn = jnp.dot(ckv, wuk_ref[...], preferred_element_type=F32)
    k_ref[0] = (kn + jnp.dot(kr, e_ref[...], preferred_element_type=F32)).astype(BF16)
    v_ref[0] = (jnp.dot(ckv, wuv_ref[...], preferred_element_type=F32) + vone_ref[...]).astype(BF16)


def _tile_mod_index(n_lat_tiles, n_batch):
    def index(b, i):
        return (jnp.where(i < n_lat_tiles, b, n_batch), 0, 0)
    return index


def _premix(xt, mods, g1, win, lng, lnb, ws, bs, qn, wuq, wuqp, kvn, wuk, emat, wuv, vone,
            cosq, sinq, cosk, sink, *, n_lat_tiles, dims):
    bsz, s, d = xt.shape
    d_a, d_b = dims[1], dims[2]
    hp = H_C * LANES
    const2 = lambda b, i: (0, 0)
    const3 = lambda b, i: (0, 0, 0)
    tile = lambda b, i: (b, i, 0)
    full = lambda a: pl.BlockSpec(a.shape, const2 if a.ndim == 2 else const3)
    return pl.pallas_call(
        functools.partial(_premix_kernel, dims=dims),
        grid=(bsz, s // TM),
        in_specs=[pl.BlockSpec((1, TM, d), tile),
                  pl.BlockSpec((1, 1, mods.shape[-1]), _tile_mod_index(n_lat_tiles, bsz)),
                  full(g1), full(win), full(lng), full(lnb), full(ws), full(bs),
                  full(qn), full(wuq), full(wuqp), full(kvn), full(wuk), full(emat), full(wuv), full(vone),
                  pl.BlockSpec((TM, LANES), lambda b, i: (i, 0)),
                  pl.BlockSpec((TM, LANES), lambda b, i: (i, 0)),
                  pl.BlockSpec((TM, cosk.shape[1]), lambda b, i: (i, 0)),
                  pl.BlockSpec((TM, sink.shape[1]), lambda b, i: (i, 0))],
        out_specs=[pl.BlockSpec((1, TM, d_a), tile), pl.BlockSpec((1, TM, d_b), tile),
                   pl.BlockSpec((1, TM, d_b), tile), pl.BlockSpec((1, TM, hp), tile),
                   pl.BlockSpec((1, TM, hp), tile), pl.BlockSpec((1, TM, hp), tile)],
        out_shape=[jax.ShapeDtypeStruct((bsz, s, d_a), BF16),
                   jax.ShapeDtypeStruct((bsz, s, d_b), F32),
                   jax.ShapeDtypeStruct((bsz, s, d_b), F32),
                   jax.ShapeDtypeStruct((bsz, s, hp), BF16),
                   jax.ShapeDtypeStruct((bsz, s, hp), BF16),
                   jax.ShapeDtypeStruct((bsz, s, hp), BF16)],
        compiler_params=_cparams(("arbitrary", "arbitrary")),
        name="premix",
    )(xt, mods, g1, win, lng, lnb, ws, bs, qn, wuq, wuqp, kvn, wuk, emat, wuv, vone,
      cosq, sinq, cosk, sink)


SUBLANES = 8


def _scan_tile(a, b, carry, reverse):
    t, c = a.shape
    g = SUBLANES
    n_groups = t // g
    a = a.reshape(n_groups, g, c)
    b = b.reshape(n_groups, g, c)
    rows = lax.broadcasted_iota(jnp.int32, a.shape, 1)
    s = 1
    while s < g:
        shift = g - s if reverse else s
        ok = rows < g - s if reverse else rows >= s
        a_sh = pltpu.roll(a, shift, 1)
        b_sh = pltpu.roll(b, shift, 1)
        b = b + a * jnp.where(ok, b_sh, 0.0)
        a = a * jnp.where(ok, a_sh, 1.0)
        s *= 2
    hs = [None] * n_groups
    for j in (range(n_groups - 1, -1, -1) if reverse else range(n_groups)):
        hj = b[j] + a[j] * carry
        hs[j] = hj
        carry = hj[0:1] if reverse else hj[g - 1:g]
    return jnp.concatenate(hs, axis=0), carry


def _lru_kernel(xb_ref, gg_ref, cw_ref, cb_ref, wg_ref, bg_ref, lam_ref, out_ref, xc_ref, hf_ref,
                *, n_lat, n_ctx):
    t = T_SCAN
    n_tiles = n_lat + n_ctx
    s_total = n_tiles * t
    d_b = xb_ref.shape[-1]
    w = cw_ref[...]
    cb = cb_ref[...]

    def conv_body(j, _):
        t0 = pl.multiple_of(j * t, t)
        is_ctx = j >= n_lat
        seq_lo = jnp.where(is_ctx, n_lat * t, 0)
        seq_hi = jnp.where(is_ctx, s_total, n_lat * t)
        cur = xb_ref[0, pl.ds(t0, t), :]
        p0 = pl.multiple_of(jnp.maximum(t0 - 8, 0), 8)
        n0 = pl.multiple_of(jnp.minimum(t0 + t, s_total - 8), 8)
        prev = jnp.where(t0 > seq_lo, xb_ref[0, pl.ds(p0, 8), :], 0.0)
        nxt = jnp.where(t0 + t < seq_hi, xb_ref[0, pl.ds(n0, 8), :], 0.0)
        ext = jnp.concatenate([prev, cur, nxt], axis=0)
        n_ext = t + 16
        xm2 = pltpu.roll(ext, 2, 0)[8:8 + t]
        xm1 = pltpu.roll(ext, 1, 0)[8:8 + t]
        xp1 = pltpu.roll(ext, n_ext - 1, 0)[8:8 + t]
        xc_ref[pl.ds(t0, t), :] = (w[0:1] * xm2 + w[1:2] * xm1 + w[2:3] * cur + w[3:4] * xp1 + cb)
        return 0

    lax.fori_loop(0, n_tiles, conv_body, 0)

    lam = lam_ref[...]
    neg = -lam
    softplus = jnp.maximum(neg, 0.0) + jnp.log1p(jnp.exp(-jnp.abs(neg)))

    def direction(dr, reverse):
        sp = softplus[dr:dr + 1]
        wg = wg_ref[:, dr * 2 * d_b:(dr + 1) * 2 * d_b]
        bg = bg_ref[:, dr * 2 * d_b:(dr + 1) * 2 * d_b]

        def body(j, carry):
            if reverse:
                idx = jnp.where(j < n_ctx, n_tiles - 1 - j, n_lat - 1 - (j - n_ctx))
            else:
                idx = jnp.where(j < n_ctx, n_lat + j, j - n_ctx)
            t0 = pl.multiple_of(idx * t, t)
            xc = xc_ref[pl.ds(t0, t), :]
            g = jnp.dot(xc.astype(BF16), wg, preferred_element_type=F32) + bg
            r = jax.nn.sigmoid(g[:, 0:d_b])
            ig = jax.nn.sigmoid(g[:, d_b:2 * d_b])
            log_a = (-LRU_C * r) * sp
            a = jnp.exp(log_a)
            bv = jnp.sqrt(-jnp.tanh(log_a) * (a * a + 1.0)) * (ig * xc)
            h, carry = _scan_tile(a, bv, carry, reverse)
            if reverse:
                y = gg_ref[0, pl.ds(t0, t), :] * (hf_ref[pl.ds(t0, t), :] + h)
                out_ref[0, pl.ds(t0, t), :] = y.astype(out_ref.dtype)
            else:
                hf_ref[pl.ds(t0, t), :] = h
            return carry

        lax.fori_loop(0, n_tiles, body, jnp.zeros((1, d_b), F32))

    direction(0, False)
    direction(1, True)


def _lru(xb, gg, cw, cb, wg, bg, lam, *, n_lat, n_ctx):
    bsz, s, d_b = xb.shape
    const2 = lambda b: (0, 0)
    full = lambda a: pl.BlockSpec(a.shape, const2)
    seq = pl.BlockSpec((1, s, d_b), lambda b: (b, 0, 0))
    return pl.pallas_call(
        functools.partial(_lru_kernel, n_lat=n_lat, n_ctx=n_ctx),
        grid=(bsz,),
        in_specs=[seq, seq, full(cw), full(cb), full(wg), full(bg), full(lam)],
        out_specs=seq,
        out_shape=jax.ShapeDtypeStruct((bsz, s, d_b), BF16),
        scratch_shapes=[pltpu.VMEM((s, d_b), F32), pltpu.VMEM((s, d_b), F32)],
        compiler_params=_cparams(("arbitrary",)),
        name="rglru",
    )(xb, gg, cw, cb, wg, bg, lam)


def _attn_kernel(q_ref, k_ref, v_ref, o_ref, *, n_lat_tiles, chunks_lat, chunks_ctx, d_v):
    i = pl.program_id(2)
    tq = q_ref.shape[1]

    def compute(chunks):
        units = [(hh, lo, n) for hh in range(NH) for lo, n in chunks]

        def scores(hh, lo, n):
            qh = q_ref[0, :, hh * LANES:(hh + 1) * LANES]
            kh = k_ref[0, lo:lo + n, hh * LANES:(hh + 1) * LANES]
            return lax.dot_general(qh, kh, (((1,), (1,)), ((), ())), preferred_element_type=F32)

        outs = []
        m = acc = None
        s_next = scores(*units[0])
        for ui, (hh, lo, n) in enumerate(units):
            s = s_next
            if ui + 1 < len(units):
                s_next = scores(*units[ui + 1])
            if lo == chunks[0][0]:
                m = acc = None
            mc = jnp.max(s, axis=-1, keepdims=True)
            m_new = mc if m is None else jnp.maximum(m, mc)
            p = jnp.exp2(s - m_new).astype(BF16)
            pv = jnp.dot(p, v_ref[0, lo:lo + n, hh * LANES:(hh + 1) * LANES],
                         preferred_element_type=F32)
            acc = pv if m is None else jnp.exp2(m - m_new) * acc + pv
            m = m_new
            if lo == chunks[-1][0]:
                outs.append(acc / acc[:, d_v:d_v + 1])
        lane = lax.broadcasted_iota(jnp.int32, (tq, LANES), 1)
        for pp in range(NH // 2):
            pair = jnp.where(lane < d_v, outs[2 * pp], pltpu.roll(outs[2 * pp + 1], d_v, 1))
            o_ref[0, :, pp * LANES:(pp + 1) * LANES] = pair.astype(o_ref.dtype)

    @pl.when(i < n_lat_tiles)
    def _():
        compute(chunks_lat + chunks_ctx)

    @pl.when(i >= n_lat_tiles)
    def _():
        compute(chunks_ctx)


def _key_chunks(lo, hi, size):
    return tuple((a, min(size, hi - a)) for a in range(lo, hi, size))


def _attention(q, k, v, *, l_lat, d_v):
    bsz, s, hp = q.shape
    d_c = H_C * d_v
    kern = functools.partial(_attn_kernel, n_lat_tiles=l_lat // TM,
                             chunks_lat=_key_chunks(0, l_lat, KC), chunks_ctx=_key_chunks(l_lat, s, KC),
                             d_v=d_v)
    return pl.pallas_call(
        kern,
        grid=(bsz, H_C // NH, s // TM),
        in_specs=[pl.BlockSpec((1, TM, NH * LANES), lambda b, h, i: (b, i, h)),
                  pl.BlockSpec((1, s, NH * LANES), lambda b, h, i: (b, 0, h)),
                  pl.BlockSpec((1, s, NH * LANES), lambda b, h, i: (b, 0, h))],
        out_specs=pl.BlockSpec((1, TM, NH * d_v), lambda b, h, i: (b, i, h)),
        out_shape=jax.ShapeDtypeStruct((bsz, s, d_c), BF16),
        compiler_params=_cparams(("arbitrary", "arbitrary", "arbitrary")),
        name="attention",
    )(q, k, v)


def _postmix_kernel(x_ref, mod_ref, ya_ref, yb_ref, yc_ref, wout_ref, g2_ref, *rest,
                    d, n_experts, route):
    if route:
        wr_ref, x1_ref, h2_ref, re_ref, rg_ref = rest
    else:
        x1_ref, h2_ref = rest
    gate1 = mod_ref[0, :, 2 * d:3 * d]
    shift2 = mod_ref[0, :, 3 * d:4 * d]
    scale2 = mod_ref[0, :, 4 * d:5 * d]
    y = jnp.concatenate([ya_ref[0], yb_ref[0], yc_ref[0]], axis=1)
    x1 = x_ref[0] + gate1 * jnp.dot(y, wout_ref[...], preferred_element_type=F32)
    x1_ref[0] = x1
    h2 = _rms(x1, g2_ref[...]) * (1.0 + scale2) + shift2
    h2_ref[0] = h2.astype(h2_ref.dtype)
    if route:
        logits = jnp.dot(h2, wr_ref[...], preferred_element_type=F32)
        lane = lax.broadcasted_iota(jnp.int32, logits.shape, 1)
        neg_inf = jnp.float32(-jnp.inf)
        lg = jnp.where(lane < n_experts, logits, neg_inf)
        m1 = jnp.max(lg, axis=-1, keepdims=True)
        i1 = jnp.min(jnp.where(lg == m1, lane, LANES), axis=-1, keepdims=True)
        lg2 = jnp.where(lane == i1, neg_inf, lg)
        m2 = jnp.max(lg2, axis=-1, keepdims=True)
        i2 = jnp.min(jnp.where(lg2 == m2, lane, LANES), axis=-1, keepdims=True)
        e = jnp.exp(m2 - m1)
        den = 1.0 + e
        re_ref[0] = jnp.where(lane == 0, i1, i2)
        rg_ref[0] = jnp.where(lane == 0, 1.0 / den, e / den)


def _postmix(xt, mods, ya, yb, yc, wout, g2, wr, *, n_lat_tiles, n_experts):
    bsz, s, d = xt.shape
    route = wr is not None
    tile = lambda b, i: (b, i, 0)
    const2 = lambda b, i: (0, 0)
    full = lambda a: pl.BlockSpec(a.shape, const2)
    in_specs = [pl.BlockSpec((1, TM, d), tile),
                pl.BlockSpec((1, 1, mods.shape[-1]), _tile_mod_index(n_lat_tiles, bsz)),
                pl.BlockSpec((1, TM, ya.shape[-1]), tile), pl.BlockSpec((1, TM, yb.shape[-1]), tile),
                pl.BlockSpec((1, TM, yc.shape[-1]), tile), full(wout), full(g2)]
    args = [xt, mods, ya, yb, yc, wout, g2]
    out_specs = [pl.BlockSpec((1, TM, d), tile), pl.BlockSpec((1, TM, d), tile)]
    out_shape = [jax.ShapeDtypeStruct((bsz, s, d), F32),
                 jax.ShapeDtypeStruct((bsz, s, d), F32 if route else BF16)]
    if route:
        in_specs.append(full(wr))
        args.append(wr)
        out_specs += [pl.BlockSpec((1, TM, LANES), tile), pl.BlockSpec((1, TM, LANES), tile)]
        out_shape += [jax.ShapeDtypeStruct((bsz, s, LANES), jnp.int32),
                      jax.ShapeDtypeStruct((bsz, s, LANES), F32)]
    return pl.pallas_call(
        functools.partial(_postmix_kernel, d=d, n_experts=n_experts, route=route),
        grid=(bsz, s // TM),
        in_specs=in_specs, out_specs=out_specs, out_shape=out_shape,
        compiler_params=_cparams(("arbitrary", "arbitrary")),
        name="postmix_route" if route else "postmix",
    )(*args)


def _ffn_kernel(x1_ref, h2_ref, mod_ref, w1_ref, w3_ref, w2_ref, o_ref, *, d, fc):
    h = h2_ref[0]
    d_ff = w1_ref.shape[1]
    n_chunks = d_ff // fc

    def up(c):
        return (jnp.dot(h, w1_ref[:, c * fc:(c + 1) * fc], preferred_element_type=F32),
                jnp.dot(h, w3_ref[:, c * fc:(c + 1) * fc], preferred_element_type=F32))

    acc = jnp.zeros((h.shape[0], d), F32)
    a, b = up(0)
    for c in range(n_chunks):
        ac, bc = a, b
        if c + 1 < n_chunks:
            a, b = up(c + 1)
        act = (ac * jax.nn.sigmoid(ac) * bc).astype(BF16)
        acc = acc + jnp.dot(act, w2_ref[c * fc:(c + 1) * fc, :], preferred_element_type=F32)
    gate2 = mod_ref[0, :, 5 * d:6 * d]
    o_ref[0] = x1_ref[0] + gate2 * acc


def _ffn(x1, h2, mods, w1, w3, w2, *, n_lat_tiles):
    bsz, s, d = x1.shape
    tile = lambda b, i: (b, i, 0)
    const2 = lambda b, i: (0, 0)
    full = lambda a: pl.BlockSpec(a.shape, const2)
    return pl.pallas_call(
        functools.partial(_ffn_kernel, d=d, fc=256),
        grid=(bsz, s // TM),
        in_specs=[pl.BlockSpec((1, TM, d), tile), pl.BlockSpec((1, TM, d), tile),
                  pl.BlockSpec((1, 1, mods.shape[-1]), _tile_mod_index(n_lat_tiles, bsz)),
                  full(w1), full(w3), full(w2)],
        out_specs=pl.BlockSpec((1, TM, d), tile),
        out_shape=jax.ShapeDtypeStruct((bsz, s, d), F32),
        compiler_params=_cparams(("arbitrary", "arbitrary")),
        name="ffn",
    )(x1, h2, mods, w1, w3, w2)


DMA_UNROLL = 8


def _row_copy(src_ref, dst_ref, sem, src_row, dst_row):
    return pltpu.make_async_copy(src_ref.at[pl.ds(src_row, 1)], dst_ref.at[pl.ds(dst_row, 1)], sem)


def _dispatch_kernel(pos_ref, h_ref, xs_in_hbm, xs_hbm, sem):
    del xs_in_hbm
    n = h_ref.shape[1]

    def start(j, _):
        for kk in range(TOP_K):
            _row_copy(h_ref.at[0], xs_hbm, sem, j, pos_ref[0, 0, kk * n + j]).start()
        return 0

    def wait(j, _):
        for kk in range(TOP_K):
            _row_copy(h_ref.at[0], xs_hbm, sem, j, 0).wait()
        return 0

    lax.fori_loop(0, n, start, 0, unroll=DMA_UNROLL)
    lax.fori_loop(0, n, wait, 0, unroll=DMA_UNROLL)


def _dispatch(pos, h2, n_pad, *, n_tiles):
    bsz, s, d = h2.shape
    xs0 = jnp.zeros((n_pad, d), h2.dtype)
    return pl.pallas_call(
        _dispatch_kernel,
        grid=(bsz, n_tiles),
        in_specs=[pl.BlockSpec((1, 1, TOP_K * TM), lambda b, i: (b * n_tiles + i, 0, 0),
                               memory_space=pltpu.SMEM),
                  pl.BlockSpec((1, TM, d), lambda b, i: (b, i, 0)),
                  pl.BlockSpec(memory_space=pl.ANY)],
        out_specs=pl.BlockSpec(memory_space=pl.ANY),
        out_shape=jax.ShapeDtypeStruct((n_pad, d), h2.dtype),
        scratch_shapes=[pltpu.SemaphoreType.DMA(())],
        input_output_aliases={2: 0},
        compiler_params=_cparams(("arbitrary", "arbitrary")),
        name="moe_dispatch",
    )(pos, h2, xs0)


def _experts_kernel(be_ref, nu_ref, xs_ref, w1_ref, w3_ref, w2_ref, o_ref, xb_ref, acc_ref):
    r = pl.program_id(0)
    f = pl.program_id(1)
    nf = pl.num_programs(1)

    @pl.when(r < nu_ref[0])
    def _():
        @pl.when(f == 0)
        def _():
            xb_ref[...] = xs_ref[...].astype(BF16)
            acc_ref[...] = jnp.zeros_like(acc_ref)

        xb = xb_ref[...]
        a = jnp.dot(xb, w1_ref[0, 0].astype(BF16), preferred_element_type=F32)
        b = jnp.dot(xb, w3_ref[0, 0].astype(BF16), preferred_element_type=F32)
        act = (a * jax.nn.sigmoid(a) * b).astype(BF16)
        acc_ref[...] += jnp.dot(act, w2_ref[0, 0].astype(BF16), preferred_element_type=F32)

        @pl.when(f == nf - 1)
        def _():
            o_ref[...] = acc_ref[...]

    @pl.when(jnp.logical_and(r >= nu_ref[0], f == nf - 1))
    def _():
        o_ref[...] = jnp.zeros_like(o_ref)


def _experts(blk_e, n_used, xs, w1, w3, w2, layer):
    n_pad, d = xs.shape
    n_blk = n_pad // TMB
    d_ff = w1.shape[-1]
    nf = d_ff // TF

    def w_col(r, f, be, nu):
        live = r < nu[0]
        return (layer, be[r], 0, jnp.where(live, f, nf - 1))

    def w_row(r, f, be, nu):
        live = r < nu[0]
        return (layer, be[r], jnp.where(live, f, nf - 1), 0)

    grid_spec = pltpu.PrefetchScalarGridSpec(
        num_scalar_prefetch=2,
        grid=(n_blk, nf),
        in_specs=[pl.BlockSpec((TMB, d), lambda r, f, be, nu: (r, 0)),
                  pl.BlockSpec((1, 1, d, TF), w_col),
                  pl.BlockSpec((1, 1, d, TF), w_col),
                  pl.BlockSpec((1, 1, TF, d), w_row)],
        out_specs=pl.BlockSpec((TMB, d), lambda r, f, be, nu: (r, 0)),
        scratch_shapes=[pltpu.VMEM((TMB, d), BF16), pltpu.VMEM((TMB, d), F32)])
    return pl.pallas_call(
        _experts_kernel,
        grid_spec=grid_spec,
        out_shape=jax.ShapeDtypeStruct((n_pad, d), F32),
        compiler_params=_cparams(("arbitrary", "arbitrary")),
        name="moe_experts",
    )(blk_e, n_used, xs, w1, w3, w2)


def _combine_kernel(pos_ref, x1_ref, mod_ref, rg_ref, yp_hbm, *rest, d, final):
    if final:
        gf_ref, o_ref, buf_ref, sem = rest
    else:
        o_ref, buf_ref, sem = rest
    n = x1_ref.shape[1]

    def start(j, _):
        for kk in range(TOP_K):
            _row_copy(yp_hbm, buf_ref.at[kk], sem, pos_ref[0, 0, kk * n + j], j).start()
        return 0

    def wait(j, _):
        for kk in range(TOP_K):
            _row_copy(yp_hbm, buf_ref.at[kk], sem, 0, j).wait()
        return 0

    lax.fori_loop(0, n, start, 0, unroll=DMA_UNROLL)
    lax.fori_loop(0, n, wait, 0, unroll=DMA_UNROLL)
    gate2 = mod_ref[0, :, 5 * d:6 * d]
    rg = rg_ref[0]
    y = rg[:, 0:1] * buf_ref[0]
    for kk in range(1, TOP_K):
        y = y + rg[:, kk:kk + 1] * buf_ref[kk]
    x2 = x1_ref[0] + gate2 * y
    if final:
        x2 = _rms(x2, gf_ref[...])
    o_ref[0] = x2


def _combine(pos, x1, mods, route_g, yp, gf, *, n_lat_tiles, n_tiles):
    bsz, s, d = x1.shape
    final = gf is not None
    tile = lambda b, i: (b, i, 0)
    in_specs = [pl.BlockSpec((1, 1, TOP_K * TM), lambda b, i: (b * n_tiles + i, 0, 0),
                             memory_space=pltpu.SMEM),
                pl.BlockSpec((1, TM, d), tile),
                pl.BlockSpec((1, 1, mods.shape[-1]), _tile_mod_index(n_lat_tiles, bsz)),
                pl.BlockSpec((1, TM, LANES), tile),
                pl.BlockSpec(memory_space=pl.ANY)]
    args = [pos, x1, mods, route_g, yp]
    if final:
        in_specs.append(pl.BlockSpec(gf.shape, lambda b, i: (0, 0)))
        args.append(gf)
    return pl.pallas_call(
        functools.partial(_combine_kernel, d=d, final=final),
        grid=(bsz, n_tiles),
        in_specs=in_specs,
        out_specs=pl.BlockSpec((1, TM, d), tile),
        out_shape=jax.ShapeDtypeStruct((bsz, n_tiles * TM, d), F32),
        scratch_shapes=[pltpu.VMEM((TOP_K, TM, d), F32), pltpu.SemaphoreType.DMA(())],
        compiler_params=_cparams(("arbitrary", "arbitrary")),
        name="moe_combine_final" if final else "moe_combine",
    )(*args)


def _route_tables(route_e, n_experts, s_eff):
    bsz = route_e.shape[0]
    s = s_eff
    n_assign = bsz * s * TOP_K
    lane = jnp.arange(route_e.shape[-1], dtype=jnp.int32)
    flat_e = jnp.stack([jnp.max(jnp.where(lane == k, route_e[:, :s], -1), axis=-1)
                        for k in range(TOP_K)]).reshape(n_assign)
    onehot = (flat_e[:, None] == jnp.arange(n_experts, dtype=jnp.int32)[None, :]).astype(jnp.int32)
    csum = jnp.cumsum(onehot, axis=0)
    rank = jnp.sum(csum * onehot, axis=1) - 1
    counts = csum[-1]
    padded = (counts + TMB - 1) // TMB * TMB
    pad_ends = jnp.cumsum(padded)
    pad_starts = pad_ends - padded
    dest = pad_starts[flat_e] + rank
    n_blk = (n_assign + TMB - 1) // TMB + n_experts
    n_pad = n_blk * TMB
    blk_e = jnp.minimum(
        jnp.searchsorted(pad_ends, jnp.arange(n_blk, dtype=jnp.int32) * TMB, side='right'),
        n_experts - 1).astype(jnp.int32)
    n_used = (pad_ends[-1] // TMB).astype(jnp.int32).reshape(1)
    n_tok_tiles = bsz * s // TM
    pos = dest.astype(jnp.int32).reshape(TOP_K, n_tok_tiles, TM).transpose(1, 0, 2)
    return blk_e, n_used, pos.reshape(n_tok_tiles, 1, TOP_K * TM), n_pad


def _final_kernel(x_ref, g_ref, o_ref):
    o_ref[0] = _rms(x_ref[0], g_ref[...])


def _final_norm(xt, g, *, n_tiles):
    bsz, s, d = xt.shape
    tile = lambda b, i: (b, i, 0)
    return pl.pallas_call(
        _final_kernel,
        grid=(bsz, n_tiles),
        in_specs=[pl.BlockSpec((1, TM, d), tile), pl.BlockSpec(g.shape, lambda b, i: (0, 0))],
        out_specs=pl.BlockSpec((1, TM, d), tile),
        out_shape=jax.ShapeDtypeStruct((bsz, n_tiles * TM, d), F32),
        compiler_params=_cparams(("arbitrary", "arbitrary")),
        name="final_norm",
    )(xt, g)


def _rope_partner(r):
    return jnp.concatenate([-r[..., 8:16], r[..., 0:8], -r[..., 24:32], r[..., 16:24]], axis=-1)


def _rope_tables(l_lat, s_total, d_rope, d_nope):
    t = jnp.arange(l_lat, dtype=jnp.int32)
    row = (t // GRID_W).astype(F32)
    col = (t % GRID_W).astype(F32)
    half = d_rope // 2
    inv_freq = ROPE_BASE ** (-jnp.arange(0, half, 2, dtype=F32) / half)
    ang_r = row[:, None] * inv_freq
    ang_c = col[:, None] * inv_freq
    cos = jnp.concatenate([jnp.cos(ang_r), jnp.cos(ang_r), jnp.cos(ang_c), jnp.cos(ang_c)], axis=1)
    sin = jnp.concatenate([jnp.sin(ang_r), jnp.sin(ang_r), jnp.sin(ang_c), jnp.sin(ang_c)], axis=1)
    n_ctx = s_total - l_lat
    cosk = jnp.concatenate([cos, jnp.ones((n_ctx, d_rope), F32)], axis=0)
    sink = jnp.concatenate([sin, jnp.zeros((n_ctx, d_rope), F32)], axis=0)
    pad = LANES - d_nope - d_rope
    cosq = jnp.concatenate([jnp.ones((s_total, d_nope), F32), cosk, jnp.zeros((s_total, pad), F32)], axis=1)
    sinq = jnp.concatenate([jnp.zeros((s_total, d_nope), F32), sink, jnp.zeros((s_total, pad), F32)], axis=1)
    return cosq, sinq, cosk, sink


def _block_diag(w):
    h, a, b = w.shape
    eye = jnp.eye(h, dtype=w.dtype)
    return (eye[:, None, :, None] * w[:, :, None, :]).reshape(h * a, h * b)


def kernel(x, c, ctx, c_ctx, w_mod, b_mod, norm1_g, norm2_g, w_in, w_out, sgu_ln_g, sgu_ln_b, sgu_w, sgu_b, conv_w, conv_b, lru_w_a, lru_b_a, lru_w_x, lru_b_x, lru_lam, mla_q_norm, mla_w_uq, mla_kv_norm, mla_w_uk, mla_w_uv, ffn_w1, ffn_w3, ffn_w2, moe_router, moe_w1, moe_w3, moe_w2, final_norm_g):
    bsz, l_lat, d = x.shape
    l_ctx = ctx.shape[1]
    s_total = l_lat + l_ctx
    depth = w_mod.shape[0]
    d_a = sgu_ln_g.shape[-1]
    h_a, chunk = sgu_w.shape[1], sgu_w.shape[2]
    d_b = conv_w.shape[-1]
    q_lora = mla_q_norm.shape[-1]
    kv_lora = mla_kv_norm.shape[-1]
    d_c = mla_w_uv.shape[-1]
    d_v = d_c // H_C
    d_nope = mla_w_uk.shape[-1] // H_C
    d_qk = mla_w_uq.shape[-1] // H_C
    d_rope = d_qk - d_nope
    n_experts = moe_router.shape[-1]
    assert l_lat % TM == 0 and l_ctx % TM == 0 and TM % chunk == 0 and TM == T_SCAN
    assert d_qk <= LANES and 2 * d_v == LANES and H_C % 2 == 0 and d_rope == 32
    n_lat_tiles = l_lat // TM
    n_ctx_tiles = l_ctx // TM
    dims = (d, d_a, d_b, q_lora, kv_lora, d_rope, h_a, chunk)

    xt = jnp.concatenate([x, ctx], axis=1)

    n_rows = (bsz + 1 + 7) // 8 * 8
    cond = jnp.zeros((n_rows, d), F32).at[:bsz].set(c).at[bsz].set(c_ctx)
    mods_all = _modulation(cond, w_mod.astype(BF16), b_mod[:, None, :])
    mods_all = mods_all[:, :bsz + 1, None, :]

    cosq, sinq, cosk, sink = _rope_tables(l_lat, s_total, d_rope, d_nope)
    q_scale = float(d_qk) ** -0.5 * 1.4426950408889634
    cosq, sinq = cosq * q_scale, sinq * q_scale
    head_pad = LANES - d_qk
    vone = jnp.tile((jnp.arange(LANES) == d_v).astype(F32)[None], (1, H_C))
    e_head = jnp.concatenate([jnp.zeros((d_rope, d_nope), F32), jnp.eye(d_rope, dtype=F32),
                              jnp.zeros((d_rope, head_pad), F32)], axis=1)
    emat = jnp.tile(e_head, (1, H_C)).astype(BF16)

    out = None
    for l in range(depth):
        last = l == depth - 1
        mods = mods_all[l]
        o_m = 2 * d_a + 2 * d_b
        o_r = o_m + q_lora + kv_lora
        w_rope = w_in[l][:, o_r:o_r + d_rope]
        n_in = (o_r + 2 * d_rope + LANES - 1) // LANES * LANES
        win = jnp.concatenate([w_in[l][:, :o_r + d_rope], _rope_partner(w_rope),
                               jnp.zeros((d, n_in - o_r - 2 * d_rope), F32)], axis=1).astype(BF16)
        wq = mla_w_uq[l].reshape(q_lora, H_C, d_qk)
        zq = jnp.zeros((q_lora, H_C, head_pad), F32)
        wuq = jnp.concatenate([wq, zq], axis=-1).reshape(q_lora, H_C * LANES).astype(BF16)
        wuqp = jnp.concatenate([jnp.zeros((q_lora, H_C, d_nope), F32), _rope_partner(wq[..., d_nope:]), zq],
                               axis=-1).reshape(q_lora, H_C * LANES).astype(BF16)
        wk = mla_w_uk[l].reshape(kv_lora, H_C, d_nope)
        wuk = jnp.concatenate([wk, jnp.zeros((kv_lora, H_C, LANES - d_nope), F32)],
                              axis=-1).reshape(kv_lora, H_C * LANES).astype(BF16)
        wv = mla_w_uv[l].reshape(kv_lora, H_C, d_v)
        wuv = jnp.concatenate([wv, jnp.zeros((kv_lora, H_C, LANES - d_v), F32)],
                              axis=-1).reshape(kv_lora, H_C * LANES).astype(BF16)
        bs_full = jnp.repeat(sgu_b[l].T, d_a // h_a, axis=1)

        ya, gg, xb, q, k, v = _premix(
            xt, mods, norm1_g[l][None], win, sgu_ln_g[l][None], sgu_ln_b[l][None],
            sgu_w[l].astype(BF16), bs_full, mla_q_norm[l][None], wuq, wuqp, mla_kv_norm[l][None],
            wuk, emat, wuv, vone, cosq, sinq, cosk, sink,
            n_lat_tiles=n_lat_tiles, dims=dims)

        wg = jnp.concatenate([_block_diag(lru_w_a[l, 0]), _block_diag(lru_w_x[l, 0]),
                              _block_diag(lru_w_a[l, 1]), _block_diag(lru_w_x[l, 1])], axis=1).astype(BF16)
        bg = jnp.concatenate([lru_b_a[l, 0], lru_b_x[l, 0], lru_b_a[l, 1], lru_b_x[l, 1]])[None]
        yb = _lru(xb, gg, conv_w[l], conv_b[l][None], wg, bg, lru_lam[l], n_lat=n_lat_tiles, n_ctx=n_ctx_tiles)

        yc = _attention(q, k, v, l_lat=l_lat, d_v=d_v)

        if l % 2 == 0:
            i = l // 2
            x1, h2 = _postmix(xt, mods, ya, yb, yc, w_out[l].astype(BF16), norm2_g[l][None], None,
                              n_lat_tiles=n_lat_tiles, n_experts=n_experts)
            xt = _ffn(x1, h2, mods, ffn_w1[i].astype(BF16), ffn_w3[i].astype(BF16), ffn_w2[i].astype(BF16),
                      n_lat_tiles=n_lat_tiles)
            if last:
                out = _final_norm(xt, final_norm_g[None], n_tiles=n_lat_tiles)
        else:
            i = l // 2
            wr = jnp.concatenate([moe_router[i], jnp.zeros((d, LANES - n_experts), F32)], axis=1)
            x1, h2, route_e, route_g = _postmix(xt, mods, ya, yb, yc, w_out[l].astype(BF16),
                                                norm2_g[l][None], wr,
                                                n_lat_tiles=n_lat_tiles, n_experts=n_experts)
            n_tiles = n_lat_tiles if last else n_lat_tiles + n_ctx_tiles
            blk_e, n_used, pos, n_pad = _route_tables(route_e, n_experts, n_tiles * TM)
            xs = _dispatch(pos, h2, n_pad, n_tiles=n_tiles)
            yp = _experts(blk_e, n_used, xs, moe_w1, moe_w3, moe_w2, i)
            out_or_xt = _combine(pos, x1, mods, route_g, yp, final_norm_g[None] if last else None,
                                 n_lat_tiles=n_lat_tiles, n_tiles=n_tiles)
            if last:
                out = out_or_xt
            else:
                xt = out_or_xt
    return out
```

```python
import functools

import jax
import jax.numpy as jnp
from jax import lax
from jax.experimental import pallas as pl
from jax.experimental.pallas import tpu as pltpu

F32 = jnp.float32
BF16 = jnp.bfloat16

EPS = 1e-6
GRID_W = 64
ROPE_BASE = 10000.0
LRU_C = 8.0
H_C = 8
TOP_K = 2
LANES = 128
TM = 256
T_SCAN = 256
KC = 1024
NH = 4
TQ = 512
TMB = 1024
TF = 512
VMEM_LIMIT = 56 * 1024 * 1024


def _cparams(sem):
    return pltpu.CompilerParams(dimension_semantics=sem, vmem_limit_bytes=VMEM_LIMIT)


def _rms(x, g):
    return x * lax.rsqrt(jnp.mean(x * x, axis=-1, keepdims=True) + EPS) * g


def _mod_kernel(c_ref, w_ref, b_ref, o_ref):
    c = c_ref[...]
    a = (c * jax.nn.sigmoid(c)).astype(BF16)
    o_ref[0] = jnp.dot(a, w_ref[0], preferred_element_type=F32) + b_ref[0]


def _modulation(cond, w_mod, b_mod):
    depth, d, n = w_mod.shape
    r = cond.shape[0]
    tn = 1024
    return pl.pallas_call(
        _mod_kernel,
        grid=(depth, n // tn),
        in_specs=[pl.BlockSpec((r, d), lambda l, j: (0, 0)),
                  pl.BlockSpec((1, d, tn), lambda l, j: (l, 0, j)),
                  pl.BlockSpec((1, 1, tn), lambda l, j: (l, 0, j))],
        out_specs=pl.BlockSpec((1, r, tn), lambda l, j: (l, 0, j)),
        out_shape=jax.ShapeDtypeStruct((depth, r, n), F32),
        compiler_params=_cparams(("arbitrary", "arbitrary")),
        name="modulation",
    )(cond, w_mod, b_mod)


def _premix_kernel(x_ref, mod_ref, g_ref, win_ref, lng_ref, lnb_ref, ws_ref, bs_ref,
                   qn_ref, wuq_ref, wuqp_ref, kvn_ref, wuk_ref, e_ref, wuv_ref, vone_ref,
                   cq_ref, sq_ref, ck_ref, sk_ref,
                   ya_ref, gg_ref, xb_ref, q_ref, k_ref, v_ref, *, dims):
    d, d_a, d_b, q_lora, kv_lora, d_rope, h_a, chunk = dims
    x = x_ref[0]
    shift = mod_ref[0, :, 0:d]
    scale = mod_ref[0, :, d:2 * d]
    h = (_rms(x, g_ref[...]) * (1.0 + scale) + shift).astype(BF16)
    z = jnp.dot(h, win_ref[...], preferred_element_type=F32)

    o = 0
    u = jax.nn.gelu(z[:, o:o + d_a])
    v = jax.nn.gelu(z[:, o + d_a:o + 2 * d_a])
    mu = jnp.mean(v, axis=-1, keepdims=True)
    vc = v - mu
    var = jnp.mean(vc * vc, axis=-1, keepdims=True)
    vn = (vc * lax.rsqrt(var + EPS) * lng_ref[...] + lnb_ref[...]).astype(BF16)
    dh_a = d_a // h_a
    tm = x.shape[0]
    head_of_lane = lax.broadcasted_iota(jnp.int32, (chunk, d_a), 1) // dh_a
    for c in range(tm // chunk):
        vch = vn[c * chunk:(c + 1) * chunk]
        s = jnp.dot(ws_ref[0], vch, preferred_element_type=F32)
        for hd in range(1, h_a):
            s = jnp.where(head_of_lane == hd,
                          jnp.dot(ws_ref[hd], vch, preferred_element_type=F32), s)
        s = s + bs_ref[...]
        ya_ref[0, c * chunk:(c + 1) * chunk, :] = (u[c * chunk:(c + 1) * chunk] * s).astype(BF16)

    o = 2 * d_a
    gg_ref[0] = jax.nn.gelu(z[:, o:o + d_b])
    xb_ref[0] = z[:, o + d_b:o + 2 * d_b]

    o = 2 * d_a + 2 * d_b
    cq = _rms(z[:, o:o + q_lora], qn_ref[...]).astype(BF16)
    qa = jnp.dot(cq, wuq_ref[...], preferred_element_type=F32)
    qb = jnp.dot(cq, wuqp_ref[...], preferred_element_type=F32)
    cos_q = jnp.concatenate([cq_ref[...]] * H_C, axis=1)
    sin_q = jnp.concatenate([sq_ref[...]] * H_C, axis=1)
    q_ref[0] = (qa * cos_q + qb * sin_q).astype(BF16)
    o += q_lora
    ckv = _rms(z[:, o:o + kv_lora], kvn_ref[...]).astype(BF16)
    o += kv_lora
    zr = z[:, o:o + d_rope]
    zrp = z[:, o + d_rope:o + 2 * d_rope]
    kr = (zr * ck_ref[...] + zrp * sk_ref[...]).astype(BF16)
    kn = jnp.dot(ckv, wuk_ref[...], preferred_element_type=F32)
    k_ref[0] = (kn + jnp.dot(kr, e_ref[...], preferred_element_type=F32)).astype(BF16)
    v_ref[0] = (jnp.dot(ckv, wuv_ref[...], preferred_element_type=F32) + vone_ref[...]).astype(BF16)


def _tile_mod_index(n_lat_tiles, n_batch):
    def index(b, i):
        return (jnp.where(i < n_lat_tiles, b, n_batch), 0, 0)
    return index


def _premix(xt, mods, g1, win, lng, lnb, ws, bs, qn, wuq, wuqp, kvn, wuk, emat, wuv, vone,
            cosq, sinq, cosk, sink, *, n_lat_tiles, dims):
    bsz, s, d = xt.shape
    d_a, d_b = dims[1], dims[2]
    hp = H_C * LANES
    const2 = lambda b, i: (0, 0)
    const3 = lambda b, i: (0, 0, 0)
    tile = lambda b, i: (b, i, 0)
    full = lambda a: pl.BlockSpec(a.shape, const2 if a.ndim == 2 else const3)
    return pl.pallas_call(
        functools.partial(_premix_kernel, dims=dims),
        grid=(bsz, s // TM),
        in_specs=[pl.BlockSpec((1, TM, d), tile),
                  pl.BlockSpec((1, 1, mods.shape[-1]), _tile_mod_index(n_lat_tiles, bsz)),
                  full(g1), full(win), full(lng), full(lnb), full(ws), full(bs),
                  full(qn), full(wuq), full(wuqp), full(kvn), full(wuk), full(emat), full(wuv), full(vone),
                  pl.BlockSpec((TM, LANES), lambda b, i: (i, 0)),
                  pl.BlockSpec((TM, LANES), lambda b, i: (i, 0)),
                  pl.BlockSpec((TM, cosk.shape[1]), lambda b, i: (i, 0)),
                  pl.BlockSpec((TM, sink.shape[1]), lambda b, i: (i, 0))],
        out_specs=[pl.BlockSpec((1, TM, d_a), tile), pl.BlockSpec((1, TM, d_b), tile),
                   pl.BlockSpec((1, TM, d_b), tile), pl.BlockSpec((1, TM, hp), tile),
                   pl.BlockSpec((1, TM, hp), tile), pl.BlockSpec((1, TM, hp), tile)],
        out_shape=[jax.ShapeDtypeStruct((bsz, s, d_a), BF16),
                   jax.ShapeDtypeStruct((bsz, s, d_b), F32),
                   jax.ShapeDtypeStruct((bsz, s, d_b), F32),
                   jax.ShapeDtypeStruct((bsz, s, hp), BF16),
                   jax.ShapeDtypeStruct((bsz, s, hp), BF16),
                   jax.ShapeDtypeStruct((bsz, s, hp), BF16)],
        compiler_params=_cparams(("arbitrary", "arbitrary")),
        name="premix",
    )(xt, mods, g1, win, lng, lnb, ws, bs, qn, wuq, wuqp, kvn, wuk, emat, wuv, vone,
      cosq, sinq, cosk, sink)


SUBLANES = 8


def _scan_tile(a, b, carry, reverse):
    t, c = a.shape
    g = SUBLANES
    n_groups = t // g
    a = a.reshape(n_groups, g, c)
    b = b.reshape(n_groups, g, c)
    rows = lax.broadcasted_iota(jnp.int32, a.shape, 1)
    s = 1
    while s < g:
        shift = g - s if reverse else s
        ok = rows < g - s if reverse else rows >= s
        a_sh = pltpu.roll(a, shift, 1)
        b_sh = pltpu.roll(b, shift, 1)
        b = b + a * jnp.where(ok, b_sh, 0.0)
        a = a * jnp.where(ok, a_sh, 1.0)
        s *= 2
    hs = [None] * n_groups
    for j in (range(n_groups - 1, -1, -1) if reverse else range(n_groups)):
        hj = b[j] + a[j] * carry
        hs[j] = hj
        carry = hj[0:1] if reverse else hj[g - 1:g]
    return jnp.concatenate(hs, axis=0), carry


def _lru_kernel(xb_ref, gg_ref, cw_ref, cb_ref, wg_ref, bg_ref, lam_ref, out_ref, xc_ref, hf_ref,
                *, n_lat, n_ctx):
    t = T_SCAN
    n_tiles = n_lat + n_ctx
    s_total = n_tiles * t
    d_b = xb_ref.shape[-1]
    w = cw_ref[...]
    cb = cb_ref[...]

    def conv_body(j, _):
        t0 = pl.multiple_of(j * t, t)
        is_ctx = j >= n_lat
        seq_lo = jnp.where(is_ctx, n_lat * t, 0)
        seq_hi = jnp.where(is_ctx, s_total, n_lat * t)
        cur = xb_ref[0, pl.ds(t0, t), :]
        p0 = pl.multiple_of(jnp.maximum(t0 - 8, 0), 8)
        n0 = pl.multiple_of(jnp.minimum(t0 + t, s_total - 8), 8)
        prev = jnp.where(t0 > seq_lo, xb_ref[0, pl.ds(p0, 8), :], 0.0)
        nxt = jnp.where(t0 + t < seq_hi, xb_ref[0, pl.ds(n0, 8), :], 0.0)
        ext = jnp.concatenate([prev, cur, nxt], axis=0)
        n_ext = t + 16
        xm2 = pltpu.roll(ext, 2, 0)[8:8 + t]
        xm1 = pltpu.roll(ext, 1, 0)[8:8 + t]
        xp1 = pltpu.roll(ext, n_ext - 1, 0)[8:8 + t]
        xc_ref[pl.ds(t0, t), :] = (w[0:1] * xm2 + w[1:2] * xm1 + w[2:3] * cur + w[3:4] * xp1 + cb)
        return 0

    lax.fori_loop(0, n_tiles, conv_body, 0)

    lam = lam_ref[...]
    neg = -lam
    softplus = jnp.maximum(neg, 0.0) + jnp.log1p(jnp.exp(-jnp.abs(neg)))

    def direction(dr, reverse):
        sp = softplus[dr:dr + 1]
        wg = wg_ref[:, dr * 2 * d_b:(dr + 1) * 2 * d_b]
        bg = bg_ref[:, dr * 2 * d_b:(dr + 1) * 2 * d_b]

        def body(j, carry):
            if reverse:
                idx = jnp.where(j < n_ctx, n_tiles - 1 - j, n_lat - 1 - (j - n_ctx))
            else:
                idx = jnp.where(j < n_ctx, n_lat + j, j - n_ctx)
            t0 = pl.multiple_of(idx * t, t)
            xc = xc_ref[pl.ds(t0, t), :]
            g = jnp.dot(xc.astype(BF16), wg, preferred_element_type=F32) + bg
            r = jax.nn.sigmoid(g[:, 0:d_b])
            ig = jax.nn.sigmoid(g[:, d_b:2 * d_b])
            log_a = (-LRU_C * r) * sp
            a = jnp.exp(log_a)
            bv = jnp.sqrt(-jnp.tanh(log_a) * (a * a + 1.0)) * (ig * xc)
            h, carry = _scan_tile(a, bv, carry, reverse)
            if reverse:
                y = gg_ref[0, pl.ds(t0, t), :] * (hf_ref[pl.ds(t0, t), :] + h)
                out_ref[0, pl.ds(t0, t), :] = y.astype(out_ref.dtype)
            else:
                hf_ref[pl.ds(t0, t), :] = h
            return carry

        lax.fori_loop(0, n_tiles, body, jnp.zeros((1, d_b), F32))

    direction(0, False)
    direction(1, True)


def _lru(xb, gg, cw, cb, wg, bg, lam, *, n_lat, n_ctx):
    bsz, s, d_b = xb.shape
    const2 = lambda b: (0, 0)
    full = lambda a: pl.BlockSpec(a.shape, const2)
    seq = pl.BlockSpec((1, s, d_b), lambda b: (b, 0, 0))
    return pl.pallas_call(
        functools.partial(_lru_kernel, n_lat=n_lat, n_ctx=n_ctx),
        grid=(bsz,),
        in_specs=[seq, seq, full(cw), full(cb), full(wg), full(bg), full(lam)],
        out_specs=seq,
        out_shape=jax.ShapeDtypeStruct((bsz, s, d_b), BF16),
        scratch_shapes=[pltpu.VMEM((s, d_b), F32), pltpu.VMEM((s, d_b), F32)],
        compiler_params=_cparams(("arbitrary",)),
        name="rglru",
    )(xb, gg, cw, cb, wg, bg, lam)


def _attn_kernel(q_ref, k_ref, v_ref, *rest, chunks, d_v):
    o_ref = rest[-1]
    tq = q_ref.shape[1]
    units = [(hh, lo, n) for hh in range(NH) for lo, n in chunks]

    def scores(hh, lo, n):
        qh = q_ref[0, :, hh * LANES:(hh + 1) * LANES]
        kh = k_ref[0, lo:lo + n, hh * LANES:(hh + 1) * LANES]
        return lax.dot_general(qh, kh, (((1,), (1,)), ((), ())), preferred_element_type=F32)

    outs = []
    m = acc = None
    s_next = scores(*units[0])
    for ui, (hh, lo, n) in enumerate(units):
        s = s_next
        if ui + 1 < len(units):
            s_next = scores(*units[ui + 1])
        if lo == chunks[0][0]:
            m = acc = None
        mc = jnp.max(s, axis=-1, keepdims=True)
        m_new = mc if m is None else jnp.maximum(m, mc)
        p = jnp.exp2(s - m_new).astype(BF16)
        pv = jnp.dot(p, v_ref[0, lo:lo + n, hh * LANES:(hh + 1) * LANES],
                     preferred_element_type=F32)
        acc = pv if m is None else jnp.exp2(m - m_new) * acc + pv
        m = m_new
        if lo == chunks[-1][0]:
            outs.append(acc / acc[:, d_v:d_v + 1])
    lane = lax.broadcasted_iota(jnp.int32, (tq, LANES), 1)
    for pp in range(NH // 2):
        pair = jnp.where(lane < d_v, outs[2 * pp], pltpu.roll(outs[2 * pp + 1], d_v, 1))
        o_ref[0, :, pp * LANES:(pp + 1) * LANES] = pair.astype(o_ref.dtype)


def _key_chunks(lo, hi, size):
    return tuple((a, min(size, hi - a)) for a in range(lo, hi, size))


def _attention(q, k, v, *, l_lat, d_v, ctx_queries):
    bsz, s, hp = q.shape
    l_ctx = s - l_lat
    d_c = H_C * d_v
    sem = ("arbitrary", "arbitrary", "arbitrary")
    out_shape = jax.ShapeDtypeStruct((bsz, s, d_c), BF16)
    y = pl.pallas_call(
        functools.partial(_attn_kernel, chunks=_key_chunks(0, s, KC), d_v=d_v),
        grid=(bsz, H_C // NH, l_lat // TQ),
        in_specs=[pl.BlockSpec((1, TQ, NH * LANES), lambda b, h, i: (b, i, h)),
                  pl.BlockSpec((1, s, NH * LANES), lambda b, h, i: (b, 0, h)),
                  pl.BlockSpec((1, s, NH * LANES), lambda b, h, i: (b, 0, h))],
        out_specs=pl.BlockSpec((1, TQ, NH * d_v), lambda b, h, i: (b, i, h)),
        out_shape=out_shape,
        compiler_params=_cparams(sem),
        name="attention",
    )(q, k, v)
    if not ctx_queries:
        return y
    first = l_lat // l_ctx
    return pl.pallas_call(
        functools.partial(_attn_kernel, chunks=_key_chunks(0, l_ctx, KC), d_v=d_v),
        grid=(bsz, H_C // NH, 1),
        in_specs=[pl.BlockSpec((1, l_ctx, NH * LANES), lambda b, h, i: (b, first, h)),
                  pl.BlockSpec((1, l_ctx, NH * LANES), lambda b, h, i: (b, first, h)),
                  pl.BlockSpec((1, l_ctx, NH * LANES), lambda b, h, i: (b, first, h)),
                  pl.BlockSpec(memory_space=pl.ANY)],
        out_specs=pl.BlockSpec((1, l_ctx, NH * d_v), lambda b, h, i: (b, first, h)),
        out_shape=out_shape,
        input_output_aliases={3: 0},
        compiler_params=_cparams(sem),
        name="attention_ctx",
    )(q, k, v, y)


def _postmix_kernel(x_ref, mod_ref, ya_ref, yb_ref, yc_ref, wout_ref, g2_ref, *rest,
                    d, n_experts, route):
    if route:
        wr_ref, x1_ref, h2_ref, re_ref, rg_ref = rest
    else:
        x1_ref, h2_ref = rest
    gate1 = mod_ref[0, :, 2 * d:3 * d]
    shift2 = mod_ref[0, :, 3 * d:4 * d]
    scale2 = mod_ref[0, :, 4 * d:5 * d]
    y = jnp.concatenate([ya_ref[0], yb_ref[0], yc_ref[0]], axis=1)
    x1 = x_ref[0] + gate1 * jnp.dot(y, wout_ref[...], preferred_element_type=F32)
    x1_ref[0] = x1
    h2 = _rms(x1, g2_ref[...]) * (1.0 + scale2) + shift2
    h2_ref[0] = h2.astype(h2_ref.dtype)
    if route:
        logits = jnp.dot(h2, wr_ref[...], preferred_element_type=F32)
        lane = lax.broadcasted_iota(jnp.int32, logits.shape, 1)
        neg_inf = jnp.float32(-jnp.inf)
        lg = jnp.where(lane < n_experts, logits, neg_inf)
        m1 = jnp.max(lg, axis=-1, keepdims=True)
        i1 = jnp.min(jnp.where(lg == m1, lane, LANES), axis=-1, keepdims=True)
        lg2 = jnp.where(lane == i1, neg_inf, lg)
        m2 = jnp.max(lg2, axis=-1, keepdims=True)
        i2 = jnp.min(jnp.where(lg2 == m2, lane, LANES), axis=-1, keepdims=True)
        e = jnp.exp(m2 - m1)
        den = 1.0 + e
        re_ref[0] = jnp.where(lane == 0, i1, i2)
        rg_ref[0] = jnp.where(lane == 0, 1.0 / den, e / den)


def _postmix(xt, mods, ya, yb, yc, wout, g2, wr, *, n_lat_tiles, n_tiles, n_experts):
    bsz, s, d = xt.shape
    route = wr is not None
    tile = lambda b, i: (b, i, 0)
    const2 = lambda b, i: (0, 0)
    full = lambda a: pl.BlockSpec(a.shape, const2)
    in_specs = [pl.BlockSpec((1, TM, d), tile),
                pl.BlockSpec((1, 1, mods.shape[-1]), _tile_mod_index(n_lat_tiles, bsz)),
                pl.BlockSpec((1, TM, ya.shape[-1]), tile), pl.BlockSpec((1, TM, yb.shape[-1]), tile),
                pl.BlockSpec((1, TM, yc.shape[-1]), tile), full(wout), full(g2)]
    args = [xt, mods, ya, yb, yc, wout, g2]
    out_specs = [pl.BlockSpec((1, TM, d), tile), pl.BlockSpec((1, TM, d), tile)]
    out_shape = [jax.ShapeDtypeStruct((bsz, s, d), F32),
                 jax.ShapeDtypeStruct((bsz, s, d), F32 if route else BF16)]
    if route:
        in_specs.append(full(wr))
        args.append(wr)
        out_specs += [pl.BlockSpec((1, TM, LANES), tile), pl.BlockSpec((1, TM, LANES), tile)]
        out_shape += [jax.ShapeDtypeStruct((bsz, s, LANES), jnp.int32),
                      jax.ShapeDtypeStruct((bsz, s, LANES), F32)]
    return pl.pallas_call(
        functools.partial(_postmix_kernel, d=d, n_experts=n_experts, route=route),
        grid=(bsz, n_tiles),
        in_specs=in_specs, out_specs=out_specs, out_shape=out_shape,
        compiler_params=_cparams(("arbitrary", "arbitrary")),
        name="postmix_route" if route else "postmix",
    )(*args)


def _ffn_kernel(x1_ref, h2_ref, mod_ref, w1_ref, w3_ref, w2_ref, o_ref, *, d, fc):
    h = h2_ref[0]
    d_ff = w1_ref.shape[1]
    n_chunks = d_ff // fc

    def up(c):
        return (jnp.dot(h, w1_ref[:, c * fc:(c + 1) * fc], preferred_element_type=F32),
                jnp.dot(h, w3_ref[:, c * fc:(c + 1) * fc], preferred_element_type=F32))

    acc = jnp.zeros((h.shape[0], d), F32)
    a, b = up(0)
    for c in range(n_chunks):
        ac, bc = a, b
        if c + 1 < n_chunks:
            a, b = up(c + 1)
        act = (ac * jax.nn.sigmoid(ac) * bc).astype(BF16)
        acc = acc + jnp.dot(act, w2_ref[c * fc:(c + 1) * fc, :], preferred_element_type=F32)
    gate2 = mod_ref[0, :, 5 * d:6 * d]
    o_ref[0] = x1_ref[0] + gate2 * acc


def _ffn(x1, h2, mods, w1, w3, w2, *, n_lat_tiles, n_tiles):
    bsz, s, d = x1.shape
    tile = lambda b, i: (b, i, 0)
    const2 = lambda b, i: (0, 0)
    full = lambda a: pl.BlockSpec(a.shape, const2)
    return pl.pallas_call(
        functools.partial(_ffn_kernel, d=d, fc=256),
        grid=(bsz, n_tiles),
        in_specs=[pl.BlockSpec((1, TM, d), tile), pl.BlockSpec((1, TM, d), tile),
                  pl.BlockSpec((1, 1, mods.shape[-1]), _tile_mod_index(n_lat_tiles, bsz)),
                  full(w1), full(w3), full(w2)],
        out_specs=pl.BlockSpec((1, TM, d), tile),
        out_shape=jax.ShapeDtypeStruct((bsz, s, d), F32),
        compiler_params=_cparams(("arbitrary", "arbitrary")),
        name="ffn",
    )(x1, h2, mods, w1, w3, w2)


DMA_UNROLL = 8


def _row_copy(src_ref, dst_ref, sem, src_row, dst_row):
    return pltpu.make_async_copy(src_ref.at[pl.ds(src_row, 1)], dst_ref.at[pl.ds(dst_row, 1)], sem)


def _dispatch_kernel(pos_ref, h_ref, xs_in_hbm, xs_hbm, sem):
    del xs_in_hbm
    n = h_ref.shape[1]

    def start(j, _):
        for kk in range(TOP_K):
            _row_copy(h_ref.at[0], xs_hbm, sem, j, pos_ref[0, 0, kk * n + j]).start()
        return 0

    def wait(j, _):
        for kk in range(TOP_K):
            _row_copy(h_ref.at[0], xs_hbm, sem, j, 0).wait()
        return 0

    lax.fori_loop(0, n, start, 0, unroll=DMA_UNROLL)
    lax.fori_loop(0, n, wait, 0, unroll=DMA_UNROLL)


def _dispatch(pos, h2, n_pad, *, n_tiles):
    bsz, s, d = h2.shape
    xs0 = jnp.zeros((n_pad, d), h2.dtype)
    return pl.pallas_call(
        _dispatch_kernel,
        grid=(bsz, n_tiles),
        in_specs=[pl.BlockSpec((1, 1, TOP_K * TM), lambda b, i: (b * n_tiles + i, 0, 0),
                               memory_space=pltpu.SMEM),
                  pl.BlockSpec((1, TM, d), lambda b, i: (b, i, 0)),
                  pl.BlockSpec(memory_space=pl.ANY)],
        out_specs=pl.BlockSpec(memory_space=pl.ANY),
        out_shape=jax.ShapeDtypeStruct((n_pad, d), h2.dtype),
        scratch_shapes=[pltpu.SemaphoreType.DMA(())],
        input_output_aliases={2: 0},
        compiler_params=_cparams(("arbitrary", "arbitrary")),
        name="moe_dispatch",
    )(pos, h2, xs0)


def _experts_kernel(be_ref, nu_ref, xs_ref, w1_ref, w3_ref, w2_ref, o_ref, xb_ref, acc_ref):
    r = pl.program_id(0)
    f = pl.program_id(1)
    nf = pl.num_programs(1)

    @pl.when(r < nu_ref[0])
    def _():
        @pl.when(f == 0)
        def _():
            xb_ref[...] = xs_ref[...].astype(BF16)
            acc_ref[...] = jnp.zeros_like(acc_ref)

        xb = xb_ref[...]
        a = jnp.dot(xb, w1_ref[0, 0].astype(BF16), preferred_element_type=F32)
        b = jnp.dot(xb, w3_ref[0, 0].astype(BF16), preferred_element_type=F32)
        act = (a * jax.nn.sigmoid(a) * b).astype(BF16)
        acc_ref[...] += jnp.dot(act, w2_ref[0, 0].astype(BF16), preferred_element_type=F32)

        @pl.when(f == nf - 1)
        def _():
            o_ref[...] = acc_ref[...]

    @pl.when(jnp.logical_and(r >= nu_ref[0], f == nf - 1))
    def _():
        o_ref[...] = jnp.zeros_like(o_ref)


def _experts(blk_e, n_used, xs, w1, w3, w2, layer):
    n_pad, d = xs.shape
    n_blk = n_pad // TMB
    d_ff = w1.shape[-1]
    nf = d_ff // TF

    def w_col(r, f, be, nu):
        live = r < nu[0]
        return (layer, be[r], 0, jnp.where(live, f, nf - 1))

    def w_row(r, f, be, nu):
        live = r < nu[0]
        return (layer, be[r], jnp.where(live, f, nf - 1), 0)

    grid_spec = pltpu.PrefetchScalarGridSpec(
        num_scalar_prefetch=2,
        grid=(n_blk, nf),
        in_specs=[pl.BlockSpec((TMB, d), lambda r, f, be, nu: (r, 0)),
                  pl.BlockSpec((1, 1, d, TF), w_col),
                  pl.BlockSpec((1, 1, d, TF), w_col),
                  pl.BlockSpec((1, 1, TF, d), w_row)],
        out_specs=pl.BlockSpec((TMB, d), lambda r, f, be, nu: (r, 0)),
        scratch_shapes=[pltpu.VMEM((TMB, d), BF16), pltpu.VMEM((TMB, d), F32)])
    return pl.pallas_call(
        _experts_kernel,
        grid_spec=grid_spec,
        out_shape=jax.ShapeDtypeStruct((n_pad, d), F32),
        compiler_params=_cparams(("arbitrary", "arbitrary")),
        name="moe_experts",
    )(blk_e, n_used, xs, w1, w3, w2)


def _combine_kernel(pos_ref, x1_ref, mod_ref, rg_ref, yp_hbm, *rest, d, final):
    if final:
        gf_ref, o_ref, buf_ref, sem = rest
    else:
        o_ref, buf_ref, sem = rest
    n = x1_ref.shape[1]

    def start(j, _):
        for kk in range(TOP_K):
            _row_copy(yp_hbm, buf_ref.at[kk], sem, pos_ref[0, 0, kk * n + j], j).start()
        return 0

    def wait(j, _):
        for kk in range(TOP_K):
            _row_copy(yp_hbm, buf_ref.at[kk], sem, 0, j).wait()
        return 0

    lax.fori_loop(0, n, start, 0, unroll=DMA_UNROLL)
    lax.fori_loop(0, n, wait, 0, unroll=DMA_UNROLL)
    gate2 = mod_ref[0, :, 5 * d:6 * d]
    rg = rg_ref[0]
    y = rg[:, 0:1] * buf_ref[0]
    for kk in range(1, TOP_K):
        y = y + rg[:, kk:kk + 1] * buf_ref[kk]
    x2 = x1_ref[0] + gate2 * y
    if final:
        x2 = _rms(x2, gf_ref[...])
    o_ref[0] = x2


def _combine(pos, x1, mods, route_g, yp, gf, *, n_lat_tiles, n_tiles):
    bsz, s, d = x1.shape
    final = gf is not None
    tile = lambda b, i: (b, i, 0)
    in_specs = [pl.BlockSpec((1, 1, TOP_K * TM), lambda b, i: (b * n_tiles + i, 0, 0),
                             memory_space=pltpu.SMEM),
                pl.BlockSpec((1, TM, d), tile),
                pl.BlockSpec((1, 1, mods.shape[-1]), _tile_mod_index(n_lat_tiles, bsz)),
                pl.BlockSpec((1, TM, LANES), tile),
                pl.BlockSpec(memory_space=pl.ANY)]
    args = [pos, x1, mods, route_g, yp]
    if final:
        in_specs.append(pl.BlockSpec(gf.shape, lambda b, i: (0, 0)))
        args.append(gf)
    return pl.pallas_call(
        functools.partial(_combine_kernel, d=d, final=final),
        grid=(bsz, n_tiles),
        in_specs=in_specs,
        out_specs=pl.BlockSpec((1, TM, d), tile),
        out_shape=jax.ShapeDtypeStruct((bsz, n_tiles * TM, d), F32),
        scratch_shapes=[pltpu.VMEM((TOP_K, TM, d), F32), pltpu.SemaphoreType.DMA(())],
        compiler_params=_cparams(("arbitrary", "arbitrary")),
        name="moe_combine_final" if final else "moe_combine",
    )(*args)


def _route_tables(route_e, n_experts, s_eff):
    bsz = route_e.shape[0]
    s = s_eff
    n_assign = bsz * s * TOP_K
    lane = jnp.arange(route_e.shape[-1], dtype=jnp.int32)
    flat_e = jnp.stack([jnp.max(jnp.where(lane == k, route_e[:, :s], -1), axis=-1)
                        for k in range(TOP_K)]).reshape(n_assign)
    onehot = (flat_e[:, None] == jnp.arange(n_experts, dtype=jnp.int32)[None, :]).astype(jnp.int32)
    csum = jnp.cumsum(onehot, axis=0)
    rank = jnp.sum(csum * onehot, axis=1) - 1
    counts = csum[-1]
    padded = (counts + TMB - 1) // TMB * TMB
    pad_ends = jnp.cumsum(padded)
    pad_starts = pad_ends - padded
    dest = pad_starts[flat_e] + rank
    n_blk = (n_assign + TMB - 1) // TMB + n_experts
    n_pad = n_blk * TMB
    blk_e = jnp.minimum(
        jnp.searchsorted(pad_ends, jnp.arange(n_blk, dtype=jnp.int32) * TMB, side='right'),
        n_experts - 1).astype(jnp.int32)
    n_used = (pad_ends[-1] // TMB).astype(jnp.int32).reshape(1)
    n_tok_tiles = bsz * s // TM
    pos = dest.astype(jnp.int32).reshape(TOP_K, n_tok_tiles, TM).transpose(1, 0, 2)
    return blk_e, n_used, pos.reshape(n_tok_tiles, 1, TOP_K * TM), n_pad


def _final_kernel(x_ref, g_ref, o_ref):
    o_ref[0] = _rms(x_ref[0], g_ref[...])


def _final_norm(xt, g, *, n_tiles):
    bsz, s, d = xt.shape
    tile = lambda b, i: (b, i, 0)
    return pl.pallas_call(
        _final_kernel,
        grid=(bsz, n_tiles),
        in_specs=[pl.BlockSpec((1, TM, d), tile), pl.BlockSpec(g.shape, lambda b, i: (0, 0))],
        out_specs=pl.BlockSpec((1, TM, d), tile),
        out_shape=jax.ShapeDtypeStruct((bsz, n_tiles * TM, d), F32),
        compiler_params=_cparams(("arbitrary", "arbitrary")),
        name="final_norm",
    )(xt, g)


def _rope_partner(r):
    return jnp.concatenate([-r[..., 8:16], r[..., 0:8], -r[..., 24:32], r[..., 16:24]], axis=-1)


def _rope_tables(l_lat, s_total, d_rope, d_nope):
    t = jnp.arange(l_lat, dtype=jnp.int32)
    row = (t // GRID_W).astype(F32)
    col = (t % GRID_W).astype(F32)
    half = d_rope // 2
    inv_freq = ROPE_BASE ** (-jnp.arange(0, half, 2, dtype=F32) / half)
    ang_r = row[:, None] * inv_freq
    ang_c = col[:, None] * inv_freq
    cos = jnp.concatenate([jnp.cos(ang_r), jnp.cos(ang_r), jnp.cos(ang_c), jnp.cos(ang_c)], axis=1)
    sin = jnp.concatenate([jnp.sin(ang_r), jnp.sin(ang_r), jnp.sin(ang_c), jnp.sin(ang_c)], axis=1)
    n_ctx = s_total - l_lat
    cosk = jnp.concatenate([cos, jnp.ones((n_ctx, d_rope), F32)], axis=0)
    sink = jnp.concatenate([sin, jnp.zeros((n_ctx, d_rope), F32)], axis=0)
    pad = LANES - d_nope - d_rope
    cosq = jnp.concatenate([jnp.ones((s_total, d_nope), F32), cosk, jnp.zeros((s_total, pad), F32)], axis=1)
    sinq = jnp.concatenate([jnp.zeros((s_total, d_nope), F32), sink, jnp.zeros((s_total, pad), F32)], axis=1)
    return cosq, sinq, cosk, sink


def _block_diag(w):
    h, a, b = w.shape
    eye = jnp.eye(h, dtype=w.dtype)
    return (eye[:, None, :, None] * w[:, :, None, :]).reshape(h * a, h * b)


def kernel(x, c, ctx, c_ctx, w_mod, b_mod, norm1_g, norm2_g, w_in, w_out, sgu_ln_g, sgu_ln_b, sgu_w, sgu_b, conv_w, conv_b, lru_w_a, lru_b_a, lru_w_x, lru_b_x, lru_lam, mla_q_norm, mla_w_uq, mla_kv_norm, mla_w_uk, mla_w_uv, ffn_w1, ffn_w3, ffn_w2, moe_router, moe_w1, moe_w3, moe_w2, final_norm_g):
    bsz, l_lat, d = x.shape
    l_ctx = ctx.shape[1]
    s_total = l_lat + l_ctx
    depth = w_mod.shape[0]
    d_a = sgu_ln_g.shape[-1]
    h_a, chunk = sgu_w.shape[1], sgu_w.shape[2]
    d_b = conv_w.shape[-1]
    q_lora = mla_q_norm.shape[-1]
    kv_lora = mla_kv_norm.shape[-1]
    d_c = mla_w_uv.shape[-1]
    d_v = d_c // H_C
    d_nope = mla_w_uk.shape[-1] // H_C
    d_qk = mla_w_uq.shape[-1] // H_C
    d_rope = d_qk - d_nope
    n_experts = moe_router.shape[-1]
    assert l_lat % TM == 0 and l_ctx % TM == 0 and TM % chunk == 0 and TM == T_SCAN
    assert l_lat % TQ == 0 and l_lat % l_ctx == 0
    assert d_qk <= LANES and 2 * d_v == LANES and H_C % 2 == 0 and d_rope == 32
    n_lat_tiles = l_lat // TM
    n_ctx_tiles = l_ctx // TM
    dims = (d, d_a, d_b, q_lora, kv_lora, d_rope, h_a, chunk)

    xt = jnp.concatenate([x, ctx], axis=1)

    n_rows = (bsz + 1 + 7) // 8 * 8
    cond = jnp.zeros((n_rows, d), F32).at[:bsz].set(c).at[bsz].set(c_ctx)
    mods_all = _modulation(cond, w_mod.astype(BF16), b_mod[:, None, :])
    mods_all = mods_all[:, :bsz + 1, None, :]

    cosq, sinq, cosk, sink = _rope_tables(l_lat, s_total, d_rope, d_nope)
    q_scale = float(d_qk) ** -0.5 * 1.4426950408889634
    cosq, sinq = cosq * q_scale, sinq * q_scale
    head_pad = LANES - d_qk
    vone = jnp.tile((jnp.arange(LANES) == d_v).astype(F32)[None], (1, H_C))
    e_head = jnp.concatenate([jnp.zeros((d_rope, d_nope), F32), jnp.eye(d_rope, dtype=F32),
                              jnp.zeros((d_rope, head_pad), F32)], axis=1)
    emat = jnp.tile(e_head, (1, H_C)).astype(BF16)

    out = None
    for l in range(depth):
        last = l == depth - 1
        mods = mods_all[l]
        o_m = 2 * d_a + 2 * d_b
        o_r = o_m + q_lora + kv_lora
        w_rope = w_in[l][:, o_r:o_r + d_rope]
        n_in = (o_r + 2 * d_rope + LANES - 1) // LANES * LANES
        win = jnp.concatenate([w_in[l][:, :o_r + d_rope], _rope_partner(w_rope),
                               jnp.zeros((d, n_in - o_r - 2 * d_rope), F32)], axis=1).astype(BF16)
        wq = mla_w_uq[l].reshape(q_lora, H_C, d_qk)
        zq = jnp.zeros((q_lora, H_C, head_pad), F32)
        wuq = jnp.concatenate([wq, zq], axis=-1).reshape(q_lora, H_C * LANES).astype(BF16)
        wuqp = jnp.concatenate([jnp.zeros((q_lora, H_C, d_nope), F32), _rope_partner(wq[..., d_nope:]), zq],
                               axis=-1).reshape(q_lora, H_C * LANES).astype(BF16)
        wk = mla_w_uk[l].reshape(kv_lora, H_C, d_nope)
        wuk = jnp.concatenate([wk, jnp.zeros((kv_lora, H_C, LANES - d_nope), F32)],
                              axis=-1).reshape(kv_lora, H_C * LANES).astype(BF16)
        wv = mla_w_uv[l].reshape(kv_lora, H_C, d_v)
        wuv = jnp.concatenate([wv, jnp.zeros((kv_lora, H_C, LANES - d_v), F32)],
                              axis=-1).reshape(kv_lora, H_C * LANES).astype(BF16)
        bs_full = jnp.repeat(sgu_b[l].T, d_a // h_a, axis=1)

        ya, gg, xb, q, k, v = _premix(
            xt, mods, norm1_g[l][None], win, sgu_ln_g[l][None], sgu_ln_b[l][None],
            sgu_w[l].astype(BF16), bs_full, mla_q_norm[l][None], wuq, wuqp, mla_kv_norm[l][None],
            wuk, emat, wuv, vone, cosq, sinq, cosk, sink,
            n_lat_tiles=n_lat_tiles, dims=dims)

        wg = jnp.concatenate([_block_diag(lru_w_a[l, 0]), _block_diag(lru_w_x[l, 0]),
                              _block_diag(lru_w_a[l, 1]), _block_diag(lru_w_x[l, 1])], axis=1).astype(BF16)
        bg = jnp.concatenate([lru_b_a[l, 0], lru_b_x[l, 0], lru_b_a[l, 1], lru_b_x[l, 1]])[None]
        yb = _lru(xb, gg, conv_w[l], conv_b[l][None], wg, bg, lru_lam[l], n_lat=n_lat_tiles, n_ctx=n_ctx_tiles)

        n_tiles = n_lat_tiles if last else n_lat_tiles + n_ctx_tiles
        yc = _attention(q, k, v, l_lat=l_lat, d_v=d_v, ctx_queries=not last)

        if l % 2 == 0:
            i = l // 2
            x1, h2 = _postmix(xt, mods, ya, yb, yc, w_out[l].astype(BF16), norm2_g[l][None], None,
                              n_lat_tiles=n_lat_tiles, n_tiles=n_tiles, n_experts=n_experts)
            xt = _ffn(x1, h2, mods, ffn_w1[i].astype(BF16), ffn_w3[i].astype(BF16), ffn_w2[i].astype(BF16),
                      n_lat_tiles=n_lat_tiles, n_tiles=n_tiles)
            if last:
                out = _final_norm(xt, final_norm_g[None], n_tiles=n_lat_tiles)
        else:
            i = l // 2
            wr = jnp.concatenate([moe_router[i], jnp.zeros((d, LANES - n_experts), F32)], axis=1)
            x1, h2, route_e, route_g = _postmix(xt, mods, ya, yb, yc, w_out[l].astype(BF16),
                                                norm2_g[l][None], wr,
                                                n_lat_tiles=n_lat_tiles, n_tiles=n_tiles, n_experts=n_experts)
            blk_e, n_used, pos, n_pad = _route_tables(route_e, n_experts, n_tiles * TM)
            xs = _dispatch(pos, h2, n_pad, n_tiles=n_tiles)
            yp = _experts(blk_e, n_used, xs, moe_w1, moe_w3, moe_w2, i)
            out_or_xt = _combine(pos, x1, mods, route_g, yp, final_norm_g[None] if last else None,
                                 n_lat_tiles=n_lat_tiles, n_tiles=n_tiles)
            if last:
                out = out_or_xt
            else:
                xt = out_or_xt
    return out
```

```python
import functools

import jax
import jax.numpy as jnp
from jax import lax
from jax.experimental import pallas as pl
from jax.experimental.pallas import tpu as pltpu

F32 = jnp.float32
BF16 = jnp.bfloat16

EPS = 1e-6
GRID_W = 64
ROPE_BASE = 10000.0
LRU_C = 8.0
H_C = 8
TOP_K = 2
LANES = 128
TM = 256
T_SCAN = 256
KC = 1024
NH = 4
TQ = 512
TMB = 1024
TF = 512
VMEM_LIMIT = 56 * 1024 * 1024


def _cparams(sem):
    return pltpu.CompilerParams(dimension_semantics=sem, vmem_limit_bytes=VMEM_LIMIT)


def _rms(x, g):
    return x * lax.rsqrt(jnp.mean(x * x, axis=-1, keepdims=True) + EPS) * g


def _mod_kernel(c_ref, w_ref, b_ref, o_ref):
    c = c_ref[...]
    a = (c * jax.nn.sigmoid(c)).astype(BF16)
    o_ref[0] = jnp.dot(a, w_ref[0], preferred_element_type=F32) + b_ref[0]


def _modulation(cond, w_mod, b_mod):
    depth, d, n = w_mod.shape
    r = cond.shape[0]
    tn = 1024
    return pl.pallas_call(
        _mod_kernel,
        grid=(depth, n // tn),
        in_specs=[pl.BlockSpec((r, d), lambda l, j: (0, 0)),
                  pl.BlockSpec((1, d, tn), lambda l, j: (l, 0, j)),
                  pl.BlockSpec((1, 1, tn), lambda l, j: (l, 0, j))],
        out_specs=pl.BlockSpec((1, r, tn), lambda l, j: (l, 0, j)),
        out_shape=jax.ShapeDtypeStruct((depth, r, n), F32),
        compiler_params=_cparams(("arbitrary", "arbitrary")),
        name="modulation",
    )(cond, w_mod, b_mod)


def _premix_kernel(x_ref, mod_ref, g_ref, win_ref, lng_ref, lnb_ref, ws_ref, bs_ref,
                   qn_ref, wuq_ref, wuqp_ref, kvn_ref, wuk_ref, e_ref, wuv_ref, vone_ref,
                   cq_ref, sq_ref, ck_ref, sk_ref,
                   ya_ref, gg_ref, xb_ref, q_ref, k_ref, v_ref, *, dims):
    d, d_a, d_b, q_lora, kv_lora, d_rope, h_a, chunk = dims
    x = x_ref[0]
    shift = mod_ref[0, :, 0:d]
    scale = mod_ref[0, :, d:2 * d]
    h = (_rms(x, g_ref[...]) * (1.0 + scale) + shift).astype(BF16)
    z = jnp.dot(h, win_ref[...], preferred_element_type=F32)

    o = 0
    u = jax.nn.gelu(z[:, o:o + d_a])
    v = jax.nn.gelu(z[:, o + d_a:o + 2 * d_a])
    mu = jnp.mean(v, axis=-1, keepdims=True)
    vc = v - mu
    var = jnp.mean(vc * vc, axis=-1, keepdims=True)
    vn = (vc * lax.rsqrt(var + EPS) * lng_ref[...] + lnb_ref[...]).astype(BF16)
    dh_a = d_a // h_a
    tm = x.shape[0]
    head_of_lane = lax.broadcasted_iota(jnp.int32, (chunk, d_a), 1) // dh_a
    for c in range(tm // chunk):
        vch = vn[c * chunk:(c + 1) * chunk]
        s = jnp.dot(ws_ref[0], vch, preferred_element_type=F32)
        for hd in range(1, h_a):
            s = jnp.where(head_of_lane == hd,
                          jnp.dot(ws_ref[hd], vch, preferred_element_type=F32), s)
        s = s + bs_ref[...]
        ya_ref[0, c * chunk:(c + 1) * chunk, :] = (u[c * chunk:(c + 1) * chunk] * s).astype(BF16)

    o = 2 * d_a
    gg_ref[0] = jax.nn.gelu(z[:, o:o + d_b])
    xb_ref[0] = z[:, o + d_b:o + 2 * d_b]

    o = 2 * d_a + 2 * d_b
    cq = _rms(z[:, o:o + q_lora], qn_ref[...]).astype(BF16)
    qa = jnp.dot(cq, wuq_ref[...], preferred_element_type=F32)
    qb = jnp.dot(cq, wuqp_ref[...], preferred_element_type=F32)
    cos_q = jnp.concatenate([cq_ref[...]] * H_C, axis=1)
    sin_q = jnp.concatenate([sq_ref[...]] * H_C, axis=1)
    q_ref[0] = (qa * cos_q + qb * sin_q).astype(BF16)
    o += q_lora
    ckv = _rms(z[:, o:o + kv_lora], kvn_ref[...]).astype(BF16)
    o += kv_lora
    zr = z[:, o:o + d_rope]
    zrp = z[:, o + d_rope:o + 2 * d_rope]
    kr = (zr * ck_ref[...] + zrp * sk_ref[...]).astype(BF16)
    kn = jnp.dot(ckv, wuk_ref[...], preferred_element_type=F32)
    k_ref[0] = (kn + jnp.dot(kr, e_ref[...], preferred_element_type=F32)).astype(BF16)
    v_ref[0] = (jnp.dot(ckv, wuv_ref[...], preferred_element_type=F32) + vone_ref[...]).astype(BF16)


def _tile_mod_index(n_lat_tiles, n_batch):
    def index(b, i):
        return (jnp.where(i < n_lat_tiles, b, n_batch), 0, 0)
    return index


def _premix(xt, mods, g1, win, lng, lnb, ws, bs, qn, wuq, wuqp, kvn, wuk, emat, wuv, vone,
            cosq, sinq, cosk, sink, *, n_lat_tiles, dims):
    bsz, s, d = xt.shape
    d_a, d_b = dims[1], dims[2]
    hp = H_C * LANES
    const2 = lambda b, i: (0, 0)
    const3 = lambda b, i: (0, 0, 0)
    tile = lambda b, i: (b, i, 0)
    full = lambda a: pl.BlockSpec(a.shape, const2 if a.ndim == 2 else const3)
    return pl.pallas_call(
        functools.partial(_premix_kernel, dims=dims),
        grid=(bsz, s // TM),
        in_specs=[pl.BlockSpec((1, TM, d), tile),
                  pl.BlockSpec((1, 1, mods.shape[-1]), _tile_mod_index(n_lat_tiles, bsz)),
                  full(g1), full(win), full(lng), full(lnb), full(ws), full(bs),
                  full(qn), full(wuq), full(wuqp), full(kvn), full(wuk), full(emat), full(wuv), full(vone),
                  pl.BlockSpec((TM, LANES), lambda b, i: (i, 0)),
                  pl.BlockSpec((TM, LANES), lambda b, i: (i, 0)),
                  pl.BlockSpec((TM, cosk.shape[1]), lambda b, i: (i, 0)),
                  pl.BlockSpec((TM, sink.shape[1]), lambda b, i: (i, 0))],
        out_specs=[pl.BlockSpec((1, TM, d_a), tile), pl.BlockSpec((1, TM, d_b), tile),
                   pl.BlockSpec((1, TM, d_b), tile), pl.BlockSpec((1, TM, hp), tile),
                   pl.BlockSpec((1, TM, hp), tile), pl.BlockSpec((1, TM, hp), tile)],
        out_shape=[jax.ShapeDtypeStruct((bsz, s, d_a), BF16),
                   jax.ShapeDtypeStruct((bsz, s, d_b), F32),
                   jax.ShapeDtypeStruct((bsz, s, d_b), F32),
                   jax.ShapeDtypeStruct((bsz, s, hp), BF16),
                   jax.ShapeDtypeStruct((bsz, s, hp), BF16),
                   jax.ShapeDtypeStruct((bsz, s, hp), BF16)],
        compiler_params=_cparams(("arbitrary", "arbitrary")),
        name="premix",
    )(xt, mods, g1, win, lng, lnb, ws, bs, qn, wuq, wuqp, kvn, wuk, emat, wuv, vone,
      cosq, sinq, cosk, sink)


SUBLANES = 8


def _scan_tile(a, b, carry, reverse):
    t, c = a.shape
    g = SUBLANES
    n_groups = t // g
    a = a.reshape(n_groups, g, c)
    b = b.reshape(n_groups, g, c)
    rows = lax.broadcasted_iota(jnp.int32, a.shape, 1)
    s = 1
    while s < g:
        shift = g - s if reverse else s
        ok = rows < g - s if reverse else rows >= s
        a_sh = pltpu.roll(a, shift, 1)
        b_sh = pltpu.roll(b, shift, 1)
        b = b + a * jnp.where(ok, b_sh, 0.0)
        a = a * jnp.where(ok, a_sh, 1.0)
        s *= 2
    hs = [None] * n_groups
    for j in (range(n_groups - 1, -1, -1) if reverse else range(n_groups)):
        hj = b[j] + a[j] * carry
        hs[j] = hj
        carry = hj[0:1] if reverse else hj[g - 1:g]
    return jnp.concatenate(hs, axis=0), carry


def _lru_kernel(xb_ref, gg_ref, cw_ref, cb_ref, wg_ref, bg_ref, lam_ref, out_ref, xc_ref, hf_ref,
                *, n_lat, n_ctx):
    t = T_SCAN
    n_tiles = n_lat + n_ctx
    s_total = n_tiles * t
    d_b = xb_ref.shape[-1]
    w = cw_ref[...]
    cb = cb_ref[...]

    def conv_body(j, _):
        t0 = pl.multiple_of(j * t, t)
        is_ctx = j >= n_lat
        seq_lo = jnp.where(is_ctx, n_lat * t, 0)
        seq_hi = jnp.where(is_ctx, s_total, n_lat * t)
        cur = xb_ref[0, pl.ds(t0, t), :]
        p0 = pl.multiple_of(jnp.maximum(t0 - 8, 0), 8)
        n0 = pl.multiple_of(jnp.minimum(t0 + t, s_total - 8), 8)
        prev = jnp.where(t0 > seq_lo, xb_ref[0, pl.ds(p0, 8), :], 0.0)
        nxt = jnp.where(t0 + t < seq_hi, xb_ref[0, pl.ds(n0, 8), :], 0.0)
        ext = jnp.concatenate([prev, cur, nxt], axis=0)
        n_ext = t + 16
        xm2 = pltpu.roll(ext, 2, 0)[8:8 + t]
        xm1 = pltpu.roll(ext, 1, 0)[8:8 + t]
        xp1 = pltpu.roll(ext, n_ext - 1, 0)[8:8 + t]
        xc_ref[pl.ds(t0, t), :] = (w[0:1] * xm2 + w[1:2] * xm1 + w[2:3] * cur + w[3:4] * xp1 + cb)
        return 0

    lax.fori_loop(0, n_tiles, conv_body, 0)

    lam = lam_ref[...]
    neg = -lam
    softplus = jnp.maximum(neg, 0.0) + jnp.log1p(jnp.exp(-jnp.abs(neg)))

    def direction(dr, reverse):
        sp = softplus[dr:dr + 1]
        wg = wg_ref[:, dr * 2 * d_b:(dr + 1) * 2 * d_b]
        bg = bg_ref[:, dr * 2 * d_b:(dr + 1) * 2 * d_b]

        def body(j, carry):
            if reverse:
                idx = jnp.where(j < n_ctx, n_tiles - 1 - j, n_lat - 1 - (j - n_ctx))
            else:
                idx = jnp.where(j < n_ctx, n_lat + j, j - n_ctx)
            t0 = pl.multiple_of(idx * t, t)
            xc = xc_ref[pl.ds(t0, t), :]
            g = jnp.dot(xc.astype(BF16), wg, preferred_element_type=F32) + bg
            r = jax.nn.sigmoid(g[:, 0:d_b])
            ig = jax.nn.sigmoid(g[:, d_b:2 * d_b])
            log_a = (-LRU_C * r) * sp
            a = jnp.exp(log_a)
            bv = jnp.sqrt(-jnp.tanh(log_a) * (a * a + 1.0)) * (ig * xc)
            h, carry = _scan_tile(a, bv, carry, reverse)
            if reverse:
                y = gg_ref[0, pl.ds(t0, t), :] * (hf_ref[pl.ds(t0, t), :] + h)
                out_ref[0, pl.ds(t0, t), :] = y.astype(out_ref.dtype)
            else:
                hf_ref[pl.ds(t0, t), :] = h
            return carry

        lax.fori_loop(0, n_tiles, body, jnp.zeros((1, d_b), F32))

    direction(0, False)
    direction(1, True)


def _lru(xb, gg, cw, cb, wg, bg, lam, *, n_lat, n_ctx):
    bsz, s, d_b = xb.shape
    const2 = lambda b: (0, 0)
    full = lambda a: pl.BlockSpec(a.shape, const2)
    seq = pl.BlockSpec((1, s, d_b), lambda b: (b, 0, 0))
    return pl.pallas_call(
        functools.partial(_lru_kernel, n_lat=n_lat, n_ctx=n_ctx),
        grid=(bsz,),
        in_specs=[seq, seq, full(cw), full(cb), full(wg), full(bg), full(lam)],
        out_specs=seq,
        out_shape=jax.ShapeDtypeStruct((bsz, s, d_b), BF16),
        scratch_shapes=[pltpu.VMEM((s, d_b), F32), pltpu.VMEM((s, d_b), F32)],
        compiler_params=_cparams(("arbitrary",)),
        name="rglru",
    )(xb, gg, cw, cb, wg, bg, lam)


def _attn_kernel(q_ref, k_ref, v_ref, *rest, chunks, d_v):
    o_ref = rest[-1]
    tq = q_ref.shape[1]
    units = [(hh, lo, n) for hh in range(NH) for lo, n in chunks]

    def scores(hh, lo, n):
        qh = q_ref[0, :, hh * LANES:(hh + 1) * LANES]
        kh = k_ref[0, lo:lo + n, hh * LANES:(hh + 1) * LANES]
        return lax.dot_general(qh, kh, (((1,), (1,)), ((), ())), preferred_element_type=F32)

    outs = []
    m = acc = None
    s_next = scores(*units[0])
    for ui, (hh, lo, n) in enumerate(units):
        s = s_next
        if ui + 1 < len(units):
            s_next = scores(*units[ui + 1])
        if lo == chunks[0][0]:
            m = acc = None
        mc = jnp.max(s, axis=-1, keepdims=True)
        m_new = mc if m is None else jnp.maximum(m, mc)
        p = jnp.exp2(s - m_new).astype(BF16)
        pv = jnp.dot(p, v_ref[0, lo:lo + n, hh * LANES:(hh + 1) * LANES],
                     preferred_element_type=F32)
        acc = pv if m is None else jnp.exp2(m - m_new) * acc + pv
        m = m_new
        if lo == chunks[-1][0]:
            outs.append(acc / acc[:, d_v:d_v + 1])
    lane = lax.broadcasted_iota(jnp.int32, (tq, LANES), 1)
    for pp in range(NH // 2):
        pair = jnp.where(lane < d_v, outs[2 * pp], pltpu.roll(outs[2 * pp + 1], d_v, 1))
        o_ref[0, :, pp * LANES:(pp + 1) * LANES] = pair.astype(o_ref.dtype)


def _key_chunks(lo, hi, size):
    return tuple((a, min(size, hi - a)) for a in range(lo, hi, size))


def _attention(q, k, v, *, l_lat, d_v, ctx_queries):
    bsz, s, hp = q.shape
    l_ctx = s - l_lat
    d_c = H_C * d_v
    sem = ("arbitrary", "arbitrary", "arbitrary")
    out_shape = jax.ShapeDtypeStruct((bsz, s, d_c), BF16)
    y = pl.pallas_call(
        functools.partial(_attn_kernel, chunks=_key_chunks(0, s, KC), d_v=d_v),
        grid=(bsz, H_C // NH, l_lat // TQ),
        in_specs=[pl.BlockSpec((1, TQ, NH * LANES), lambda b, h, i: (b, i, h)),
                  pl.BlockSpec((1, s, NH * LANES), lambda b, h, i: (b, 0, h)),
                  pl.BlockSpec((1, s, NH * LANES), lambda b, h, i: (b, 0, h))],
        out_specs=pl.BlockSpec((1, TQ, NH * d_v), lambda b, h, i: (b, i, h)),
        out_shape=out_shape,
        compiler_params=_cparams(sem),
        name="attention",
    )(q, k, v)
    if not ctx_queries:
        return y
    first = l_lat // l_ctx
    return pl.pallas_call(
        functools.partial(_attn_kernel, chunks=_key_chunks(0, l_ctx, KC), d_v=d_v),
        grid=(bsz, H_C // NH, 1),
        in_specs=[pl.BlockSpec((1, l_ctx, NH * LANES), lambda b, h, i: (b, first, h)),
                  pl.BlockSpec((1, l_ctx, NH * LANES), lambda b, h, i: (b, first, h)),
                  pl.BlockSpec((1, l_ctx, NH * LANES), lambda b, h, i: (b, first, h)),
                  pl.BlockSpec(memory_space=pl.ANY)],
        out_specs=pl.BlockSpec((1, l_ctx, NH * d_v), lambda b, h, i: (b, first, h)),
        out_shape=out_shape,
        input_output_aliases={3: 0},
        compiler_params=_cparams(sem),
        name="attention_ctx",
    )(q, k, v, y)


def _postmix_kernel(x_ref, mod_ref, ya_ref, yb_ref, yc_ref, wout_ref, g2_ref, *rest,
                    d, n_experts, route):
    if route:
        wr_ref, x1_ref, h2_ref, re_ref, rg_ref = rest
    else:
        x1_ref, h2_ref = rest
    gate1 = mod_ref[0, :, 2 * d:3 * d]
    shift2 = mod_ref[0, :, 3 * d:4 * d]
    scale2 = mod_ref[0, :, 4 * d:5 * d]
    y = jnp.concatenate([ya_ref[0], yb_ref[0], yc_ref[0]], axis=1)
    x1 = x_ref[0] + gate1 * jnp.dot(y, wout_ref[...], preferred_element_type=F32)
    x1_ref[0] = x1
    h2 = _rms(x1, g2_ref[...]) * (1.0 + scale2) + shift2
    h2_ref[0] = h2.astype(h2_ref.dtype)
    if route:
        logits = jnp.dot(h2, wr_ref[...], preferred_element_type=F32)
        lane = lax.broadcasted_iota(jnp.int32, logits.shape, 1)
        neg_inf = jnp.float32(-jnp.inf)
        lg = jnp.where(lane < n_experts, logits, neg_inf)
        m1 = jnp.max(lg, axis=-1, keepdims=True)
        i1 = jnp.min(jnp.where(lg == m1, lane, LANES), axis=-1, keepdims=True)
        lg2 = jnp.where(lane == i1, neg_inf, lg)
        m2 = jnp.max(lg2, axis=-1, keepdims=True)
        i2 = jnp.min(jnp.where(lg2 == m2, lane, LANES), axis=-1, keepdims=True)
        e = jnp.exp(m2 - m1)
        den = 1.0 + e
        re_ref[0] = jnp.where(lane == 0, i1, i2)
        rg_ref[0] = jnp.where(lane == 0, 1.0 / den, e / den)


def _postmix(xt, mods, ya, yb, yc, wout, g2, wr, *, n_lat_tiles, n_tiles, n_experts):
    bsz, s, d = xt.shape
    route = wr is not None
    tile = lambda b, i: (b, i, 0)
    const2 = lambda b, i: (0, 0)
    full = lambda a: pl.BlockSpec(a.shape, const2)
    in_specs = [pl.BlockSpec((1, TM, d), tile),
                pl.BlockSpec((1, 1, mods.shape[-1]), _tile_mod_index(n_lat_tiles, bsz)),
                pl.BlockSpec((1, TM, ya.shape[-1]), tile), pl.BlockSpec((1, TM, yb.shape[-1]), tile),
                pl.BlockSpec((1, TM, yc.shape[-1]), tile), full(wout), full(g2)]
    args = [xt, mods, ya, yb, yc, wout, g2]
    out_specs = [pl.BlockSpec((1, TM, d), tile), pl.BlockSpec((1, TM, d), tile)]
    out_shape = [jax.ShapeDtypeStruct((bsz, s, d), F32),
                 jax.ShapeDtypeStruct((bsz, s, d), F32 if route else BF16)]
    if route:
        in_specs.append(full(wr))
        args.append(wr)
        out_specs += [pl.BlockSpec((1, TM, LANES), tile), pl.BlockSpec((1, TM, LANES), tile)]
        out_shape += [jax.ShapeDtypeStruct((bsz, s, LANES), jnp.int32),
                      jax.ShapeDtypeStruct((bsz, s, LANES), F32)]
    return pl.pallas_call(
        functools.partial(_postmix_kernel, d=d, n_experts=n_experts, route=route),
        grid=(bsz, n_tiles),
        in_specs=in_specs, out_specs=out_specs, out_shape=out_shape,
        compiler_params=_cparams(("arbitrary", "arbitrary")),
        name="postmix_route" if route else "postmix",
    )(*args)


def _ffn_kernel(x1_ref, h2_ref, mod_ref, w1_ref, w3_ref, w2_ref, o_ref, *, d, fc):
    h = h2_ref[0]
    d_ff = w1_ref.shape[1]
    n_chunks = d_ff // fc

    def up(c):
        return (jnp.dot(h, w1_ref[:, c * fc:(c + 1) * fc], preferred_element_type=F32),
                jnp.dot(h, w3_ref[:, c * fc:(c + 1) * fc], preferred_element_type=F32))

    acc = jnp.zeros((h.shape[0], d), F32)
    a, b = up(0)
    for c in range(n_chunks):
        ac, bc = a, b
        if c + 1 < n_chunks:
            a, b = up(c + 1)
        act = (ac * jax.nn.sigmoid(ac) * bc).astype(BF16)
        acc = acc + jnp.dot(act, w2_ref[c * fc:(c + 1) * fc, :], preferred_element_type=F32)
    gate2 = mod_ref[0, :, 5 * d:6 * d]
    o_ref[0] = x1_ref[0] + gate2 * acc


def _ffn(x1, h2, mods, w1, w3, w2, *, n_lat_tiles, n_tiles):
    bsz, s, d = x1.shape
    tile = lambda b, i: (b, i, 0)
    const2 = lambda b, i: (0, 0)
    full = lambda a: pl.BlockSpec(a.shape, const2)
    return pl.pallas_call(
        functools.partial(_ffn_kernel, d=d, fc=256),
        grid=(bsz, n_tiles),
        in_specs=[pl.BlockSpec((1, TM, d), tile), pl.BlockSpec((1, TM, d), tile),
                  pl.BlockSpec((1, 1, mods.shape[-1]), _tile_mod_index(n_lat_tiles, bsz)),
                  full(w1), full(w3), full(w2)],
        out_specs=pl.BlockSpec((1, TM, d), tile),
        out_shape=jax.ShapeDtypeStruct((bsz, s, d), F32),
        compiler_params=_cparams(("arbitrary", "arbitrary")),
        name="ffn",
    )(x1, h2, mods, w1, w3, w2)


DMA_UNROLL = 8


def _row_copy(src_ref, dst_ref, sem, src_row, dst_row):
    return pltpu.make_async_copy(src_ref.at[pl.ds(src_row, 1)], dst_ref.at[pl.ds(dst_row, 1)], sem)


def _dispatch_kernel(pos_ref, h_ref, xs_in_hbm, xs_hbm, sem):
    del xs_in_hbm
    n = h_ref.shape[1]

    def start(j, _):
        for kk in range(TOP_K):
            _row_copy(h_ref.at[0], xs_hbm, sem, j, pos_ref[0, 0, kk * n + j]).start()
        return 0

    lax.fori_loop(0, n, start, 0, unroll=DMA_UNROLL)
    for kk in range(TOP_K):
        pltpu.make_async_copy(h_ref.at[0], xs_hbm.at[pl.ds(0, n)], sem).wait()


def _pad_fill_kernel(start_ref, len_ref, xs_hbm, zero_ref, sem):
    zero_ref[...] = jnp.zeros_like(zero_ref)
    n_experts = start_ref.shape[0]
    top = zero_ref.shape[0]

    def pieces(e, fn):
        off = start_ref[e]
        size = top
        while size >= SUBLANES:
            @pl.when((len_ref[e] & size) != 0)
            def _(off=off, size=size):
                dst = xs_hbm.at[pl.ds(pl.multiple_of(off, SUBLANES), size)]
                fn(pltpu.make_async_copy(zero_ref.at[pl.ds(0, size)], dst, sem))
            off = off + (len_ref[e] & size)
            size //= 2

    for e in range(n_experts):
        pieces(e, lambda cp: cp.start())
    for e in range(n_experts):
        pieces(e, lambda cp: cp.wait())


def _pad_fill(fill_start, fill_len, n_pad, d, dtype):
    return pl.pallas_call(
        _pad_fill_kernel,
        grid_spec=pltpu.PrefetchScalarGridSpec(
            num_scalar_prefetch=2, grid=(1,), in_specs=[],
            out_specs=pl.BlockSpec(memory_space=pl.ANY),
            scratch_shapes=[pltpu.VMEM((TMB, d), dtype), pltpu.SemaphoreType.DMA(())]),
        out_shape=jax.ShapeDtypeStruct((n_pad, d), dtype),
        compiler_params=_cparams(("arbitrary",)),
        name="moe_pad_fill",
    )(fill_start, fill_len)


def _dispatch(pos, h2, xs0, *, n_tiles):
    bsz, s, d = h2.shape
    n_pad = xs0.shape[0]
    return pl.pallas_call(
        _dispatch_kernel,
        grid=(bsz, n_tiles),
        in_specs=[pl.BlockSpec((1, 1, TOP_K * TM), lambda b, i: (b * n_tiles + i, 0, 0),
                               memory_space=pltpu.SMEM),
                  pl.BlockSpec((1, TM, d), lambda b, i: (b, i, 0)),
                  pl.BlockSpec(memory_space=pl.ANY)],
        out_specs=pl.BlockSpec(memory_space=pl.ANY),
        out_shape=jax.ShapeDtypeStruct((n_pad, d), h2.dtype),
        scratch_shapes=[pltpu.SemaphoreType.DMA(())],
        input_output_aliases={2: 0},
        compiler_params=_cparams(("arbitrary", "arbitrary")),
        name="moe_dispatch",
    )(pos, h2, xs0)


def _experts_kernel(be_ref, nu_ref, xs_ref, w1_ref, w3_ref, w2_ref, o_ref, xb_ref, acc_ref):
    r = pl.program_id(0)
    f = pl.program_id(1)
    nf = pl.num_programs(1)

    @pl.when(r < nu_ref[0])
    def _():
        @pl.when(f == 0)
        def _():
            xb_ref[...] = xs_ref[...].astype(BF16)
            acc_ref[...] = jnp.zeros_like(acc_ref)

        xb = xb_ref[...]
        a = jnp.dot(xb, w1_ref[0, 0].astype(BF16), preferred_element_type=F32)
        b = jnp.dot(xb, w3_ref[0, 0].astype(BF16), preferred_element_type=F32)
        act = (a * jax.nn.sigmoid(a) * b).astype(BF16)
        acc_ref[...] += jnp.dot(act, w2_ref[0, 0].astype(BF16), preferred_element_type=F32)

        @pl.when(f == nf - 1)
        def _():
            o_ref[...] = acc_ref[...]

    @pl.when(jnp.logical_and(r >= nu_ref[0], f == nf - 1))
    def _():
        o_ref[...] = jnp.zeros_like(o_ref)


def _experts(blk_e, n_used, xs, w1, w3, w2, layer):
    n_pad, d = xs.shape
    n_blk = n_pad // TMB
    d_ff = w1.shape[-1]
    nf = d_ff // TF

    def w_col(r, f, be, nu):
        live = r < nu[0]
        return (layer, be[r], 0, jnp.where(live, f, nf - 1))

    def w_row(r, f, be, nu):
        live = r < nu[0]
        return (layer, be[r], jnp.where(live, f, nf - 1), 0)

    grid_spec = pltpu.PrefetchScalarGridSpec(
        num_scalar_prefetch=2,
        grid=(n_blk, nf),
        in_specs=[pl.BlockSpec((TMB, d), lambda r, f, be, nu: (jnp.where(r < nu[0], r, 0), 0)),
                  pl.BlockSpec((1, 1, d, TF), w_col),
                  pl.BlockSpec((1, 1, d, TF), w_col),
                  pl.BlockSpec((1, 1, TF, d), w_row)],
        out_specs=pl.BlockSpec((TMB, d), lambda r, f, be, nu: (r, 0)),
        scratch_shapes=[pltpu.VMEM((TMB, d), BF16), pltpu.VMEM((TMB, d), F32)])
    return pl.pallas_call(
        _experts_kernel,
        grid_spec=grid_spec,
        out_shape=jax.ShapeDtypeStruct((n_pad, d), F32),
        compiler_params=_cparams(("arbitrary", "arbitrary")),
        name="moe_experts",
    )(blk_e, n_used, xs, w1, w3, w2)


def _combine_kernel(pos_ref, x1_ref, mod_ref, rg_ref, yp_hbm, *rest, d, final):
    if final:
        gf_ref, o_ref, buf_ref, sem = rest
    else:
        o_ref, buf_ref, sem = rest
    n = x1_ref.shape[1]

    def start(j, _):
        for kk in range(TOP_K):
            _row_copy(yp_hbm, buf_ref.at[kk], sem, pos_ref[0, 0, kk * n + j], j).start()
        return 0

    lax.fori_loop(0, n, start, 0, unroll=DMA_UNROLL)
    for kk in range(TOP_K):
        pltpu.make_async_copy(yp_hbm.at[pl.ds(0, n)], buf_ref.at[kk], sem).wait()
    gate2 = mod_ref[0, :, 5 * d:6 * d]
    rg = rg_ref[0]
    y = rg[:, 0:1] * buf_ref[0]
    for kk in range(1, TOP_K):
        y = y + rg[:, kk:kk + 1] * buf_ref[kk]
    x2 = x1_ref[0] + gate2 * y
    if final:
        x2 = _rms(x2, gf_ref[...])
    o_ref[0] = x2


def _combine(pos, x1, mods, route_g, yp, gf, *, n_lat_tiles, n_tiles):
    bsz, s, d = x1.shape
    final = gf is not None
    tile = lambda b, i: (b, i, 0)
    in_specs = [pl.BlockSpec((1, 1, TOP_K * TM), lambda b, i: (b * n_tiles + i, 0, 0),
                             memory_space=pltpu.SMEM),
                pl.BlockSpec((1, TM, d), tile),
                pl.BlockSpec((1, 1, mods.shape[-1]), _tile_mod_index(n_lat_tiles, bsz)),
                pl.BlockSpec((1, TM, LANES), tile),
                pl.BlockSpec(memory_space=pl.ANY)]
    args = [pos, x1, mods, route_g, yp]
    if final:
        in_specs.append(pl.BlockSpec(gf.shape, lambda b, i: (0, 0)))
        args.append(gf)
    return pl.pallas_call(
        functools.partial(_combine_kernel, d=d, final=final),
        grid=(bsz, n_tiles),
        in_specs=in_specs,
        out_specs=pl.BlockSpec((1, TM, d), tile),
        out_shape=jax.ShapeDtypeStruct((bsz, n_tiles * TM, d), F32),
        scratch_shapes=[pltpu.VMEM((TOP_K, TM, d), F32), pltpu.SemaphoreType.DMA(())],
        compiler_params=_cparams(("arbitrary", "arbitrary")),
        name="moe_combine_final" if final else "moe_combine",
    )(*args)


def _route_tables(route_e, n_experts, s_eff):
    bsz = route_e.shape[0]
    s = s_eff
    n_assign = bsz * s * TOP_K
    lane = jnp.arange(route_e.shape[-1], dtype=jnp.int32)
    flat_e = jnp.stack([jnp.max(jnp.where(lane == k, route_e[:, :s], -1), axis=-1)
                        for k in range(TOP_K)]).reshape(n_assign)
    onehot = (flat_e[:, None] == jnp.arange(n_experts, dtype=jnp.int32)[None, :]).astype(jnp.int32)
    csum = jnp.cumsum(onehot, axis=0)
    rank = jnp.sum(csum * onehot, axis=1) - 1
    counts = csum[-1]
    padded = (counts + TMB - 1) // TMB * TMB
    pad_ends = jnp.cumsum(padded)
    pad_starts = pad_ends - padded
    dest = pad_starts[flat_e] + rank
    n_blk = (n_assign + TMB - 1) // TMB + n_experts
    n_pad = n_blk * TMB
    blk_e = jnp.minimum(
        jnp.searchsorted(pad_ends, jnp.arange(n_blk, dtype=jnp.int32) * TMB, side='right'),
        n_experts - 1).astype(jnp.int32)
    n_used = (pad_ends[-1] // TMB).astype(jnp.int32).reshape(1)
    fill_start = ((pad_starts + counts) // SUBLANES * SUBLANES).astype(jnp.int32)
    fill_len = (pad_ends - fill_start).astype(jnp.int32)
    n_tok_tiles = bsz * s // TM
    pos = dest.astype(jnp.int32).reshape(TOP_K, n_tok_tiles, TM).transpose(1, 0, 2)
    return blk_e, n_used, pos.reshape(n_tok_tiles, 1, TOP_K * TM), n_pad, fill_start, fill_len


def _final_kernel(x_ref, g_ref, o_ref):
    o_ref[0] = _rms(x_ref[0], g_ref[...])


def _final_norm(xt, g, *, n_tiles):
    bsz, s, d = xt.shape
    tile = lambda b, i: (b, i, 0)
    return pl.pallas_call(
        _final_kernel,
        grid=(bsz, n_tiles),
        in_specs=[pl.BlockSpec((1, TM, d), tile), pl.BlockSpec(g.shape, lambda b, i: (0, 0))],
        out_specs=pl.BlockSpec((1, TM, d), tile),
        out_shape=jax.ShapeDtypeStruct((bsz, n_tiles * TM, d), F32),
        compiler_params=_cparams(("arbitrary", "arbitrary")),
        name="final_norm",
    )(xt, g)


def _rope_partner(r):
    return jnp.concatenate([-r[..., 8:16], r[..., 0:8], -r[..., 24:32], r[..., 16:24]], axis=-1)


def _rope_tables(l_lat, s_total, d_rope, d_nope):
    t = jnp.arange(l_lat, dtype=jnp.int32)
    row = (t // GRID_W).astype(F32)
    col = (t % GRID_W).astype(F32)
    half = d_rope // 2
    inv_freq = ROPE_BASE ** (-jnp.arange(0, half, 2, dtype=F32) / half)
    ang_r = row[:, None] * inv_freq
    ang_c = col[:, None] * inv_freq
    cos = jnp.concatenate([jnp.cos(ang_r), jnp.cos(ang_r), jnp.cos(ang_c), jnp.cos(ang_c)], axis=1)
    sin = jnp.concatenate([jnp.sin(ang_r), jnp.sin(ang_r), jnp.sin(ang_c), jnp.sin(ang_c)], axis=1)
    n_ctx = s_total - l_lat
    cosk = jnp.concatenate([cos, jnp.ones((n_ctx, d_rope), F32)], axis=0)
    sink = jnp.concatenate([sin, jnp.zeros((n_ctx, d_rope), F32)], axis=0)
    pad = LANES - d_nope - d_rope
    cosq = jnp.concatenate([jnp.ones((s_total, d_nope), F32), cosk, jnp.zeros((s_total, pad), F32)], axis=1)
    sinq = jnp.concatenate([jnp.zeros((s_total, d_nope), F32), sink, jnp.zeros((s_total, pad), F32)], axis=1)
    return cosq, sinq, cosk, sink


def _block_diag(w):
    h, a, b = w.shape
    eye = jnp.eye(h, dtype=w.dtype)
    return (eye[:, None, :, None] * w[:, :, None, :]).reshape(h * a, h * b)


def kernel(x, c, ctx, c_ctx, w_mod, b_mod, norm1_g, norm2_g, w_in, w_out, sgu_ln_g, sgu_ln_b, sgu_w, sgu_b, conv_w, conv_b, lru_w_a, lru_b_a, lru_w_x, lru_b_x, lru_lam, mla_q_norm, mla_w_uq, mla_kv_norm, mla_w_uk, mla_w_uv, ffn_w1, ffn_w3, ffn_w2, moe_router, moe_w1, moe_w3, moe_w2, final_norm_g):
    bsz, l_lat, d = x.shape
    l_ctx = ctx.shape[1]
    s_total = l_lat + l_ctx
    depth = w_mod.shape[0]
    d_a = sgu_ln_g.shape[-1]
    h_a, chunk = sgu_w.shape[1], sgu_w.shape[2]
    d_b = conv_w.shape[-1]
    q_lora = mla_q_norm.shape[-1]
    kv_lora = mla_kv_norm.shape[-1]
    d_c = mla_w_uv.shape[-1]
    d_v = d_c // H_C
    d_nope = mla_w_uk.shape[-1] // H_C
    d_qk = mla_w_uq.shape[-1] // H_C
    d_rope = d_qk - d_nope
    n_experts = moe_router.shape[-1]
    assert l_lat % TM == 0 and l_ctx % TM == 0 and TM % chunk == 0 and TM == T_SCAN
    assert l_lat % TQ == 0 and l_lat % l_ctx == 0
    assert d_qk <= LANES and 2 * d_v == LANES and H_C % 2 == 0 and d_rope == 32
    n_lat_tiles = l_lat // TM
    n_ctx_tiles = l_ctx // TM
    dims = (d, d_a, d_b, q_lora, kv_lora, d_rope, h_a, chunk)

    xt = jnp.concatenate([x, ctx], axis=1)

    n_rows = (bsz + 1 + 7) // 8 * 8
    cond = jnp.zeros((n_rows, d), F32).at[:bsz].set(c).at[bsz].set(c_ctx)
    mods_all = _modulation(cond, w_mod.astype(BF16), b_mod[:, None, :])
    mods_all = mods_all[:, :bsz + 1, None, :]

    cosq, sinq, cosk, sink = _rope_tables(l_lat, s_total, d_rope, d_nope)
    q_scale = float(d_qk) ** -0.5 * 1.4426950408889634
    cosq, sinq = cosq * q_scale, sinq * q_scale
    head_pad = LANES - d_qk
    vone = jnp.tile((jnp.arange(LANES) == d_v).astype(F32)[None], (1, H_C))
    e_head = jnp.concatenate([jnp.zeros((d_rope, d_nope), F32), jnp.eye(d_rope, dtype=F32),
                              jnp.zeros((d_rope, head_pad), F32)], axis=1)
    emat = jnp.tile(e_head, (1, H_C)).astype(BF16)

    out = None
    for l in range(depth):
        last = l == depth - 1
        mods = mods_all[l]
        o_m = 2 * d_a + 2 * d_b
        o_r = o_m + q_lora + kv_lora
        w_rope = w_in[l][:, o_r:o_r + d_rope]
        n_in = (o_r + 2 * d_rope + LANES - 1) // LANES * LANES
        win = jnp.concatenate([w_in[l][:, :o_r + d_rope], _rope_partner(w_rope),
                               jnp.zeros((d, n_in - o_r - 2 * d_rope), F32)], axis=1).astype(BF16)
        wq = mla_w_uq[l].reshape(q_lora, H_C, d_qk)
        zq = jnp.zeros((q_lora, H_C, head_pad), F32)
        wuq = jnp.concatenate([wq, zq], axis=-1).reshape(q_lora, H_C * LANES).astype(BF16)
        wuqp = jnp.concatenate([jnp.zeros((q_lora, H_C, d_nope), F32), _rope_partner(wq[..., d_nope:]), zq],
                               axis=-1).reshape(q_lora, H_C * LANES).astype(BF16)
        wk = mla_w_uk[l].reshape(kv_lora, H_C, d_nope)
        wuk = jnp.concatenate([wk, jnp.zeros((kv_lora, H_C, LANES - d_nope), F32)],
                              axis=-1).reshape(kv_lora, H_C * LANES).astype(BF16)
        wv = mla_w_uv[l].reshape(kv_lora, H_C, d_v)
        wuv = jnp.concatenate([wv, jnp.zeros((kv_lora, H_C, LANES - d_v), F32)],
                              axis=-1).reshape(kv_lora, H_C * LANES).astype(BF16)
        bs_full = jnp.repeat(sgu_b[l].T, d_a // h_a, axis=1)

        ya, gg, xb, q, k, v = _premix(
            xt, mods, norm1_g[l][None], win, sgu_ln_g[l][None], sgu_ln_b[l][None],
            sgu_w[l].astype(BF16), bs_full, mla_q_norm[l][None], wuq, wuqp, mla_kv_norm[l][None],
            wuk, emat, wuv, vone, cosq, sinq, cosk, sink,
            n_lat_tiles=n_lat_tiles, dims=dims)

        wg = jnp.concatenate([_block_diag(lru_w_a[l, 0]), _block_diag(lru_w_x[l, 0]),
                              _block_diag(lru_w_a[l, 1]), _block_diag(lru_w_x[l, 1])], axis=1).astype(BF16)
        bg = jnp.concatenate([lru_b_a[l, 0], lru_b_x[l, 0], lru_b_a[l, 1], lru_b_x[l, 1]])[None]
        yb = _lru(xb, gg, conv_w[l], conv_b[l][None], wg, bg, lru_lam[l], n_lat=n_lat_tiles, n_ctx=n_ctx_tiles)

        n_tiles = n_lat_tiles if last else n_lat_tiles + n_ctx_tiles
        yc = _attention(q, k, v, l_lat=l_lat, d_v=d_v, ctx_queries=not last)

        if l % 2 == 0:
            i = l // 2
            x1, h2 = _postmix(xt, mods, ya, yb, yc, w_out[l].astype(BF16), norm2_g[l][None], None,
                              n_lat_tiles=n_lat_tiles, n_tiles=n_tiles, n_experts=n_experts)
            xt = _ffn(x1, h2, mods, ffn_w1[i].astype(BF16), ffn_w3[i].astype(BF16), ffn_w2[i].astype(BF16),
                      n_lat_tiles=n_lat_tiles, n_tiles=n_tiles)
            if last:
                out = _final_norm(xt, final_norm_g[None], n_tiles=n_lat_tiles)
        else:
            i = l // 2
            wr = jnp.concatenate([moe_router[i], jnp.zeros((d, LANES - n_experts), F32)], axis=1)
            x1, h2, route_e, route_g = _postmix(xt, mods, ya, yb, yc, w_out[l].astype(BF16),
                                                norm2_g[l][None], wr,
                                                n_lat_tiles=n_lat_tiles, n_tiles=n_tiles, n_experts=n_experts)
            blk_e, n_used, pos, n_pad, fill_start, fill_len = _route_tables(route_e, n_experts, n_tiles * TM)
            xs = _dispatch(pos, h2, _pad_fill(fill_start, fill_len, n_pad, d, h2.dtype), n_tiles=n_tiles)
            yp = _experts(blk_e, n_used, xs, moe_w1, moe_w3, moe_w2, i)
            out_or_xt = _combine(pos, x1, mods, route_g, yp, final_norm_g[None] if last else None,
                                 n_lat_tiles=n_lat_tiles, n_tiles=n_tiles)
            if last:
                out = out_or_xt
            else:
                xt = out_or_xt
    return out
```

```python
import functools

import jax
import jax.numpy as jnp
from jax import lax
from jax.experimental import pallas as pl
from jax.experimental.pallas import tpu as pltpu

F32 = jnp.float32
BF16 = jnp.bfloat16

EPS = 1e-6
GRID_W = 64
ROPE_BASE = 10000.0
LRU_C = 8.0
H_C = 8
TOP_K = 2
LANES = 128
TM = 256
T_SCAN = 256
KC = 2048
NH = 4
TQ = 512
TMB = 1024
TF = 512
VMEM_LIMIT = 56 * 1024 * 1024


def _cparams(sem):
    return pltpu.CompilerParams(dimension_semantics=sem, vmem_limit_bytes=VMEM_LIMIT)


def _rms(x, g):
    return x * lax.rsqrt(jnp.mean(x * x, axis=-1, keepdims=True) + EPS) * g


def _mod_kernel(c_ref, w_ref, b_ref, o_ref):
    c = c_ref[...]
    a = (c * jax.nn.sigmoid(c)).astype(BF16)
    o_ref[0] = jnp.dot(a, w_ref[0], preferred_element_type=F32) + b_ref[0]


def _modulation(cond, w_mod, b_mod):
    depth, d, n = w_mod.shape
    r = cond.shape[0]
    tn = 1024
    return pl.pallas_call(
        _mod_kernel,
        grid=(depth, n // tn),
        in_specs=[pl.BlockSpec((r, d), lambda l, j: (0, 0)),
                  pl.BlockSpec((1, d, tn), lambda l, j: (l, 0, j)),
                  pl.BlockSpec((1, 1, tn), lambda l, j: (l, 0, j))],
        out_specs=pl.BlockSpec((1, r, tn), lambda l, j: (l, 0, j)),
        out_shape=jax.ShapeDtypeStruct((depth, r, n), F32),
        compiler_params=_cparams(("arbitrary", "arbitrary")),
        name="modulation",
    )(cond, w_mod, b_mod)


def _premix_kernel(x_ref, mod_ref, g_ref, win_ref, lng_ref, lnb_ref, ws_ref, bs_ref,
                   qn_ref, wuq_ref, wuqp_ref, kvn_ref, wuk_ref, e_ref, wuv_ref, vone_ref,
                   cq_ref, sq_ref, ck_ref, sk_ref,
                   ya_ref, gg_ref, xb_ref, q_ref, k_ref, v_ref, *, dims):
    d, d_a, d_b, q_lora, kv_lora, d_rope, h_a, chunk = dims
    x = x_ref[0]
    shift = mod_ref[0, :, 0:d]
    scale = mod_ref[0, :, d:2 * d]
    h = (_rms(x, g_ref[...]) * (1.0 + scale) + shift).astype(BF16)
    z = jnp.dot(h, win_ref[...], preferred_element_type=F32)

    o = 0
    u = jax.nn.gelu(z[:, o:o + d_a])
    v = jax.nn.gelu(z[:, o + d_a:o + 2 * d_a])
    mu = jnp.mean(v, axis=-1, keepdims=True)
    vc = v - mu
    var = jnp.mean(vc * vc, axis=-1, keepdims=True)
    vn = (vc * lax.rsqrt(var + EPS) * lng_ref[...] + lnb_ref[...]).astype(BF16)
    dh_a = d_a // h_a
    tm = x.shape[0]
    head_of_lane = lax.broadcasted_iota(jnp.int32, (chunk, d_a), 1) // dh_a
    for c in range(tm // chunk):
        vch = vn[c * chunk:(c + 1) * chunk]
        s = jnp.dot(ws_ref[0], vch, preferred_element_type=F32)
        for hd in range(1, h_a):
            s = jnp.where(head_of_lane == hd,
                          jnp.dot(ws_ref[hd], vch, preferred_element_type=F32), s)
        s = s + bs_ref[...]
        ya_ref[0, c * chunk:(c + 1) * chunk, :] = (u[c * chunk:(c + 1) * chunk] * s).astype(BF16)

    o = 2 * d_a
    gg_ref[0] = jax.nn.gelu(z[:, o:o + d_b])
    xb_ref[0] = z[:, o + d_b:o + 2 * d_b]

    o = 2 * d_a + 2 * d_b
    cq = _rms(z[:, o:o + q_lora], qn_ref[...]).astype(BF16)
    qa = jnp.dot(cq, wuq_ref[...], preferred_element_type=F32)
    qb = jnp.dot(cq, wuqp_ref[...], preferred_element_type=F32)
    cos_q = jnp.concatenate([cq_ref[...]] * H_C, axis=1)
    sin_q = jnp.concatenate([sq_ref[...]] * H_C, axis=1)
    q_ref[0] = (qa * cos_q + qb * sin_q).astype(BF16)
    o += q_lora
    ckv = _rms(z[:, o:o + kv_lora], kvn_ref[...]).astype(BF16)
    o += kv_lora
    zr = z[:, o:o + d_rope]
    zrp = z[:, o + d_rope:o + 2 * d_rope]
    kr = (zr * ck_ref[...] + zrp * sk_ref[...]).astype(BF16)
    kn = jnp.dot(ckv, wuk_ref[...], preferred_element_type=F32)
    k_ref[0] = (kn + jnp.dot(kr, e_ref[...], preferred_element_type=F32)).astype(BF16)
    v_ref[0] = (jnp.dot(ckv, wuv_ref[...], preferred_element_type=F32) + vone_ref[...]).astype(BF16)


def _tile_mod_index(n_lat_tiles, n_batch):
    def index(b, i):
        return (jnp.where(i < n_lat_tiles, b, n_batch), 0, 0)
    return index


def _premix(xt, mods, g1, win, lng, lnb, ws, bs, qn, wuq, wuqp, kvn, wuk, emat, wuv, vone,
            cosq, sinq, cosk, sink, *, n_lat_tiles, dims):
    bsz, s, d = xt.shape
    d_a, d_b = dims[1], dims[2]
    hp = H_C * LANES
    const2 = lambda b, i: (0, 0)
    const3 = lambda b, i: (0, 0, 0)
    tile = lambda b, i: (b, i, 0)
    full = lambda a: pl.BlockSpec(a.shape, const2 if a.ndim == 2 else const3)
    return pl.pallas_call(
        functools.partial(_premix_kernel, dims=dims),
        grid=(bsz, s // TM),
        in_specs=[pl.BlockSpec((1, TM, d), tile),
                  pl.BlockSpec((1, 1, mods.shape[-1]), _tile_mod_index(n_lat_tiles, bsz)),
                  full(g1), full(win), full(lng), full(lnb), full(ws), full(bs),
                  full(qn), full(wuq), full(wuqp), full(kvn), full(wuk), full(emat), full(wuv), full(vone),
                  pl.BlockSpec((TM, LANES), lambda b, i: (i, 0)),
                  pl.BlockSpec((TM, LANES), lambda b, i: (i, 0)),
                  pl.BlockSpec((TM, cosk.shape[1]), lambda b, i: (i, 0)),
                  pl.BlockSpec((TM, sink.shape[1]), lambda b, i: (i, 0))],
        out_specs=[pl.BlockSpec((1, TM, d_a), tile), pl.BlockSpec((1, TM, d_b), tile),
                   pl.BlockSpec((1, TM, d_b), tile), pl.BlockSpec((1, TM, hp), tile),
                   pl.BlockSpec((1, TM, hp), tile), pl.BlockSpec((1, TM, hp), tile)],
        out_shape=[jax.ShapeDtypeStruct((bsz, s, d_a), BF16),
                   jax.ShapeDtypeStruct((bsz, s, d_b), F32),
                   jax.ShapeDtypeStruct((bsz, s, d_b), F32),
                   jax.ShapeDtypeStruct((bsz, s, hp), BF16),
                   jax.ShapeDtypeStruct((bsz, s, hp), BF16),
                   jax.ShapeDtypeStruct((bsz, s, hp), BF16)],
        compiler_params=_cparams(("arbitrary", "arbitrary")),
        name="premix",
    )(xt, mods, g1, win, lng, lnb, ws, bs, qn, wuq, wuqp, kvn, wuk, emat, wuv, vone,
      cosq, sinq, cosk, sink)


SUBLANES = 8


def _scan_tile(a, b, carry, reverse):
    t, c = a.shape
    g = SUBLANES
    n_groups = t // g
    a = a.reshape(n_groups, g, c)
    b = b.reshape(n_groups, g, c)
    rows = lax.broadcasted_iota(jnp.int32, a.shape, 1)
    s = 1
    while s < g:
        shift = g - s if reverse else s
        ok = rows < g - s if reverse else rows >= s
        a_sh = pltpu.roll(a, shift, 1)
        b_sh = pltpu.roll(b, shift, 1)
        b = b + a * jnp.where(ok, b_sh, 0.0)
        a = a * jnp.where(ok, a_sh, 1.0)
        s *= 2
    hs = [None] * n_groups
    for j in (range(n_groups - 1, -1, -1) if reverse else range(n_groups)):
        hj = b[j] + a[j] * carry
        hs[j] = hj
        carry = hj[0:1] if reverse else hj[g - 1:g]
    return jnp.concatenate(hs, axis=0), carry


def _lru_kernel(xb_ref, gg_ref, cw_ref, cb_ref, wg_ref, bg_ref, lam_ref, out_ref, xc_ref, hf_ref,
                *, n_lat, n_ctx):
    t = T_SCAN
    n_tiles = n_lat + n_ctx
    s_total = n_tiles * t
    d_b = xb_ref.shape[-1]
    w = cw_ref[...]
    cb = cb_ref[...]

    def conv_body(j, _):
        t0 = pl.multiple_of(j * t, t)
        is_ctx = j >= n_lat
        seq_lo = jnp.where(is_ctx, n_lat * t, 0)
        seq_hi = jnp.where(is_ctx, s_total, n_lat * t)
        cur = xb_ref[0, pl.ds(t0, t), :]
        p0 = pl.multiple_of(jnp.maximum(t0 - 8, 0), 8)
        n0 = pl.multiple_of(jnp.minimum(t0 + t, s_total - 8), 8)
        prev = jnp.where(t0 > seq_lo, xb_ref[0, pl.ds(p0, 8), :], 0.0)
        nxt = jnp.where(t0 + t < seq_hi, xb_ref[0, pl.ds(n0, 8), :], 0.0)
        ext = jnp.concatenate([prev, cur, nxt], axis=0)
        n_ext = t + 16
        xm2 = pltpu.roll(ext, 2, 0)[8:8 + t]
        xm1 = pltpu.roll(ext, 1, 0)[8:8 + t]
        xp1 = pltpu.roll(ext, n_ext - 1, 0)[8:8 + t]
        xc_ref[pl.ds(t0, t), :] = (w[0:1] * xm2 + w[1:2] * xm1 + w[2:3] * cur + w[3:4] * xp1 + cb)
        return 0

    lax.fori_loop(0, n_tiles, conv_body, 0)

    lam = lam_ref[...]
    neg = -lam
    softplus = jnp.maximum(neg, 0.0) + jnp.log1p(jnp.exp(-jnp.abs(neg)))

    def direction(dr, reverse):
        sp = softplus[dr:dr + 1]
        wg = wg_ref[:, dr * 2 * d_b:(dr + 1) * 2 * d_b]
        bg = bg_ref[:, dr * 2 * d_b:(dr + 1) * 2 * d_b]

        def body(j, carry):
            if reverse:
                idx = jnp.where(j < n_ctx, n_tiles - 1 - j, n_lat - 1 - (j - n_ctx))
            else:
                idx = jnp.where(j < n_ctx, n_lat + j, j - n_ctx)
            t0 = pl.multiple_of(idx * t, t)
            xc = xc_ref[pl.ds(t0, t), :]
            g = jnp.dot(xc.astype(BF16), wg, preferred_element_type=F32) + bg
            r = jax.nn.sigmoid(g[:, 0:d_b])
            ig = jax.nn.sigmoid(g[:, d_b:2 * d_b])
            log_a = (-LRU_C * r) * sp
            a = jnp.exp(log_a)
            bv = jnp.sqrt(-jnp.tanh(log_a) * (a * a + 1.0)) * (ig * xc)
            h, carry = _scan_tile(a, bv, carry, reverse)
            if reverse:
                y = gg_ref[0, pl.ds(t0, t), :] * (hf_ref[pl.ds(t0, t), :] + h)
                out_ref[0, pl.ds(t0, t), :] = y.astype(out_ref.dtype)
            else:
                hf_ref[pl.ds(t0, t), :] = h
            return carry

        lax.fori_loop(0, n_tiles, body, jnp.zeros((1, d_b), F32))

    direction(0, False)
    direction(1, True)


def _lru(xb, gg, cw, cb, wg, bg, lam, *, n_lat, n_ctx):
    bsz, s, d_b = xb.shape
    const2 = lambda b: (0, 0)
    full = lambda a: pl.BlockSpec(a.shape, const2)
    seq = pl.BlockSpec((1, s, d_b), lambda b: (b, 0, 0))
    return pl.pallas_call(
        functools.partial(_lru_kernel, n_lat=n_lat, n_ctx=n_ctx),
        grid=(bsz,),
        in_specs=[seq, seq, full(cw), full(cb), full(wg), full(bg), full(lam)],
        out_specs=seq,
        out_shape=jax.ShapeDtypeStruct((bsz, s, d_b), BF16),
        scratch_shapes=[pltpu.VMEM((s, d_b), F32), pltpu.VMEM((s, d_b), F32)],
        compiler_params=_cparams(("arbitrary",)),
        name="rglru",
    )(xb, gg, cw, cb, wg, bg, lam)


def _attn_kernel(q_ref, k_ref, v_ref, *rest, chunks, d_v):
    o_ref = rest[-1]
    tq = q_ref.shape[1]
    units = [(hh, lo, n) for hh in range(NH) for lo, n in chunks]

    def scores(hh, lo, n):
        qh = q_ref[0, :, hh * LANES:(hh + 1) * LANES]
        kh = k_ref[0, lo:lo + n, hh * LANES:(hh + 1) * LANES]
        return lax.dot_general(qh, kh, (((1,), (1,)), ((), ())), preferred_element_type=F32)

    outs = []
    m = acc = None
    s_next = scores(*units[0])
    for ui, (hh, lo, n) in enumerate(units):
        s = s_next
        if ui + 1 < len(units):
            s_next = scores(*units[ui + 1])
        if lo == chunks[0][0]:
            m = acc = None
        mc = jnp.max(s, axis=-1, keepdims=True)
        m_new = mc if m is None else jnp.maximum(m, mc)
        p = jnp.exp2(s - m_new).astype(BF16)
        pv = jnp.dot(p, v_ref[0, lo:lo + n, hh * LANES:(hh + 1) * LANES],
                     preferred_element_type=F32)
        acc = pv if m is None else jnp.exp2(m - m_new) * acc + pv
        m = m_new
        if lo == chunks[-1][0]:
            outs.append(acc / acc[:, d_v:d_v + 1])
    lane = lax.broadcasted_iota(jnp.int32, (tq, LANES), 1)
    for pp in range(NH // 2):
        pair = jnp.where(lane < d_v, outs[2 * pp], pltpu.roll(outs[2 * pp + 1], d_v, 1))
        o_ref[0, :, pp * LANES:(pp + 1) * LANES] = pair.astype(o_ref.dtype)


def _key_chunks(lo, hi, size):
    return tuple((a, min(size, hi - a)) for a in range(lo, hi, size))


def _attention(q, k, v, *, l_lat, d_v, ctx_queries):
    bsz, s, hp = q.shape
    l_ctx = s - l_lat
    d_c = H_C * d_v
    sem = ("arbitrary", "arbitrary", "arbitrary")
    out_shape = jax.ShapeDtypeStruct((bsz, s, d_c), BF16)
    y = pl.pallas_call(
        functools.partial(_attn_kernel, chunks=_key_chunks(0, s, KC), d_v=d_v),
        grid=(bsz, H_C // NH, l_lat // TQ),
        in_specs=[pl.BlockSpec((1, TQ, NH * LANES), lambda b, h, i: (b, i, h)),
                  pl.BlockSpec((1, s, NH * LANES), lambda b, h, i: (b, 0, h)),
                  pl.BlockSpec((1, s, NH * LANES), lambda b, h, i: (b, 0, h))],
        out_specs=pl.BlockSpec((1, TQ, NH * d_v), lambda b, h, i: (b, i, h)),
        out_shape=out_shape,
        compiler_params=_cparams(sem),
        name="attention",
    )(q, k, v)
    if not ctx_queries:
        return y
    first = l_lat // l_ctx
    return pl.pallas_call(
        functools.partial(_attn_kernel, chunks=_key_chunks(0, l_ctx, KC), d_v=d_v),
        grid=(bsz, H_C // NH, 1),
        in_specs=[pl.BlockSpec((1, l_ctx, NH * LANES), lambda b, h, i: (b, first, h)),
                  pl.BlockSpec((1, l_ctx, NH * LANES), lambda b, h, i: (b, first, h)),
                  pl.BlockSpec((1, l_ctx, NH * LANES), lambda b, h, i: (b, first, h)),
                  pl.BlockSpec(memory_space=pl.ANY)],
        out_specs=pl.BlockSpec((1, l_ctx, NH * d_v), lambda b, h, i: (b, first, h)),
        out_shape=out_shape,
        input_output_aliases={3: 0},
        compiler_params=_cparams(sem),
        name="attention_ctx",
    )(q, k, v, y)


def _postmix_kernel(x_ref, mod_ref, ya_ref, yb_ref, yc_ref, wout_ref, g2_ref, *rest,
                    d, n_experts, route):
    if route:
        wr_ref, x1_ref, h2_ref, re_ref, rg_ref = rest
    else:
        x1_ref, h2_ref = rest
    gate1 = mod_ref[0, :, 2 * d:3 * d]
    shift2 = mod_ref[0, :, 3 * d:4 * d]
    scale2 = mod_ref[0, :, 4 * d:5 * d]
    y = jnp.concatenate([ya_ref[0], yb_ref[0], yc_ref[0]], axis=1)
    x1 = x_ref[0] + gate1 * jnp.dot(y, wout_ref[...], preferred_element_type=F32)
    x1_ref[0] = x1
    h2 = _rms(x1, g2_ref[...]) * (1.0 + scale2) + shift2
    h2_ref[0] = h2.astype(h2_ref.dtype)
    if route:
        logits = jnp.dot(h2, wr_ref[...], preferred_element_type=F32)
        lane = lax.broadcasted_iota(jnp.int32, logits.shape, 1)
        neg_inf = jnp.float32(-jnp.inf)
        lg = jnp.where(lane < n_experts, logits, neg_inf)
        m1 = jnp.max(lg, axis=-1, keepdims=True)
        i1 = jnp.min(jnp.where(lg == m1, lane, LANES), axis=-1, keepdims=True)
        lg2 = jnp.where(lane == i1, neg_inf, lg)
        m2 = jnp.max(lg2, axis=-1, keepdims=True)
        i2 = jnp.min(jnp.where(lg2 == m2, lane, LANES), axis=-1, keepdims=True)
        e = jnp.exp(m2 - m1)
        den = 1.0 + e
        re_ref[0] = jnp.where(lane == 0, i1, i2)
        rg_ref[0] = jnp.where(lane == 0, 1.0 / den, e / den)


def _postmix(xt, mods, ya, yb, yc, wout, g2, wr, *, n_lat_tiles, n_tiles, n_experts):
    bsz, s, d = xt.shape
    route = wr is not None
    tile = lambda b, i: (b, i, 0)
    const2 = lambda b, i: (0, 0)
    full = lambda a: pl.BlockSpec(a.shape, const2)
    in_specs = [pl.BlockSpec((1, TM, d), tile),
                pl.BlockSpec((1, 1, mods.shape[-1]), _tile_mod_index(n_lat_tiles, bsz)),
                pl.BlockSpec((1, TM, ya.shape[-1]), tile), pl.BlockSpec((1, TM, yb.shape[-1]), tile),
                pl.BlockSpec((1, TM, yc.shape[-1]), tile), full(wout), full(g2)]
    args = [xt, mods, ya, yb, yc, wout, g2]
    out_specs = [pl.BlockSpec((1, TM, d), tile), pl.BlockSpec((1, TM, d), tile)]
    out_shape = [jax.ShapeDtypeStruct((bsz, s, d), F32),
                 jax.ShapeDtypeStruct((bsz, s, d), F32 if route else BF16)]
    if route:
        in_specs.append(full(wr))
        args.append(wr)
        out_specs += [pl.BlockSpec((1, TM, LANES), tile), pl.BlockSpec((1, TM, LANES), tile)]
        out_shape += [jax.ShapeDtypeStruct((bsz, s, LANES), jnp.int32),
                      jax.ShapeDtypeStruct((bsz, s, LANES), F32)]
    return pl.pallas_call(
        functools.partial(_postmix_kernel, d=d, n_experts=n_experts, route=route),
        grid=(bsz, n_tiles),
        in_specs=in_specs, out_specs=out_specs, out_shape=out_shape,
        compiler_params=_cparams(("arbitrary", "arbitrary")),
        name="postmix_route" if route else "postmix",
    )(*args)


def _ffn_kernel(x1_ref, h2_ref, mod_ref, w1_ref, w3_ref, w2_ref, o_ref, *, d, fc):
    h = h2_ref[0]
    d_ff = w1_ref.shape[1]
    n_chunks = d_ff // fc

    def up(c):
        return (jnp.dot(h, w1_ref[:, c * fc:(c + 1) * fc], preferred_element_type=F32),
                jnp.dot(h, w3_ref[:, c * fc:(c + 1) * fc], preferred_element_type=F32))

    acc = jnp.zeros((h.shape[0], d), F32)
    a, b = up(0)
    for c in range(n_chunks):
        ac, bc = a, b
        if c + 1 < n_chunks:
            a, b = up(c + 1)
        act = (ac * jax.nn.sigmoid(ac) * bc).astype(BF16)
        acc = acc + jnp.dot(act, w2_ref[c * fc:(c + 1) * fc, :], preferred_element_type=F32)
    gate2 = mod_ref[0, :, 5 * d:6 * d]
    o_ref[0] = x1_ref[0] + gate2 * acc


def _ffn(x1, h2, mods, w1, w3, w2, *, n_lat_tiles, n_tiles):
    bsz, s, d = x1.shape
    tile = lambda b, i: (b, i, 0)
    const2 = lambda b, i: (0, 0)
    full = lambda a: pl.BlockSpec(a.shape, const2)
    return pl.pallas_call(
        functools.partial(_ffn_kernel, d=d, fc=256),
        grid=(bsz, n_tiles),
        in_specs=[pl.BlockSpec((1, TM, d), tile), pl.BlockSpec((1, TM, d), tile),
                  pl.BlockSpec((1, 1, mods.shape[-1]), _tile_mod_index(n_lat_tiles, bsz)),
                  full(w1), full(w3), full(w2)],
        out_specs=pl.BlockSpec((1, TM, d), tile),
        out_shape=jax.ShapeDtypeStruct((bsz, s, d), F32),
        compiler_params=_cparams(("arbitrary", "arbitrary")),
        name="ffn",
    )(x1, h2, mods, w1, w3, w2)


DMA_UNROLL = 8


def _row_copy(src_ref, dst_ref, sem, src_row, dst_row):
    return pltpu.make_async_copy(src_ref.at[pl.ds(src_row, 1)], dst_ref.at[pl.ds(dst_row, 1)], sem)


def _dispatch_kernel(pos_ref, h_ref, xs_in_hbm, xs_hbm, sem):
    del xs_in_hbm
    n = h_ref.shape[1]

    def start(j, _):
        for kk in range(TOP_K):
            _row_copy(h_ref.at[0], xs_hbm, sem, j, pos_ref[0, 0, kk * n + j]).start(priority=kk % 2)
        return 0

    lax.fori_loop(0, n, start, 0, unroll=DMA_UNROLL)
    for kk in range(TOP_K):
        pltpu.make_async_copy(h_ref.at[0], xs_hbm.at[pl.ds(0, n)], sem).wait()


def _pad_fill_kernel(start_ref, len_ref, xs_hbm, zero_ref, sem):
    zero_ref[...] = jnp.zeros_like(zero_ref)
    n_experts = start_ref.shape[0]
    top = zero_ref.shape[0]

    def pieces(e, fn):
        off = start_ref[e]
        size = top
        while size >= SUBLANES:
            @pl.when((len_ref[e] & size) != 0)
            def _(off=off, size=size):
                dst = xs_hbm.at[pl.ds(pl.multiple_of(off, SUBLANES), size)]
                fn(pltpu.make_async_copy(zero_ref.at[pl.ds(0, size)], dst, sem))
            off = off + (len_ref[e] & size)
            size //= 2

    for e in range(n_experts):
        pieces(e, lambda cp: cp.start())
    for e in range(n_experts):
        pieces(e, lambda cp: cp.wait())


def _pad_fill(fill_start, fill_len, n_pad, d, dtype):
    return pl.pallas_call(
        _pad_fill_kernel,
        grid_spec=pltpu.PrefetchScalarGridSpec(
            num_scalar_prefetch=2, grid=(1,), in_specs=[],
            out_specs=pl.BlockSpec(memory_space=pl.ANY),
            scratch_shapes=[pltpu.VMEM((TMB, d), dtype), pltpu.SemaphoreType.DMA(())]),
        out_shape=jax.ShapeDtypeStruct((n_pad, d), dtype),
        compiler_params=_cparams(("arbitrary",)),
        name="moe_pad_fill",
    )(fill_start, fill_len)


def _dispatch(pos, h2, xs0, *, n_tiles):
    bsz, s, d = h2.shape
    n_pad = xs0.shape[0]
    return pl.pallas_call(
        _dispatch_kernel,
        grid=(bsz, n_tiles),
        in_specs=[pl.BlockSpec((1, 1, TOP_K * TM), lambda b, i: (b * n_tiles + i, 0, 0),
                               memory_space=pltpu.SMEM),
                  pl.BlockSpec((1, TM, d), lambda b, i: (b, i, 0)),
                  pl.BlockSpec(memory_space=pl.ANY)],
        out_specs=pl.BlockSpec(memory_space=pl.ANY),
        out_shape=jax.ShapeDtypeStruct((n_pad, d), h2.dtype),
        scratch_shapes=[pltpu.SemaphoreType.DMA(())],
        input_output_aliases={2: 0},
        compiler_params=_cparams(("arbitrary", "arbitrary")),
        name="moe_dispatch",
    )(pos, h2, xs0)


def _experts_kernel(be_ref, nu_ref, xs_ref, w1_ref, w3_ref, w2_ref, o_ref, xb_ref, acc_ref):
    r = pl.program_id(0)
    f = pl.program_id(1)
    nf = pl.num_programs(1)

    @pl.when(r < nu_ref[0])
    def _():
        @pl.when(f == 0)
        def _():
            xb_ref[...] = xs_ref[...].astype(BF16)
            acc_ref[...] = jnp.zeros_like(acc_ref)

        xb = xb_ref[...]
        a = jnp.dot(xb, w1_ref[0, 0].astype(BF16), preferred_element_type=F32)
        b = jnp.dot(xb, w3_ref[0, 0].astype(BF16), preferred_element_type=F32)
        act = (a * jax.nn.sigmoid(a) * b).astype(BF16)
        acc_ref[...] += jnp.dot(act, w2_ref[0, 0].astype(BF16), preferred_element_type=F32)

        @pl.when(f == nf - 1)
        def _():
            o_ref[...] = acc_ref[...]

    @pl.when(jnp.logical_and(r >= nu_ref[0], f == nf - 1))
    def _():
        o_ref[...] = jnp.zeros_like(o_ref)


def _experts(blk_e, n_used, xs, w1, w3, w2, layer):
    n_pad, d = xs.shape
    n_blk = n_pad // TMB
    d_ff = w1.shape[-1]
    nf = d_ff // TF

    def w_col(r, f, be, nu):
        live = r < nu[0]
        return (layer, be[r], 0, jnp.where(live, f, nf - 1))

    def w_row(r, f, be, nu):
        live = r < nu[0]
        return (layer, be[r], jnp.where(live, f, nf - 1), 0)

    grid_spec = pltpu.PrefetchScalarGridSpec(
        num_scalar_prefetch=2,
        grid=(n_blk, nf),
        in_specs=[pl.BlockSpec((TMB, d), lambda r, f, be, nu: (jnp.where(r < nu[0], r, 0), 0)),
                  pl.BlockSpec((1, 1, d, TF), w_col),
                  pl.BlockSpec((1, 1, d, TF), w_col),
                  pl.BlockSpec((1, 1, TF, d), w_row)],
        out_specs=pl.BlockSpec((TMB, d), lambda r, f, be, nu: (r, 0)),
        scratch_shapes=[pltpu.VMEM((TMB, d), BF16), pltpu.VMEM((TMB, d), F32)])
    return pl.pallas_call(
        _experts_kernel,
        grid_spec=grid_spec,
        out_shape=jax.ShapeDtypeStruct((n_pad, d), F32),
        compiler_params=_cparams(("arbitrary", "arbitrary")),
        name="moe_experts",
    )(blk_e, n_used, xs, w1, w3, w2)


def _combine_kernel(pos_ref, x1_ref, mod_ref, rg_ref, yp_hbm, *rest, d, final):
    if final:
        gf_ref, o_ref, buf_ref, sem = rest
    else:
        o_ref, buf_ref, sem = rest
    n = x1_ref.shape[1]

    def start(j, _):
        for kk in range(TOP_K):
            _row_copy(yp_hbm, buf_ref.at[kk], sem, pos_ref[0, 0, kk * n + j], j).start(priority=kk % 2)
        return 0

    lax.fori_loop(0, n, start, 0, unroll=DMA_UNROLL)
    for kk in range(TOP_K):
        pltpu.make_async_copy(yp_hbm.at[pl.ds(0, n)], buf_ref.at[kk], sem).wait()
    gate2 = mod_ref[0, :, 5 * d:6 * d]
    rg = rg_ref[0]
    y = rg[:, 0:1] * buf_ref[0]
    for kk in range(1, TOP_K):
        y = y + rg[:, kk:kk + 1] * buf_ref[kk]
    x2 = x1_ref[0] + gate2 * y
    if final:
        x2 = _rms(x2, gf_ref[...])
    o_ref[0] = x2


def _combine(pos, x1, mods, route_g, yp, gf, *, n_lat_tiles, n_tiles):
    bsz, s, d = x1.shape
    final = gf is not None
    tile = lambda b, i: (b, i, 0)
    in_specs = [pl.BlockSpec((1, 1, TOP_K * TM), lambda b, i: (b * n_tiles + i, 0, 0),
                             memory_space=pltpu.SMEM),
                pl.BlockSpec((1, TM, d), tile),
                pl.BlockSpec((1, 1, mods.shape[-1]), _tile_mod_index(n_lat_tiles, bsz)),
                pl.BlockSpec((1, TM, LANES), tile),
                pl.BlockSpec(memory_space=pl.ANY)]
    args = [pos, x1, mods, route_g, yp]
    if final:
        in_specs.append(pl.BlockSpec(gf.shape, lambda b, i: (0, 0)))
        args.append(gf)
    return pl.pallas_call(
        functools.partial(_combine_kernel, d=d, final=final),
        grid=(bsz, n_tiles),
        in_specs=in_specs,
        out_specs=pl.BlockSpec((1, TM, d), tile),
        out_shape=jax.ShapeDtypeStruct((bsz, n_tiles * TM, d), F32),
        scratch_shapes=[pltpu.VMEM((TOP_K, TM, d), F32), pltpu.SemaphoreType.DMA(())],
        compiler_params=_cparams(("arbitrary", "arbitrary")),
        name="moe_combine_final" if final else "moe_combine",
    )(*args)


def _route_tables(route_e, n_experts, s_eff):
    bsz = route_e.shape[0]
    s = s_eff
    n_assign = bsz * s * TOP_K
    lane = jnp.arange(route_e.shape[-1], dtype=jnp.int32)
    flat_e = jnp.stack([jnp.max(jnp.where(lane == k, route_e[:, :s], -1), axis=-1)
                        for k in range(TOP_K)]).reshape(n_assign)
    onehot = (flat_e[:, None] == jnp.arange(n_experts, dtype=jnp.int32)[None, :]).astype(jnp.int32)
    csum = jnp.cumsum(onehot, axis=0)
    rank = jnp.sum(csum * onehot, axis=1) - 1
    counts = csum[-1]
    padded = (counts + TMB - 1) // TMB * TMB
    pad_ends = jnp.cumsum(padded)
    pad_starts = pad_ends - padded
    dest = pad_starts[flat_e] + rank
    n_blk = (n_assign + TMB - 1) // TMB + n_experts
    n_pad = n_blk * TMB
    blk_e = jnp.minimum(
        jnp.searchsorted(pad_ends, jnp.arange(n_blk, dtype=jnp.int32) * TMB, side='right'),
        n_experts - 1).astype(jnp.int32)
    n_used = (pad_ends[-1] // TMB).astype(jnp.int32).reshape(1)
    fill_start = ((pad_starts + counts) // SUBLANES * SUBLANES).astype(jnp.int32)
    fill_len = (pad_ends - fill_start).astype(jnp.int32)
    n_tok_tiles = bsz * s // TM
    pos = dest.astype(jnp.int32).reshape(TOP_K, n_tok_tiles, TM).transpose(1, 0, 2)
    return blk_e, n_used, pos.reshape(n_tok_tiles, 1, TOP_K * TM), n_pad, fill_start, fill_len


def _final_kernel(x_ref, g_ref, o_ref):
    o_ref[0] = _rms(x_ref[0], g_ref[...])


def _final_norm(xt, g, *, n_tiles):
    bsz, s, d = xt.shape
    tile = lambda b, i: (b, i, 0)
    return pl.pallas_call(
        _final_kernel,
        grid=(bsz, n_tiles),
        in_specs=[pl.BlockSpec((1, TM, d), tile), pl.BlockSpec(g.shape, lambda b, i: (0, 0))],
        out_specs=pl.BlockSpec((1, TM, d), tile),
        out_shape=jax.ShapeDtypeStruct((bsz, n_tiles * TM, d), F32),
        compiler_params=_cparams(("arbitrary", "arbitrary")),
        name="final_norm",
    )(xt, g)


def _rope_partner(r):
    return jnp.concatenate([-r[..., 8:16], r[..., 0:8], -r[..., 24:32], r[..., 16:24]], axis=-1)


def _rope_tables(l_lat, s_total, d_rope, d_nope):
    t = jnp.arange(l_lat, dtype=jnp.int32)
    row = (t // GRID_W).astype(F32)
    col = (t % GRID_W).astype(F32)
    half = d_rope // 2
    inv_freq = ROPE_BASE ** (-jnp.arange(0, half, 2, dtype=F32) / half)
    ang_r = row[:, None] * inv_freq
    ang_c = col[:, None] * inv_freq
    cos = jnp.concatenate([jnp.cos(ang_r), jnp.cos(ang_r), jnp.cos(ang_c), jnp.cos(ang_c)], axis=1)
    sin = jnp.concatenate([jnp.sin(ang_r), jnp.sin(ang_r), jnp.sin(ang_c), jnp.sin(ang_c)], axis=1)
    n_ctx = s_total - l_lat
    cosk = jnp.concatenate([cos, jnp.ones((n_ctx, d_rope), F32)], axis=0)
    sink = jnp.concatenate([sin, jnp.zeros((n_ctx, d_rope), F32)], axis=0)
    pad = LANES - d_nope - d_rope
    cosq = jnp.concatenate([jnp.ones((s_total, d_nope), F32), cosk, jnp.zeros((s_total, pad), F32)], axis=1)
    sinq = jnp.concatenate([jnp.zeros((s_total, d_nope), F32), sink, jnp.zeros((s_total, pad), F32)], axis=1)
    return cosq, sinq, cosk, sink


def _block_diag(w):
    h, a, b = w.shape
    eye = jnp.eye(h, dtype=w.dtype)
    return (eye[:, None, :, None] * w[:, :, None, :]).reshape(h * a, h * b)


def kernel(x, c, ctx, c_ctx, w_mod, b_mod, norm1_g, norm2_g, w_in, w_out, sgu_ln_g, sgu_ln_b, sgu_w, sgu_b, conv_w, conv_b, lru_w_a, lru_b_a, lru_w_x, lru_b_x, lru_lam, mla_q_norm, mla_w_uq, mla_kv_norm, mla_w_uk, mla_w_uv, ffn_w1, ffn_w3, ffn_w2, moe_router, moe_w1, moe_w3, moe_w2, final_norm_g):
    bsz, l_lat, d = x.shape
    l_ctx = ctx.shape[1]
    s_total = l_lat + l_ctx
    depth = w_mod.shape[0]
    d_a = sgu_ln_g.shape[-1]
    h_a, chunk = sgu_w.shape[1], sgu_w.shape[2]
    d_b = conv_w.shape[-1]
    q_lora = mla_q_norm.shape[-1]
    kv_lora = mla_kv_norm.shape[-1]
    d_c = mla_w_uv.shape[-1]
    d_v = d_c // H_C
    d_nope = mla_w_uk.shape[-1] // H_C
    d_qk = mla_w_uq.shape[-1] // H_C
    d_rope = d_qk - d_nope
    n_experts = moe_router.shape[-1]
    assert l_lat % TM == 0 and l_ctx % TM == 0 and TM % chunk == 0 and TM == T_SCAN
    assert l_lat % TQ == 0 and l_lat % l_ctx == 0
    assert d_qk <= LANES and 2 * d_v == LANES and H_C % 2 == 0 and d_rope == 32
    n_lat_tiles = l_lat // TM
    n_ctx_tiles = l_ctx // TM
    dims = (d, d_a, d_b, q_lora, kv_lora, d_rope, h_a, chunk)

    xt = jnp.concatenate([x, ctx], axis=1)

    n_rows = (bsz + 1 + 7) // 8 * 8
    cond = jnp.zeros((n_rows, d), F32).at[:bsz].set(c).at[bsz].set(c_ctx)
    mods_all = _modulation(cond, w_mod.astype(BF16), b_mod[:, None, :])
    mods_all = mods_all[:, :bsz + 1, None, :]

    cosq, sinq, cosk, sink = _rope_tables(l_lat, s_total, d_rope, d_nope)
    q_scale = float(d_qk) ** -0.5 * 1.4426950408889634
    cosq, sinq = cosq * q_scale, sinq * q_scale
    head_pad = LANES - d_qk
    vone = jnp.tile((jnp.arange(LANES) == d_v).astype(F32)[None], (1, H_C))
    e_head = jnp.concatenate([jnp.zeros((d_rope, d_nope), F32), jnp.eye(d_rope, dtype=F32),
                              jnp.zeros((d_rope, head_pad), F32)], axis=1)
    emat = jnp.tile(e_head, (1, H_C)).astype(BF16)

    out = None
    for l in range(depth):
        last = l == depth - 1
        mods = mods_all[l]
        o_m = 2 * d_a + 2 * d_b
        o_r = o_m + q_lora + kv_lora
        w_rope = w_in[l][:, o_r:o_r + d_rope]
        n_in = (o_r + 2 * d_rope + LANES - 1) // LANES * LANES
        win = jnp.concatenate([w_in[l][:, :o_r + d_rope], _rope_partner(w_rope),
                               jnp.zeros((d, n_in - o_r - 2 * d_rope), F32)], axis=1).astype(BF16)
        wq = mla_w_uq[l].reshape(q_lora, H_C, d_qk)
        zq = jnp.zeros((q_lora, H_C, head_pad), F32)
        wuq = jnp.concatenate([wq, zq], axis=-1).reshape(q_lora, H_C * LANES).astype(BF16)
        wuqp = jnp.concatenate([jnp.zeros((q_lora, H_C, d_nope), F32), _rope_partner(wq[..., d_nope:]), zq],
                               axis=-1).reshape(q_lora, H_C * LANES).astype(BF16)
        wk = mla_w_uk[l].reshape(kv_lora, H_C, d_nope)
        wuk = jnp.concatenate([wk, jnp.zeros((kv_lora, H_C, LANES - d_nope), F32)],
                              axis=-1).reshape(kv_lora, H_C * LANES).astype(BF16)
        wv = mla_w_uv[l].reshape(kv_lora, H_C, d_v)
        wuv = jnp.concatenate([wv, jnp.zeros((kv_lora, H_C, LANES - d_v), F32)],
                              axis=-1).reshape(kv_lora, H_C * LANES).astype(BF16)
        bs_full = jnp.repeat(sgu_b[l].T, d_a // h_a, axis=1)

        ya, gg, xb, q, k, v = _premix(
            xt, mods, norm1_g[l][None], win, sgu_ln_g[l][None], sgu_ln_b[l][None],
            sgu_w[l].astype(BF16), bs_full, mla_q_norm[l][None], wuq, wuqp, mla_kv_norm[l][None],
            wuk, emat, wuv, vone, cosq, sinq, cosk, sink,
            n_lat_tiles=n_lat_tiles, dims=dims)

        wg = jnp.concatenate([_block_diag(lru_w_a[l, 0]), _block_diag(lru_w_x[l, 0]),
                              _block_diag(lru_w_a[l, 1]), _block_diag(lru_w_x[l, 1])], axis=1).astype(BF16)
        bg = jnp.concatenate([lru_b_a[l, 0], lru_b_x[l, 0], lru_b_a[l, 1], lru_b_x[l, 1]])[None]
        yb = _lru(xb, gg, conv_w[l], conv_b[l][None], wg, bg, lru_lam[l], n_lat=n_lat_tiles, n_ctx=n_ctx_tiles)

        n_tiles = n_lat_tiles if last else n_lat_tiles + n_ctx_tiles
        yc = _attention(q, k, v, l_lat=l_lat, d_v=d_v, ctx_queries=not last)

        if l % 2 == 0:
            i = l // 2
            x1, h2 = _postmix(xt, mods, ya, yb, yc, w_out[l].astype(BF16), norm2_g[l][None], None,
                              n_lat_tiles=n_lat_tiles, n_tiles=n_tiles, n_experts=n_experts)
            xt = _ffn(x1, h2, mods, ffn_w1[i].astype(BF16), ffn_w3[i].astype(BF16), ffn_w2[i].astype(BF16),
                      n_lat_tiles=n_lat_tiles, n_tiles=n_tiles)
            if last:
                out = _final_norm(xt, final_norm_g[None], n_tiles=n_lat_tiles)
        else:
            i = l // 2
            wr = jnp.concatenate([moe_router[i], jnp.zeros((d, LANES - n_experts), F32)], axis=1)
            x1, h2, route_e, route_g = _postmix(xt, mods, ya, yb, yc, w_out[l].astype(BF16),
                                                norm2_g[l][None], wr,
                                                n_lat_tiles=n_lat_tiles, n_tiles=n_tiles, n_experts=n_experts)
            blk_e, n_used, pos, n_pad, fill_start, fill_len = _route_tables(route_e, n_experts, n_tiles * TM)
            xs = _dispatch(pos, h2, _pad_fill(fill_start, fill_len, n_pad, d, h2.dtype), n_tiles=n_tiles)
            yp = _experts(blk_e, n_used, xs, moe_w1, moe_w3, moe_w2, i)
            out_or_xt = _combine(pos, x1, mods, route_g, yp, final_norm_g[None] if last else None,
                                 n_lat_tiles=n_lat_tiles, n_tiles=n_tiles)
            if last:
                out = out_or_xt
            else:
                xt = out_or_xt
    return out
```

```python
import functools

import jax
import jax.numpy as jnp
from jax import lax
from jax.experimental import pallas as pl
from jax.experimental.pallas import tpu as pltpu

F32 = jnp.float32
BF16 = jnp.bfloat16

EPS = 1e-6
GRID_W = 64
ROPE_BASE = 10000.0
LRU_C = 8.0
H_C = 8
TOP_K = 2
LANES = 128
TM = 256
T_SCAN = 256
KC = 2048
NH = 4
TQ = 512
TMB = 1024
TF = 512
VMEM_LIMIT = 56 * 1024 * 1024


def _cparams(sem):
    return pltpu.CompilerParams(dimension_semantics=sem, vmem_limit_bytes=VMEM_LIMIT)


def _rms(x, g):
    return x * lax.rsqrt(jnp.mean(x * x, axis=-1, keepdims=True) + EPS) * g


def _mod_kernel(c_ref, w_ref, b_ref, o_ref):
    c = c_ref[...]
    a = (c * jax.nn.sigmoid(c)).astype(BF16)
    o_ref[0] = jnp.dot(a, w_ref[0], preferred_element_type=F32) + b_ref[0]


def _modulation(cond, w_mod, b_mod):
    depth, d, n = w_mod.shape
    r = cond.shape[0]
    tn = 1024
    return pl.pallas_call(
        _mod_kernel,
        grid=(depth, n // tn),
        in_specs=[pl.BlockSpec((r, d), lambda l, j: (0, 0)),
                  pl.BlockSpec((1, d, tn), lambda l, j: (l, 0, j)),
                  pl.BlockSpec((1, 1, tn), lambda l, j: (l, 0, j))],
        out_specs=pl.BlockSpec((1, r, tn), lambda l, j: (l, 0, j)),
        out_shape=jax.ShapeDtypeStruct((depth, r, n), F32),
        compiler_params=_cparams(("arbitrary", "arbitrary")),
        name="modulation",
    )(cond, w_mod, b_mod)


def _premix_kernel(x_ref, mod_ref, g_ref, win_ref, lng_ref, lnb_ref, ws_ref, bs_ref,
                   qn_ref, wuq_ref, wuqp_ref, kvn_ref, wuk_ref, e_ref, wuv_ref, vone_ref,
                   cq_ref, sq_ref, ck_ref, sk_ref,
                   ya_ref, gg_ref, xb_ref, q_ref, k_ref, v_ref, *, dims):
    d, d_a, d_b, q_lora, kv_lora, d_rope, h_a, chunk = dims
    x = x_ref[0]
    shift = mod_ref[0, :, 0:d]
    scale = mod_ref[0, :, d:2 * d]
    h = (_rms(x, g_ref[...]) * (1.0 + scale) + shift).astype(BF16)
    z = jnp.dot(h, win_ref[...], preferred_element_type=F32)

    o = 0
    u = jax.nn.gelu(z[:, o:o + d_a])
    v = jax.nn.gelu(z[:, o + d_a:o + 2 * d_a])
    mu = jnp.mean(v, axis=-1, keepdims=True)
    vc = v - mu
    var = jnp.mean(vc * vc, axis=-1, keepdims=True)
    vn = (vc * lax.rsqrt(var + EPS) * lng_ref[...] + lnb_ref[...]).astype(BF16)
    dh_a = d_a // h_a
    tm = x.shape[0]
    head_of_lane = lax.broadcasted_iota(jnp.int32, (chunk, d_a), 1) // dh_a
    for c in range(tm // chunk):
        vch = vn[c * chunk:(c + 1) * chunk]
        s = jnp.dot(ws_ref[0], vch, preferred_element_type=F32)
        for hd in range(1, h_a):
            s = jnp.where(head_of_lane == hd,
                          jnp.dot(ws_ref[hd], vch, preferred_element_type=F32), s)
        s = s + bs_ref[...]
        ya_ref[0, c * chunk:(c + 1) * chunk, :] = (u[c * chunk:(c + 1) * chunk] * s).astype(BF16)

    o = 2 * d_a
    gg_ref[0] = jax.nn.gelu(z[:, o:o + d_b])
    xb_ref[0] = z[:, o + d_b:o + 2 * d_b]

    o = 2 * d_a + 2 * d_b
    cq = _rms(z[:, o:o + q_lora], qn_ref[...]).astype(BF16)
    qa = jnp.dot(cq, wuq_ref[...], preferred_element_type=F32)
    qb = jnp.dot(cq, wuqp_ref[...], preferred_element_type=F32)
    cos_q = jnp.concatenate([cq_ref[...]] * H_C, axis=1)
    sin_q = jnp.concatenate([sq_ref[...]] * H_C, axis=1)
    q_ref[0] = (qa * cos_q + qb * sin_q).astype(BF16)
    o += q_lora
    ckv = _rms(z[:, o:o + kv_lora], kvn_ref[...]).astype(BF16)
    o += kv_lora
    zr = z[:, o:o + d_rope]
    zrp = z[:, o + d_rope:o + 2 * d_rope]
    kr = (zr * ck_ref[...] + zrp * sk_ref[...]).astype(BF16)
    kn = jnp.dot(ckv, wuk_ref[...], preferred_element_type=F32)
    k_ref[0] = (kn + jnp.dot(kr, e_ref[...], preferred_element_type=F32)).astype(BF16)
    v_ref[0] = (jnp.dot(ckv, wuv_ref[...], preferred_element_type=F32) + vone_ref[...]).astype(BF16)


def _tile_mod_index(n_lat_tiles, n_batch):
    def index(b, i):
        return (jnp.where(i < n_lat_tiles, b, n_batch), 0, 0)
    return index


def _premix(xt, mods, g1, win, lng, lnb, ws, bs, qn, wuq, wuqp, kvn, wuk, emat, wuv, vone,
            cosq, sinq, cosk, sink, *, n_lat_tiles, dims):
    bsz, s, d = xt.shape
    d_a, d_b = dims[1], dims[2]
    hp = H_C * LANES
    const2 = lambda b, i: (0, 0)
    const3 = lambda b, i: (0, 0, 0)
    tile = lambda b, i: (b, i, 0)
    full = lambda a: pl.BlockSpec(a.shape, const2 if a.ndim == 2 else const3)
    return pl.pallas_call(
        functools.partial(_premix_kernel, dims=dims),
        grid=(bsz, s // TM),
        in_specs=[pl.BlockSpec((1, TM, d), tile),
                  pl.BlockSpec((1, 1, mods.shape[-1]), _tile_mod_index(n_lat_tiles, bsz)),
                  full(g1), full(win), full(lng), full(lnb), full(ws), full(bs),
                  full(qn), full(wuq), full(wuqp), full(kvn), full(wuk), full(emat), full(wuv), full(vone),
                  pl.BlockSpec((TM, LANES), lambda b, i: (i, 0)),
                  pl.BlockSpec((TM, LANES), lambda b, i: (i, 0)),
                  pl.BlockSpec((TM, cosk.shape[1]), lambda b, i: (i, 0)),
                  pl.BlockSpec((TM, sink.shape[1]), lambda b, i: (i, 0))],
        out_specs=[pl.BlockSpec((1, TM, d_a), tile), pl.BlockSpec((1, TM, d_b), tile),
                   pl.BlockSpec((1, TM, d_b), tile), pl.BlockSpec((1, TM, hp), tile),
                   pl.BlockSpec((1, TM, hp), tile), pl.BlockSpec((1, TM, hp), tile)],
        out_shape=[jax.ShapeDtypeStruct((bsz, s, d_a), BF16),
                   jax.ShapeDtypeStruct((bsz, s, d_b), F32),
                   jax.ShapeDtypeStruct((bsz, s, d_b), F32),
                   jax.ShapeDtypeStruct((bsz, s, hp), BF16),
                   jax.ShapeDtypeStruct((bsz, s, hp), BF16),
                   jax.ShapeDtypeStruct((bsz, s, hp), BF16)],
        compiler_params=_cparams(("arbitrary", "arbitrary")),
        name="premix",
    )(xt, mods, g1, win, lng, lnb, ws, bs, qn, wuq, wuqp, kvn, wuk, emat, wuv, vone,
      cosq, sinq, cosk, sink)


SUBLANES = 8


def _scan_tile(a, b, carry, reverse):
    t, c = a.shape
    g = SUBLANES
    n_groups = t // g
    a = a.reshape(n_groups, g, c)
    b = b.reshape(n_groups, g, c)
    rows = lax.broadcasted_iota(jnp.int32, a.shape, 1)
    s = 1
    while s < g:
        shift = g - s if reverse else s
        ok = rows < g - s if reverse else rows >= s
        a_sh = pltpu.roll(a, shift, 1)
        b_sh = pltpu.roll(b, shift, 1)
        b = b + a * jnp.where(ok, b_sh, 0.0)
        a = a * jnp.where(ok, a_sh, 1.0)
        s *= 2
    hs = [None] * n_groups
    for j in (range(n_groups - 1, -1, -1) if reverse else range(n_groups)):
        hj = b[j] + a[j] * carry
        hs[j] = hj
        carry = hj[0:1] if reverse else hj[g - 1:g]
    return jnp.concatenate(hs, axis=0), carry


def _lru_kernel(xb_ref, gg_ref, cw_ref, cb_ref, wg_ref, bg_ref, lam_ref, out_ref, xc_ref, hf_ref,
                *, n_lat, n_ctx):
    t = T_SCAN
    n_tiles = n_lat + n_ctx
    s_total = n_tiles * t
    d_b = xb_ref.shape[-1]
    w = cw_ref[...]
    cb = cb_ref[...]

    def conv_body(j, _):
        t0 = pl.multiple_of(j * t, t)
        is_ctx = j >= n_lat
        seq_lo = jnp.where(is_ctx, n_lat * t, 0)
        seq_hi = jnp.where(is_ctx, s_total, n_lat * t)
        cur = xb_ref[0, pl.ds(t0, t), :]
        p0 = pl.multiple_of(jnp.maximum(t0 - 8, 0), 8)
        n0 = pl.multiple_of(jnp.minimum(t0 + t, s_total - 8), 8)
        prev = jnp.where(t0 > seq_lo, xb_ref[0, pl.ds(p0, 8), :], 0.0)
        nxt = jnp.where(t0 + t < seq_hi, xb_ref[0, pl.ds(n0, 8), :], 0.0)
        ext = jnp.concatenate([prev, cur, nxt], axis=0)
        n_ext = t + 16
        xm2 = pltpu.roll(ext, 2, 0)[8:8 + t]
        xm1 = pltpu.roll(ext, 1, 0)[8:8 + t]
        xp1 = pltpu.roll(ext, n_ext - 1, 0)[8:8 + t]
        xc_ref[pl.ds(t0, t), :] = (w[0:1] * xm2 + w[1:2] * xm1 + w[2:3] * cur + w[3:4] * xp1 + cb)
        return 0

    lax.fori_loop(0, n_tiles, conv_body, 0)

    lam = lam_ref[...]
    neg = -lam
    softplus = jnp.maximum(neg, 0.0) + jnp.log1p(jnp.exp(-jnp.abs(neg)))

    def direction(dr, reverse):
        sp = softplus[dr:dr + 1]
        wg = wg_ref[:, dr * 2 * d_b:(dr + 1) * 2 * d_b]
        bg = bg_ref[:, dr * 2 * d_b:(dr + 1) * 2 * d_b]

        def body(j, carry):
            if reverse:
                idx = jnp.where(j < n_ctx, n_tiles - 1 - j, n_lat - 1 - (j - n_ctx))
            else:
                idx = jnp.where(j < n_ctx, n_lat + j, j - n_ctx)
            t0 = pl.multiple_of(idx * t, t)
            xc = xc_ref[pl.ds(t0, t), :]
            g = jnp.dot(xc.astype(BF16), wg, preferred_element_type=F32) + bg
            r = jax.nn.sigmoid(g[:, 0:d_b])
            ig = jax.nn.sigmoid(g[:, d_b:2 * d_b])
            log_a = (-LRU_C * r) * sp
            a = jnp.exp(log_a)
            bv = jnp.sqrt(-jnp.tanh(log_a) * (a * a + 1.0)) * (ig * xc)
            h, carry = _scan_tile(a, bv, carry, reverse)
            if reverse:
                y = gg_ref[0, pl.ds(t0, t), :] * (hf_ref[pl.ds(t0, t), :] + h)
                out_ref[0, pl.ds(t0, t), :] = y.astype(out_ref.dtype)
            else:
                hf_ref[pl.ds(t0, t), :] = h
            return carry

        lax.fori_loop(0, n_tiles, body, jnp.zeros((1, d_b), F32))

    direction(0, False)
    direction(1, True)


def _lru(xb, gg, cw, cb, wg, bg, lam, *, n_lat, n_ctx):
    bsz, s, d_b = xb.shape
    const2 = lambda b: (0, 0)
    full = lambda a: pl.BlockSpec(a.shape, const2)
    seq = pl.BlockSpec((1, s, d_b), lambda b: (b, 0, 0))
    return pl.pallas_call(
        functools.partial(_lru_kernel, n_lat=n_lat, n_ctx=n_ctx),
        grid=(bsz,),
        in_specs=[seq, seq, full(cw), full(cb), full(wg), full(bg), full(lam)],
        out_specs=seq,
        out_shape=jax.ShapeDtypeStruct((bsz, s, d_b), BF16),
        scratch_shapes=[pltpu.VMEM((s, d_b), F32), pltpu.VMEM((s, d_b), F32)],
        compiler_params=_cparams(("arbitrary",)),
        name="rglru",
    )(xb, gg, cw, cb, wg, bg, lam)


def _attn_kernel(q_ref, k_ref, v_ref, *rest, chunks, d_v):
    o_ref = rest[-1]
    tq = q_ref.shape[1]
    units = [(hh, lo, n) for hh in range(NH) for lo, n in chunks]

    def scores(hh, lo, n):
        qh = q_ref[0, :, hh * LANES:(hh + 1) * LANES]
        kh = k_ref[0, lo:lo + n, hh * LANES:(hh + 1) * LANES]
        return lax.dot_general(qh, kh, (((1,), (1,)), ((), ())), preferred_element_type=F32)

    outs = []
    m = acc = None
    s_next = scores(*units[0])
    for ui, (hh, lo, n) in enumerate(units):
        s = s_next
        if ui + 1 < len(units):
            s_next = scores(*units[ui + 1])
        if lo == chunks[0][0]:
            m = acc = None
        mc = jnp.max(s, axis=-1, keepdims=True)
        m_new = mc if m is None else jnp.maximum(m, mc)
        p = jnp.exp2(s - m_new).astype(BF16)
        pv = jnp.dot(p, v_ref[0, lo:lo + n, hh * LANES:(hh + 1) * LANES],
                     preferred_element_type=F32)
        acc = pv if m is None else jnp.exp2(m - m_new) * acc + pv
        m = m_new
        if lo == chunks[-1][0]:
            outs.append(acc / acc[:, d_v:d_v + 1])
    lane = lax.broadcasted_iota(jnp.int32, (tq, LANES), 1)
    for pp in range(NH // 2):
        pair = jnp.where(lane < d_v, outs[2 * pp], pltpu.roll(outs[2 * pp + 1], d_v, 1))
        o_ref[0, :, pp * LANES:(pp + 1) * LANES] = pair.astype(o_ref.dtype)


def _key_chunks(lo, hi, size):
    return tuple((a, min(size, hi - a)) for a in range(lo, hi, size))


def _attention(q, k, v, *, l_lat, d_v, ctx_queries):
    bsz, s, hp = q.shape
    l_ctx = s - l_lat
    d_c = H_C * d_v
    sem = ("arbitrary", "arbitrary", "arbitrary")
    out_shape = jax.ShapeDtypeStruct((bsz, s, d_c), BF16)
    y = pl.pallas_call(
        functools.partial(_attn_kernel, chunks=_key_chunks(0, s, KC), d_v=d_v),
        grid=(bsz, H_C // NH, l_lat // TQ),
        in_specs=[pl.BlockSpec((1, TQ, NH * LANES), lambda b, h, i: (b, i, h)),
                  pl.BlockSpec((1, s, NH * LANES), lambda b, h, i: (b, 0, h)),
                  pl.BlockSpec((1, s, NH * LANES), lambda b, h, i: (b, 0, h))],
        out_specs=pl.BlockSpec((1, TQ, NH * d_v), lambda b, h, i: (b, i, h)),
        out_shape=out_shape,
        compiler_params=_cparams(sem),
        name="attention",
    )(q, k, v)
    if not ctx_queries:
        return y
    first = l_lat // l_ctx
    return pl.pallas_call(
        functools.partial(_attn_kernel, chunks=_key_chunks(0, l_ctx, KC), d_v=d_v),
        grid=(bsz, H_C // NH, 1),
        in_specs=[pl.BlockSpec((1, l_ctx, NH * LANES), lambda b, h, i: (b, first, h)),
                  pl.BlockSpec((1, l_ctx, NH * LANES), lambda b, h, i: (b, first, h)),
                  pl.BlockSpec((1, l_ctx, NH * LANES), lambda b, h, i: (b, first, h)),
                  pl.BlockSpec(memory_space=pl.ANY)],
        out_specs=pl.BlockSpec((1, l_ctx, NH * d_v), lambda b, h, i: (b, first, h)),
        out_shape=out_shape,
        input_output_aliases={3: 0},
        compiler_params=_cparams(sem),
        name="attention_ctx",
    )(q, k, v, y)


def _postmix_kernel(x_ref, mod_ref, ya_ref, yb_ref, yc_ref, wout_ref, g2_ref, *rest,
                    d, n_experts, route):
    if route:
        wr_ref, x1_ref, h2_ref, re_ref, rg_ref = rest
    else:
        x1_ref, h2_ref = rest
    gate1 = mod_ref[0, :, 2 * d:3 * d]
    shift2 = mod_ref[0, :, 3 * d:4 * d]
    scale2 = mod_ref[0, :, 4 * d:5 * d]
    y = jnp.concatenate([ya_ref[0], yb_ref[0], yc_ref[0]], axis=1)
    x1 = x_ref[0] + gate1 * jnp.dot(y, wout_ref[...], preferred_element_type=F32)
    x1_ref[0] = x1
    h2 = _rms(x1, g2_ref[...]) * (1.0 + scale2) + shift2
    h2_ref[0] = h2.astype(h2_ref.dtype)
    if route:
        logits = jnp.dot(h2, wr_ref[...], preferred_element_type=F32)
        lane = lax.broadcasted_iota(jnp.int32, logits.shape, 1)
        neg_inf = jnp.float32(-jnp.inf)
        lg = jnp.where(lane < n_experts, logits, neg_inf)
        m1 = jnp.max(lg, axis=-1, keepdims=True)
        i1 = jnp.min(jnp.where(lg == m1, lane, LANES), axis=-1, keepdims=True)
        lg2 = jnp.where(lane == i1, neg_inf, lg)
        m2 = jnp.max(lg2, axis=-1, keepdims=True)
        i2 = jnp.min(jnp.where(lg2 == m2, lane, LANES), axis=-1, keepdims=True)
        e = jnp.exp(m2 - m1)
        den = 1.0 + e
        re_ref[0] = jnp.where(lane == 0, i1, i2)
        rg_ref[0] = jnp.where(lane == 0, 1.0 / den, e / den)


def _postmix(xt, mods, ya, yb, yc, wout, g2, wr, *, n_lat_tiles, n_tiles, n_experts):
    bsz, s, d = xt.shape
    route = wr is not None
    tile = lambda b, i: (b, i, 0)
    const2 = lambda b, i: (0, 0)
    full = lambda a: pl.BlockSpec(a.shape, const2)
    in_specs = [pl.BlockSpec((1, TM, d), tile),
                pl.BlockSpec((1, 1, mods.shape[-1]), _tile_mod_index(n_lat_tiles, bsz)),
                pl.BlockSpec((1, TM, ya.shape[-1]), tile), pl.BlockSpec((1, TM, yb.shape[-1]), tile),
                pl.BlockSpec((1, TM, yc.shape[-1]), tile), full(wout), full(g2)]
    args = [xt, mods, ya, yb, yc, wout, g2]
    out_specs = [pl.BlockSpec((1, TM, d), tile), pl.BlockSpec((1, TM, d), tile)]
    out_shape = [jax.ShapeDtypeStruct((bsz, s, d), F32),
                 jax.ShapeDtypeStruct((bsz, s, d), F32 if route else BF16)]
    if route:
        in_specs.append(full(wr))
        args.append(wr)
        out_specs += [pl.BlockSpec((1, TM, LANES), tile), pl.BlockSpec((1, TM, LANES), tile)]
        out_shape += [jax.ShapeDtypeStruct((bsz, s, LANES), jnp.int32),
                      jax.ShapeDtypeStruct((bsz, s, LANES), F32)]
    return pl.pallas_call(
        functools.partial(_postmix_kernel, d=d, n_experts=n_experts, route=route),
        grid=(bsz, n_tiles),
        in_specs=in_specs, out_specs=out_specs, out_shape=out_shape,
        compiler_params=_cparams(("arbitrary", "arbitrary")),
        name="postmix_route" if route else "postmix",
    )(*args)


def _ffn_kernel(x1_ref, h2_ref, mod_ref, w1_ref, w3_ref, w2_ref, o_ref, *, d, fc):
    h = h2_ref[0]
    d_ff = w1_ref.shape[1]
    n_chunks = d_ff // fc

    def up(c):
        return (jnp.dot(h, w1_ref[:, c * fc:(c + 1) * fc], preferred_element_type=F32),
                jnp.dot(h, w3_ref[:, c * fc:(c + 1) * fc], preferred_element_type=F32))

    acc = jnp.zeros((h.shape[0], d), F32)
    a, b = up(0)
    for c in range(n_chunks):
        ac, bc = a, b
        if c + 1 < n_chunks:
            a, b = up(c + 1)
        act = (ac * jax.nn.sigmoid(ac) * bc).astype(BF16)
        acc = acc + jnp.dot(act, w2_ref[c * fc:(c + 1) * fc, :], preferred_element_type=F32)
    gate2 = mod_ref[0, :, 5 * d:6 * d]
    o_ref[0] = x1_ref[0] + gate2 * acc


def _ffn(x1, h2, mods, w1, w3, w2, *, n_lat_tiles, n_tiles):
    bsz, s, d = x1.shape
    tile = lambda b, i: (b, i, 0)
    const2 = lambda b, i: (0, 0)
    full = lambda a: pl.BlockSpec(a.shape, const2)
    return pl.pallas_call(
        functools.partial(_ffn_kernel, d=d, fc=256),
        grid=(bsz, n_tiles),
        in_specs=[pl.BlockSpec((1, TM, d), tile), pl.BlockSpec((1, TM, d), tile),
                  pl.BlockSpec((1, 1, mods.shape[-1]), _tile_mod_index(n_lat_tiles, bsz)),
                  full(w1), full(w3), full(w2)],
        out_specs=pl.BlockSpec((1, TM, d), tile),
        out_shape=jax.ShapeDtypeStruct((bsz, s, d), F32),
        compiler_params=_cparams(("arbitrary", "arbitrary")),
        name="ffn",
    )(x1, h2, mods, w1, w3, w2)


DMA_UNROLL = 8


def _row_copy(src_ref, dst_ref, sem, src_row, dst_row):
    return pltpu.make_async_copy(src_ref.at[pl.ds(src_row, 1)], dst_ref.at[pl.ds(dst_row, 1)], sem)


def _dispatch_kernel(pos_ref, h_ref, xs_in_hbm, xs_hbm, sem):
    del xs_in_hbm
    n = h_ref.shape[1]

    def start(j, _):
        for kk in range(TOP_K):
            _row_copy(h_ref.at[0], xs_hbm, sem, j, pos_ref[0, 0, kk * n + j]).start(priority=kk % 2)
        return 0

    lax.fori_loop(0, n, start, 0, unroll=DMA_UNROLL)
    for kk in range(TOP_K):
        pltpu.make_async_copy(h_ref.at[0], xs_hbm.at[pl.ds(0, n)], sem).wait()


def _pad_fill_kernel(start_ref, len_ref, xs_hbm, zero_ref, sem):
    zero_ref[...] = jnp.zeros_like(zero_ref)
    n_experts = start_ref.shape[0]
    top = zero_ref.shape[0]

    def pieces(e, fn):
        off = start_ref[e]
        size = top
        while size >= SUBLANES:
            @pl.when((len_ref[e] & size) != 0)
            def _(off=off, size=size):
                dst = xs_hbm.at[pl.ds(pl.multiple_of(off, SUBLANES), size)]
                fn(pltpu.make_async_copy(zero_ref.at[pl.ds(0, size)], dst, sem))
            off = off + (len_ref[e] & size)
            size //= 2

    for e in range(n_experts):
        pieces(e, lambda cp: cp.start())
    for e in range(n_experts):
        pieces(e, lambda cp: cp.wait())


def _pad_fill(fill_start, fill_len, n_pad, d, dtype):
    return pl.pallas_call(
        _pad_fill_kernel,
        grid_spec=pltpu.PrefetchScalarGridSpec(
            num_scalar_prefetch=2, grid=(1,), in_specs=[],
            out_specs=pl.BlockSpec(memory_space=pl.ANY),
            scratch_shapes=[pltpu.VMEM((TMB, d), dtype), pltpu.SemaphoreType.DMA(())]),
        out_shape=jax.ShapeDtypeStruct((n_pad, d), dtype),
        compiler_params=_cparams(("arbitrary",)),
        name="moe_pad_fill",
    )(fill_start, fill_len)


def _dispatch(pos, h2, xs0, *, n_tiles):
    bsz, s, d = h2.shape
    n_pad = xs0.shape[0]
    return pl.pallas_call(
        _dispatch_kernel,
        grid=(bsz, n_tiles),
        in_specs=[pl.BlockSpec((1, 1, TOP_K * TM), lambda b, i: (b * n_tiles + i, 0, 0),
                               memory_space=pltpu.SMEM),
                  pl.BlockSpec((1, TM, d), lambda b, i: (b, i, 0)),
                  pl.BlockSpec(memory_space=pl.ANY)],
        out_specs=pl.BlockSpec(memory_space=pl.ANY),
        out_shape=jax.ShapeDtypeStruct((n_pad, d), h2.dtype),
        scratch_shapes=[pltpu.SemaphoreType.DMA(())],
        input_output_aliases={2: 0},
        compiler_params=_cparams(("arbitrary", "arbitrary")),
        name="moe_dispatch",
    )(pos, h2, xs0)


def _experts_kernel(be_ref, nu_ref, xs_ref, w1_ref, w3_ref, w2_ref, o_ref, xb_ref, acc_ref):
    r = pl.program_id(0)
    f = pl.program_id(1)
    nf = pl.num_programs(1)

    @pl.when(r < nu_ref[0])
    def _():
        @pl.when(f == 0)
        def _():
            xb_ref[...] = xs_ref[...].astype(BF16)
            acc_ref[...] = jnp.zeros_like(acc_ref)

        xb = xb_ref[...]
        a = jnp.dot(xb, w1_ref[0, 0].astype(BF16), preferred_element_type=F32)
        b = jnp.dot(xb, w3_ref[0, 0].astype(BF16), preferred_element_type=F32)
        act = (a * jax.nn.sigmoid(a) * b).astype(BF16)
        acc_ref[...] += jnp.dot(act, w2_ref[0, 0].astype(BF16), preferred_element_type=F32)

        @pl.when(f == nf - 1)
        def _():
            o_ref[...] = acc_ref[...]

    @pl.when(jnp.logical_and(r >= nu_ref[0], f == nf - 1))
    def _():
        o_ref[...] = jnp.zeros_like(o_ref)


def _experts(blk_e, n_used, xs, w1, w3, w2, layer):
    n_pad, d = xs.shape
    n_blk = n_pad // TMB
    d_ff = w1.shape[-1]
    nf = d_ff // TF

    def w_col(r, f, be, nu):
        live = r < nu[0]
        return (layer, be[r], 0, jnp.where(live, f, nf - 1))

    def w_row(r, f, be, nu):
        live = r < nu[0]
        return (layer, be[r], jnp.where(live, f, nf - 1), 0)

    grid_spec = pltpu.PrefetchScalarGridSpec(
        num_scalar_prefetch=2,
        grid=(n_blk, nf),
        in_specs=[pl.BlockSpec((TMB, d), lambda r, f, be, nu: (jnp.where(r < nu[0], r, 0), 0)),
                  pl.BlockSpec((1, 1, d, TF), w_col),
                  pl.BlockSpec((1, 1, d, TF), w_col),
                  pl.BlockSpec((1, 1, TF, d), w_row)],
        out_specs=pl.BlockSpec((TMB, d), lambda r, f, be, nu: (r, 0)),
        scratch_shapes=[pltpu.VMEM((TMB, d), BF16), pltpu.VMEM((TMB, d), F32)])
    return pl.pallas_call(
        _experts_kernel,
        grid_spec=grid_spec,
        out_shape=jax.ShapeDtypeStruct((n_pad, d), F32),
        compiler_params=_cparams(("arbitrary", "arbitrary")),
        name="moe_experts",
    )(blk_e, n_used, xs, w1, w3, w2)


def _combine_kernel(pos_ref, posn_ref, x1_ref, mod_ref, rg_ref, yp_hbm, *rest, d, final):
    if final:
        gf_ref, o_ref, buf_ref, sem = rest
    else:
        o_ref, buf_ref, sem = rest
    n = x1_ref.shape[1]
    t = pl.program_id(0) * pl.num_programs(1) + pl.program_id(1)
    n_steps = pl.num_programs(0) * pl.num_programs(1)
    slot = t % 2

    def gather(idx_ref, sl):
        def start(j, _):
            for kk in range(TOP_K):
                _row_copy(yp_hbm, buf_ref.at[sl, kk], sem.at[sl], idx_ref[0, 0, kk * n + j], j).start(
                    priority=kk % 2)
            return 0
        lax.fori_loop(0, n, start, 0, unroll=DMA_UNROLL)

    @pl.when(t == 0)
    def _():
        gather(pos_ref, slot)

    @pl.when(t + 1 < n_steps)
    def _():
        gather(posn_ref, 1 - slot)

    for kk in range(TOP_K):
        pltpu.make_async_copy(yp_hbm.at[pl.ds(0, n)], buf_ref.at[slot, kk], sem.at[slot]).wait()
    gate2 = mod_ref[0, :, 5 * d:6 * d]
    rg = rg_ref[0]
    y = rg[:, 0:1] * buf_ref[slot, 0]
    for kk in range(1, TOP_K):
        y = y + rg[:, kk:kk + 1] * buf_ref[slot, kk]
    x2 = x1_ref[0] + gate2 * y
    if final:
        x2 = _rms(x2, gf_ref[...])
    o_ref[0] = x2


def _combine(pos, x1, mods, route_g, yp, gf, *, n_lat_tiles, n_tiles):
    bsz, s, d = x1.shape
    final = gf is not None
    tile = lambda b, i: (b, i, 0)
    last_step = bsz * n_tiles - 1
    in_specs = [pl.BlockSpec((1, 1, TOP_K * TM), lambda b, i: (b * n_tiles + i, 0, 0),
                             memory_space=pltpu.SMEM),
                pl.BlockSpec((1, 1, TOP_K * TM),
                             lambda b, i: (jnp.minimum(b * n_tiles + i + 1, last_step), 0, 0),
                             memory_space=pltpu.SMEM),
                pl.BlockSpec((1, TM, d), tile),
                pl.BlockSpec((1, 1, mods.shape[-1]), _tile_mod_index(n_lat_tiles, bsz)),
                pl.BlockSpec((1, TM, LANES), tile),
                pl.BlockSpec(memory_space=pl.ANY)]
    args = [pos, pos, x1, mods, route_g, yp]
    if final:
        in_specs.append(pl.BlockSpec(gf.shape, lambda b, i: (0, 0)))
        args.append(gf)
    return pl.pallas_call(
        functools.partial(_combine_kernel, d=d, final=final),
        grid=(bsz, n_tiles),
        in_specs=in_specs,
        out_specs=pl.BlockSpec((1, TM, d), tile),
        out_shape=jax.ShapeDtypeStruct((bsz, n_tiles * TM, d), F32),
        scratch_shapes=[pltpu.VMEM((2, TOP_K, TM, d), F32), pltpu.SemaphoreType.DMA((2,))],
        compiler_params=_cparams(("arbitrary", "arbitrary")),
        name="moe_combine_final" if final else "moe_combine",
    )(*args)


def _route_tables(route_e, n_experts, s_eff):
    bsz = route_e.shape[0]
    s = s_eff
    n_assign = bsz * s * TOP_K
    lane = jnp.arange(route_e.shape[-1], dtype=jnp.int32)
    flat_e = jnp.stack([jnp.max(jnp.where(lane == k, route_e[:, :s], -1), axis=-1)
                        for k in range(TOP_K)]).reshape(n_assign)
    onehot = (flat_e[:, None] == jnp.arange(n_experts, dtype=jnp.int32)[None, :]).astype(jnp.int32)
    csum = jnp.cumsum(onehot, axis=0)
    rank = jnp.sum(csum * onehot, axis=1) - 1
    counts = csum[-1]
    padded = (counts + TMB - 1) // TMB * TMB
    pad_ends = jnp.cumsum(padded)
    pad_starts = pad_ends - padded
    dest = pad_starts[flat_e] + rank
    n_blk = (n_assign + TMB - 1) // TMB + n_experts
    n_pad = n_blk * TMB
    blk_e = jnp.minimum(
        jnp.searchsorted(pad_ends, jnp.arange(n_blk, dtype=jnp.int32) * TMB, side='right'),
        n_experts - 1).astype(jnp.int32)
    n_used = (pad_ends[-1] // TMB).astype(jnp.int32).reshape(1)
    fill_start = ((pad_starts + counts) // SUBLANES * SUBLANES).astype(jnp.int32)
    fill_len = (pad_ends - fill_start).astype(jnp.int32)
    n_tok_tiles = bsz * s // TM
    pos = dest.astype(jnp.int32).reshape(TOP_K, n_tok_tiles, TM).transpose(1, 0, 2)
    return blk_e, n_used, pos.reshape(n_tok_tiles, 1, TOP_K * TM), n_pad, fill_start, fill_len


def _final_kernel(x_ref, g_ref, o_ref):
    o_ref[0] = _rms(x_ref[0], g_ref[...])


def _final_norm(xt, g, *, n_tiles):
    bsz, s, d = xt.shape
    tile = lambda b, i: (b, i, 0)
    return pl.pallas_call(
        _final_kernel,
        grid=(bsz, n_tiles),
        in_specs=[pl.BlockSpec((1, TM, d), tile), pl.BlockSpec(g.shape, lambda b, i: (0, 0))],
        out_specs=pl.BlockSpec((1, TM, d), tile),
        out_shape=jax.ShapeDtypeStruct((bsz, n_tiles * TM, d), F32),
        compiler_params=_cparams(("arbitrary", "arbitrary")),
        name="final_norm",
    )(xt, g)


def _rope_partner(r):
    return jnp.concatenate([-r[..., 8:16], r[..., 0:8], -r[..., 24:32], r[..., 16:24]], axis=-1)


def _rope_tables(l_lat, s_total, d_rope, d_nope):
    t = jnp.arange(l_lat, dtype=jnp.int32)
    row = (t // GRID_W).astype(F32)
    col = (t % GRID_W).astype(F32)
    half = d_rope // 2
    inv_freq = ROPE_BASE ** (-jnp.arange(0, half, 2, dtype=F32) / half)
    ang_r = row[:, None] * inv_freq
    ang_c = col[:, None] * inv_freq
    cos = jnp.concatenate([jnp.cos(ang_r), jnp.cos(ang_r), jnp.cos(ang_c), jnp.cos(ang_c)], axis=1)
    sin = jnp.concatenate([jnp.sin(ang_r), jnp.sin(ang_r), jnp.sin(ang_c), jnp.sin(ang_c)], axis=1)
    n_ctx = s_total - l_lat
    cosk = jnp.concatenate([cos, jnp.ones((n_ctx, d_rope), F32)], axis=0)
    sink = jnp.concatenate([sin, jnp.zeros((n_ctx, d_rope), F32)], axis=0)
    pad = LANES - d_nope - d_rope
    cosq = jnp.concatenate([jnp.ones((s_total, d_nope), F32), cosk, jnp.zeros((s_total, pad), F32)], axis=1)
    sinq = jnp.concatenate([jnp.zeros((s_total, d_nope), F32), sink, jnp.zeros((s_total, pad), F32)], axis=1)
    return cosq, sinq, cosk, sink


def _block_diag(w):
    h, a, b = w.shape
    eye = jnp.eye(h, dtype=w.dtype)
    return (eye[:, None, :, None] * w[:, :, None, :]).reshape(h * a, h * b)


def kernel(x, c, ctx, c_ctx, w_mod, b_mod, norm1_g, norm2_g, w_in, w_out, sgu_ln_g, sgu_ln_b, sgu_w, sgu_b, conv_w, conv_b, lru_w_a, lru_b_a, lru_w_x, lru_b_x, lru_lam, mla_q_norm, mla_w_uq, mla_kv_norm, mla_w_uk, mla_w_uv, ffn_w1, ffn_w3, ffn_w2, moe_router, moe_w1, moe_w3, moe_w2, final_norm_g):
    bsz, l_lat, d = x.shape
    l_ctx = ctx.shape[1]
    s_total = l_lat + l_ctx
    depth = w_mod.shape[0]
    d_a = sgu_ln_g.shape[-1]
    h_a, chunk = sgu_w.shape[1], sgu_w.shape[2]
    d_b = conv_w.shape[-1]
    q_lora = mla_q_norm.shape[-1]
    kv_lora = mla_kv_norm.shape[-1]
    d_c = mla_w_uv.shape[-1]
    d_v = d_c // H_C
    d_nope = mla_w_uk.shape[-1] // H_C
    d_qk = mla_w_uq.shape[-1] // H_C
    d_rope = d_qk - d_nope
    n_experts = moe_router.shape[-1]
    assert l_lat % TM == 0 and l_ctx % TM == 0 and TM % chunk == 0 and TM == T_SCAN
    assert l_lat % TQ == 0 and l_lat % l_ctx == 0
    assert d_qk <= LANES and 2 * d_v == LANES and H_C % 2 == 0 and d_rope == 32
    n_lat_tiles = l_lat // TM
    n_ctx_tiles = l_ctx // TM
    dims = (d, d_a, d_b, q_lora, kv_lora, d_rope, h_a, chunk)

    xt = jnp.concatenate([x, ctx], axis=1)

    n_rows = (bsz + 1 + 7) // 8 * 8
    cond = jnp.zeros((n_rows, d), F32).at[:bsz].set(c).at[bsz].set(c_ctx)
    mods_all = _modulation(cond, w_mod.astype(BF16), b_mod[:, None, :])
    mods_all = mods_all[:, :bsz + 1, None, :]

    cosq, sinq, cosk, sink = _rope_tables(l_lat, s_total, d_rope, d_nope)
    q_scale = float(d_qk) ** -0.5 * 1.4426950408889634
    cosq, sinq = cosq * q_scale, sinq * q_scale
    head_pad = LANES - d_qk
    vone = jnp.tile((jnp.arange(LANES) == d_v).astype(F32)[None], (1, H_C))
    e_head = jnp.concatenate([jnp.zeros((d_rope, d_nope), F32), jnp.eye(d_rope, dtype=F32),
                              jnp.zeros((d_rope, head_pad), F32)], axis=1)
    emat = jnp.tile(e_head, (1, H_C)).astype(BF16)

    out = None
    for l in range(depth):
        last = l == depth - 1
        mods = mods_all[l]
        o_m = 2 * d_a + 2 * d_b
        o_r = o_m + q_lora + kv_lora
        w_rope = w_in[l][:, o_r:o_r + d_rope]
        n_in = (o_r + 2 * d_rope + LANES - 1) // LANES * LANES
        win = jnp.concatenate([w_in[l][:, :o_r + d_rope], _rope_partner(w_rope),
                               jnp.zeros((d, n_in - o_r - 2 * d_rope), F32)], axis=1).astype(BF16)
        wq = mla_w_uq[l].reshape(q_lora, H_C, d_qk)
        zq = jnp.zeros((q_lora, H_C, head_pad), F32)
        wuq = jnp.concatenate([wq, zq], axis=-1).reshape(q_lora, H_C * LANES).astype(BF16)
        wuqp = jnp.concatenate([jnp.zeros((q_lora, H_C, d_nope), F32), _rope_partner(wq[..., d_nope:]), zq],
                               axis=-1).reshape(q_lora, H_C * LANES).astype(BF16)
        wk = mla_w_uk[l].reshape(kv_lora, H_C, d_nope)
        wuk = jnp.concatenate([wk, jnp.zeros((kv_lora, H_C, LANES - d_nope), F32)],
                              axis=-1).reshape(kv_lora, H_C * LANES).astype(BF16)
        wv = mla_w_uv[l].reshape(kv_lora, H_C, d_v)
        wuv = jnp.concatenate([wv, jnp.zeros((kv_lora, H_C, LANES - d_v), F32)],
                              axis=-1).reshape(kv_lora, H_C * LANES).astype(BF16)
        bs_full = jnp.repeat(sgu_b[l].T, d_a // h_a, axis=1)

        ya, gg, xb, q, k, v = _premix(
            xt, mods, norm1_g[l][None], win, sgu_ln_g[l][None], sgu_ln_b[l][None],
            sgu_w[l].astype(BF16), bs_full, mla_q_norm[l][None], wuq, wuqp, mla_kv_norm[l][None],
            wuk, emat, wuv, vone, cosq, sinq, cosk, sink,
            n_lat_tiles=n_lat_tiles, dims=dims)

        wg = jnp.concatenate([_block_diag(lru_w_a[l, 0]), _block_diag(lru_w_x[l, 0]),
                              _block_diag(lru_w_a[l, 1]), _block_diag(lru_w_x[l, 1])], axis=1).astype(BF16)
        bg = jnp.concatenate([lru_b_a[l, 0], lru_b_x[l, 0], lru_b_a[l, 1], lru_b_x[l, 1]])[None]
        yb = _lru(xb, gg, conv_w[l], conv_b[l][None], wg, bg, lru_lam[l], n_lat=n_lat_tiles, n_ctx=n_ctx_tiles)

        n_tiles = n_lat_tiles if last else n_lat_tiles + n_ctx_tiles
        yc = _attention(q, k, v, l_lat=l_lat, d_v=d_v, ctx_queries=not last)

        if l % 2 == 0:
            i = l // 2
            x1, h2 = _postmix(xt, mods, ya, yb, yc, w_out[l].astype(BF16), norm2_g[l][None], None,
                              n_lat_tiles=n_lat_tiles, n_tiles=n_tiles, n_experts=n_experts)
            xt = _ffn(x1, h2, mods, ffn_w1[i].astype(BF16), ffn_w3[i].astype(BF16), ffn_w2[i].astype(BF16),
                      n_lat_tiles=n_lat_tiles, n_tiles=n_tiles)
            if last:
                out = _final_norm(xt, final_norm_g[None], n_tiles=n_lat_tiles)
        else:
            i = l // 2
            wr = jnp.concatenate([moe_router[i], jnp.zeros((d, LANES - n_experts), F32)], axis=1)
            x1, h2, route_e, route_g = _postmix(xt, mods, ya, yb, yc, w_out[l].astype(BF16),
                                                norm2_g[l][None], wr,
                                                n_lat_tiles=n_lat_tiles, n_tiles=n_tiles, n_experts=n_experts)
            blk_e, n_used, pos, n_pad, fill_start, fill_len = _route_tables(route_e, n_experts, n_tiles * TM)
            xs = _dispatch(pos, h2, _pad_fill(fill_start, fill_len, n_pad, d, h2.dtype), n_tiles=n_tiles)
            yp = _experts(blk_e, n_used, xs, moe_w1, moe_w3, moe_w2, i)
            out_or_xt = _combine(pos, x1, mods, route_g, yp, final_norm_g[None] if last else None,
                                 n_lat_tiles=n_lat_tiles, n_tiles=n_tiles)
            if last:
                out = out_or_xt
            else:
                xt = out_or_xt
    return out
```

```python
import functools

import jax
import jax.numpy as jnp
from jax import lax
from jax.experimental import pallas as pl
from jax.experimental.pallas import tpu as pltpu

F32 = jnp.float32
BF16 = jnp.bfloat16

EPS = 1e-6
GRID_W = 64
ROPE_BASE = 10000.0
LRU_C = 8.0
H_C = 8
TOP_K = 2
LANES = 128
TM = 256
T_SCAN = 256
KC = 2048
NH = 4
TQ = 512
TMB = 1024
TF = 512
VMEM_LIMIT = 56 * 1024 * 1024


def _cparams(sem):
    return pltpu.CompilerParams(dimension_semantics=sem, vmem_limit_bytes=VMEM_LIMIT)


def _rms(x, g):
    return x * lax.rsqrt(jnp.mean(x * x, axis=-1, keepdims=True) + EPS) * g


def _mod_kernel(c_ref, w_ref, b_ref, o_ref):
    c = c_ref[...]
    a = (c * jax.nn.sigmoid(c)).astype(BF16)
    o_ref[0] = jnp.dot(a, w_ref[0], preferred_element_type=F32) + b_ref[0]


def _modulation(cond, w_mod, b_mod):
    depth, d, n = w_mod.shape
    r = cond.shape[0]
    tn = 1024
    return pl.pallas_call(
        _mod_kernel,
        grid=(depth, n // tn),
        in_specs=[pl.BlockSpec((r, d), lambda l, j: (0, 0)),
                  pl.BlockSpec((1, d, tn), lambda l, j: (l, 0, j)),
                  pl.BlockSpec((1, 1, tn), lambda l, j: (l, 0, j))],
        out_specs=pl.BlockSpec((1, r, tn), lambda l, j: (l, 0, j)),
        out_shape=jax.ShapeDtypeStruct((depth, r, n), F32),
        compiler_params=_cparams(("arbitrary", "arbitrary")),
        name="modulation",
    )(cond, w_mod, b_mod)


def _premix_kernel(x_ref, mod_ref, g_ref, win_ref, lng_ref, lnb_ref, ws_ref, bs_ref,
                   qn_ref, wuq_ref, wuqp_ref, kvn_ref, wuk_ref, e_ref, wuv_ref, vone_ref,
                   cq_ref, sq_ref, ck_ref, sk_ref,
                   ya_ref, gg_ref, xb_ref, q_ref, k_ref, v_ref, *, dims):
    d, d_a, d_b, q_lora, kv_lora, d_rope, h_a, chunk = dims
    x = x_ref[0]
    shift = mod_ref[0, :, 0:d]
    scale = mod_ref[0, :, d:2 * d]
    h = (_rms(x, g_ref[...]) * (1.0 + scale) + shift).astype(BF16)
    z = jnp.dot(h, win_ref[...], preferred_element_type=F32)

    o = 0
    u = jax.nn.gelu(z[:, o:o + d_a])
    v = jax.nn.gelu(z[:, o + d_a:o + 2 * d_a])
    mu = jnp.mean(v, axis=-1, keepdims=True)
    vc = v - mu
    var = jnp.mean(vc * vc, axis=-1, keepdims=True)
    vn = (vc * lax.rsqrt(var + EPS) * lng_ref[...] + lnb_ref[...]).astype(BF16)
    dh_a = d_a // h_a
    tm = x.shape[0]
    head_of_lane = lax.broadcasted_iota(jnp.int32, (chunk, d_a), 1) // dh_a
    for c in range(tm // chunk):
        vch = vn[c * chunk:(c + 1) * chunk]
        s = jnp.dot(ws_ref[0], vch, preferred_element_type=F32)
        for hd in range(1, h_a):
            s = jnp.where(head_of_lane == hd,
                          jnp.dot(ws_ref[hd], vch, preferred_element_type=F32), s)
        s = s + bs_ref[...]
        ya_ref[0, c * chunk:(c + 1) * chunk, :] = (u[c * chunk:(c + 1) * chunk] * s).astype(BF16)

    o = 2 * d_a
    gg_ref[0] = jax.nn.gelu(z[:, o:o + d_b])
    xb_ref[0] = z[:, o + d_b:o + 2 * d_b]

    o = 2 * d_a + 2 * d_b
    cq = _rms(z[:, o:o + q_lora], qn_ref[...]).astype(BF16)
    qa = jnp.dot(cq, wuq_ref[...], preferred_element_type=F32)
    qb = jnp.dot(cq, wuqp_ref[...], preferred_element_type=F32)
    cos_q = jnp.concatenate([cq_ref[...]] * H_C, axis=1)
    sin_q = jnp.concatenate([sq_ref[...]] * H_C, axis=1)
    q_ref[0] = (qa * cos_q + qb * sin_q).astype(BF16)
    o += q_lora
    ckv = _rms(z[:, o:o + kv_lora], kvn_ref[...]).astype(BF16)
    o += kv_lora
    zr = z[:, o:o + d_rope]
    zrp = z[:, o + d_rope:o + 2 * d_rope]
    kr = (zr * ck_ref[...] + zrp * sk_ref[...]).astype(BF16)
    kn = jnp.dot(ckv, wuk_ref[...], preferred_element_type=F32)
    k_ref[0] = (kn + jnp.dot(kr, e_ref[...], preferred_element_type=F32)).astype(BF16)
    v_ref[0] = (jnp.dot(ckv, wuv_ref[...], preferred_element_type=F32) + vone_ref[...]).astype(BF16)


def _tile_mod_index(n_lat_tiles, n_batch):
    def index(b, i):
        return (jnp.where(i < n_lat_tiles, b, n_batch), 0, 0)
    return index


def _premix(xt, mods, g1, win, lng, lnb, ws, bs, qn, wuq, wuqp, kvn, wuk, emat, wuv, vone,
            cosq, sinq, cosk, sink, *, n_lat_tiles, dims):
    bsz, s, d = xt.shape
    d_a, d_b = dims[1], dims[2]
    hp = H_C * LANES
    const2 = lambda b, i: (0, 0)
    const3 = lambda b, i: (0, 0, 0)
    tile = lambda b, i: (b, i, 0)
    full = lambda a: pl.BlockSpec(a.shape, const2 if a.ndim == 2 else const3)
    return pl.pallas_call(
        functools.partial(_premix_kernel, dims=dims),
        grid=(bsz, s // TM),
        in_specs=[pl.BlockSpec((1, TM, d), tile),
                  pl.BlockSpec((1, 1, mods.shape[-1]), _tile_mod_index(n_lat_tiles, bsz)),
                  full(g1), full(win), full(lng), full(lnb), full(ws), full(bs),
                  full(qn), full(wuq), full(wuqp), full(kvn), full(wuk), full(emat), full(wuv), full(vone),
                  pl.BlockSpec((TM, LANES), lambda b, i: (i, 0)),
                  pl.BlockSpec((TM, LANES), lambda b, i: (i, 0)),
                  pl.BlockSpec((TM, cosk.shape[1]), lambda b, i: (i, 0)),
                  pl.BlockSpec((TM, sink.shape[1]), lambda b, i: (i, 0))],
        out_specs=[pl.BlockSpec((1, TM, d_a), tile), pl.BlockSpec((1, TM, d_b), tile),
                   pl.BlockSpec((1, TM, d_b), tile), pl.BlockSpec((1, TM, hp), tile),
                   pl.BlockSpec((1, TM, hp), tile), pl.BlockSpec((1, TM, hp), tile)],
        out_shape=[jax.ShapeDtypeStruct((bsz, s, d_a), BF16),
                   jax.ShapeDtypeStruct((bsz, s, d_b), F32),
                   jax.ShapeDtypeStruct((bsz, s, d_b), F32),
                   jax.ShapeDtypeStruct((bsz, s, hp), BF16),
                   jax.ShapeDtypeStruct((bsz, s, hp), BF16),
                   jax.ShapeDtypeStruct((bsz, s, hp), BF16)],
        compiler_params=_cparams(("arbitrary", "arbitrary")),
        name="premix",
    )(xt, mods, g1, win, lng, lnb, ws, bs, qn, wuq, wuqp, kvn, wuk, emat, wuv, vone,
      cosq, sinq, cosk, sink)


SUBLANES = 8


def _scan_tile(a, b, carry, reverse):
    t, c = a.shape
    g = SUBLANES
    n_groups = t // g
    a = a.reshape(n_groups, g, c)
    b = b.reshape(n_groups, g, c)
    rows = lax.broadcasted_iota(jnp.int32, a.shape, 1)
    s = 1
    while s < g:
        shift = g - s if reverse else s
        ok = rows < g - s if reverse else rows >= s
        a_sh = pltpu.roll(a, shift, 1)
        b_sh = pltpu.roll(b, shift, 1)
        b = b + a * jnp.where(ok, b_sh, 0.0)
        a = a * jnp.where(ok, a_sh, 1.0)
        s *= 2
    hs = [None] * n_groups
    for j in (range(n_groups - 1, -1, -1) if reverse else range(n_groups)):
        hj = b[j] + a[j] * carry
        hs[j] = hj
        carry = hj[0:1] if reverse else hj[g - 1:g]
    return jnp.concatenate(hs, axis=0), carry


def _lru_kernel(xb_ref, gg_ref, cw_ref, cb_ref, wg_ref, bg_ref, lam_ref, out_ref, xc_ref, hf_ref,
                *, n_lat, n_ctx):
    t = T_SCAN
    n_tiles = n_lat + n_ctx
    s_total = n_tiles * t
    d_b = xb_ref.shape[-1]
    w = cw_ref[...]
    cb = cb_ref[...]

    def conv_body(j, _):
        t0 = pl.multiple_of(j * t, t)
        is_ctx = j >= n_lat
        seq_lo = jnp.where(is_ctx, n_lat * t, 0)
        seq_hi = jnp.where(is_ctx, s_total, n_lat * t)
        cur = xb_ref[0, pl.ds(t0, t), :]
        p0 = pl.multiple_of(jnp.maximum(t0 - 8, 0), 8)
        n0 = pl.multiple_of(jnp.minimum(t0 + t, s_total - 8), 8)
        prev = jnp.where(t0 > seq_lo, xb_ref[0, pl.ds(p0, 8), :], 0.0)
        nxt = jnp.where(t0 + t < seq_hi, xb_ref[0, pl.ds(n0, 8), :], 0.0)
        ext = jnp.concatenate([prev, cur, nxt], axis=0)
        n_ext = t + 16
        xm2 = pltpu.roll(ext, 2, 0)[8:8 + t]
        xm1 = pltpu.roll(ext, 1, 0)[8:8 + t]
        xp1 = pltpu.roll(ext, n_ext - 1, 0)[8:8 + t]
        xc_ref[pl.ds(t0, t), :] = (w[0:1] * xm2 + w[1:2] * xm1 + w[2:3] * cur + w[3:4] * xp1 + cb)
        return 0

    lax.fori_loop(0, n_tiles, conv_body, 0)

    lam = lam_ref[...]
    neg = -lam
    softplus = jnp.maximum(neg, 0.0) + jnp.log1p(jnp.exp(-jnp.abs(neg)))

    def direction(dr, reverse):
        sp = softplus[dr:dr + 1]
        wg = wg_ref[:, dr * 2 * d_b:(dr + 1) * 2 * d_b]
        bg = bg_ref[:, dr * 2 * d_b:(dr + 1) * 2 * d_b]

        def body(j, carry):
            if reverse:
                idx = jnp.where(j < n_ctx, n_tiles - 1 - j, n_lat - 1 - (j - n_ctx))
            else:
                idx = jnp.where(j < n_ctx, n_lat + j, j - n_ctx)
            t0 = pl.multiple_of(idx * t, t)
            xc = xc_ref[pl.ds(t0, t), :]
            g = jnp.dot(xc.astype(BF16), wg, preferred_element_type=F32) + bg
            r = jax.nn.sigmoid(g[:, 0:d_b])
            ig = jax.nn.sigmoid(g[:, d_b:2 * d_b])
            log_a = (-LRU_C * r) * sp
            a = jnp.exp(log_a)
            bv = jnp.sqrt(-jnp.tanh(log_a) * (a * a + 1.0)) * (ig * xc)
            h, carry = _scan_tile(a, bv, carry, reverse)
            if reverse:
                y = gg_ref[0, pl.ds(t0, t), :] * (hf_ref[pl.ds(t0, t), :] + h)
                out_ref[0, pl.ds(t0, t), :] = y.astype(out_ref.dtype)
            else:
                hf_ref[pl.ds(t0, t), :] = h
            return carry

        lax.fori_loop(0, n_tiles, body, jnp.zeros((1, d_b), F32))

    direction(0, False)
    direction(1, True)


def _lru(xb, gg, cw, cb, wg, bg, lam, *, n_lat, n_ctx):
    bsz, s, d_b = xb.shape
    const2 = lambda b: (0, 0)
    full = lambda a: pl.BlockSpec(a.shape, const2)
    seq = pl.BlockSpec((1, s, d_b), lambda b: (b, 0, 0))
    return pl.pallas_call(
        functools.partial(_lru_kernel, n_lat=n_lat, n_ctx=n_ctx),
        grid=(bsz,),
        in_specs=[seq, seq, full(cw), full(cb), full(wg), full(bg), full(lam)],
        out_specs=seq,
        out_shape=jax.ShapeDtypeStruct((bsz, s, d_b), BF16),
        scratch_shapes=[pltpu.VMEM((s, d_b), F32), pltpu.VMEM((s, d_b), F32)],
        compiler_params=_cparams(("arbitrary",)),
        name="rglru",
    )(xb, gg, cw, cb, wg, bg, lam)


def _attn_kernel(q_ref, k_ref, v_ref, *rest, chunks, d_v):
    o_ref = rest[-1]
    tq = q_ref.shape[1]
    units = [(hh, lo, n) for hh in range(NH) for lo, n in chunks]

    def scores(hh, lo, n):
        qh = q_ref[0, :, hh * LANES:(hh + 1) * LANES]
        kh = k_ref[0, lo:lo + n, hh * LANES:(hh + 1) * LANES]
        return lax.dot_general(qh, kh, (((1,), (1,)), ((), ())), preferred_element_type=F32)

    outs = []
    m = acc = None
    s_next = scores(*units[0])
    for ui, (hh, lo, n) in enumerate(units):
        s = s_next
        if ui + 1 < len(units):
            s_next = scores(*units[ui + 1])
        if lo == chunks[0][0]:
            m = acc = None
        mc = jnp.max(s, axis=-1, keepdims=True)
        m_new = mc if m is None else jnp.maximum(m, mc)
        p = jnp.exp2(s - m_new).astype(BF16)
        pv = jnp.dot(p, v_ref[0, lo:lo + n, hh * LANES:(hh + 1) * LANES],
                     preferred_element_type=F32)
        acc = pv if m is None else jnp.exp2(m - m_new) * acc + pv
        m = m_new
        if lo == chunks[-1][0]:
            outs.append(acc / acc[:, d_v:d_v + 1])
    lane = lax.broadcasted_iota(jnp.int32, (tq, LANES), 1)
    for pp in range(NH // 2):
        pair = jnp.where(lane < d_v, outs[2 * pp], pltpu.roll(outs[2 * pp + 1], d_v, 1))
        o_ref[0, :, pp * LANES:(pp + 1) * LANES] = pair.astype(o_ref.dtype)


def _key_chunks(lo, hi, size):
    return tuple((a, min(size, hi - a)) for a in range(lo, hi, size))


def _attention(q, k, v, *, l_lat, d_v, ctx_queries):
    bsz, s, hp = q.shape
    l_ctx = s - l_lat
    d_c = H_C * d_v
    sem = ("arbitrary", "arbitrary", "arbitrary")
    out_shape = jax.ShapeDtypeStruct((bsz, s, d_c), BF16)
    y = pl.pallas_call(
        functools.partial(_attn_kernel, chunks=_key_chunks(0, s, KC), d_v=d_v),
        grid=(bsz, H_C // NH, l_lat // TQ),
        in_specs=[pl.BlockSpec((1, TQ, NH * LANES), lambda b, h, i: (b, i, h)),
                  pl.BlockSpec((1, s, NH * LANES), lambda b, h, i: (b, 0, h)),
                  pl.BlockSpec((1, s, NH * LANES), lambda b, h, i: (b, 0, h))],
        out_specs=pl.BlockSpec((1, TQ, NH * d_v), lambda b, h, i: (b, i, h)),
        out_shape=out_shape,
        compiler_params=_cparams(sem),
        name="attention",
    )(q, k, v)
    if not ctx_queries:
        return y
    first = l_lat // l_ctx
    return pl.pallas_call(
        functools.partial(_attn_kernel, chunks=_key_chunks(0, l_ctx, KC), d_v=d_v),
        grid=(bsz, H_C // NH, 1),
        in_specs=[pl.BlockSpec((1, l_ctx, NH * LANES), lambda b, h, i: (b, first, h)),
                  pl.BlockSpec((1, l_ctx, NH * LANES), lambda b, h, i: (b, first, h)),
                  pl.BlockSpec((1, l_ctx, NH * LANES), lambda b, h, i: (b, first, h)),
                  pl.BlockSpec(memory_space=pl.ANY)],
        out_specs=pl.BlockSpec((1, l_ctx, NH * d_v), lambda b, h, i: (b, first, h)),
        out_shape=out_shape,
        input_output_aliases={3: 0},
        compiler_params=_cparams(sem),
        name="attention_ctx",
    )(q, k, v, y)


def _postmix_kernel(x_ref, mod_ref, ya_ref, yb_ref, yc_ref, wout_ref, g2_ref, *rest,
                    d, n_experts, route):
    if route:
        wr_ref, x1_ref, h2_ref, re_ref, rg_ref = rest
    else:
        x1_ref, h2_ref = rest
    gate1 = mod_ref[0, :, 2 * d:3 * d]
    shift2 = mod_ref[0, :, 3 * d:4 * d]
    scale2 = mod_ref[0, :, 4 * d:5 * d]
    y = jnp.concatenate([ya_ref[0], yb_ref[0], yc_ref[0]], axis=1)
    x1 = x_ref[0] + gate1 * jnp.dot(y, wout_ref[...], preferred_element_type=F32)
    x1_ref[0] = x1
    h2 = _rms(x1, g2_ref[...]) * (1.0 + scale2) + shift2
    h2_ref[0] = h2.astype(h2_ref.dtype)
    if route:
        logits = jnp.dot(h2, wr_ref[...], preferred_element_type=F32)
        lane = lax.broadcasted_iota(jnp.int32, logits.shape, 1)
        neg_inf = jnp.float32(-jnp.inf)
        lg = jnp.where(lane < n_experts, logits, neg_inf)
        m1 = jnp.max(lg, axis=-1, keepdims=True)
        i1 = jnp.min(jnp.where(lg == m1, lane, LANES), axis=-1, keepdims=True)
        lg2 = jnp.where(lane == i1, neg_inf, lg)
        m2 = jnp.max(lg2, axis=-1, keepdims=True)
        i2 = jnp.min(jnp.where(lg2 == m2, lane, LANES), axis=-1, keepdims=True)
        e = jnp.exp(m2 - m1)
        den = 1.0 + e
        re_ref[0] = jnp.where(lane == 0, i1, i2)
        rg_ref[0] = jnp.where(lane == 0, 1.0 / den, e / den)


def _postmix(xt, mods, ya, yb, yc, wout, g2, wr, *, n_lat_tiles, n_tiles, n_experts):
    bsz, s, d = xt.shape
    route = wr is not None
    tile = lambda b, i: (b, i, 0)
    const2 = lambda b, i: (0, 0)
    full = lambda a: pl.BlockSpec(a.shape, const2)
    in_specs = [pl.BlockSpec((1, TM, d), tile),
                pl.BlockSpec((1, 1, mods.shape[-1]), _tile_mod_index(n_lat_tiles, bsz)),
                pl.BlockSpec((1, TM, ya.shape[-1]), tile), pl.BlockSpec((1, TM, yb.shape[-1]), tile),
                pl.BlockSpec((1, TM, yc.shape[-1]), tile), full(wout), full(g2)]
    args = [xt, mods, ya, yb, yc, wout, g2]
    out_specs = [pl.BlockSpec((1, TM, d), tile), pl.BlockSpec((1, TM, d), tile)]
    out_shape = [jax.ShapeDtypeStruct((bsz, s, d), F32),
                 jax.ShapeDtypeStruct((bsz, s, d), F32 if route else BF16)]
    if route:
        in_specs.append(full(wr))
        args.append(wr)
        out_specs += [pl.BlockSpec((1, TM, LANES), tile), pl.BlockSpec((1, TM, LANES), tile)]
        out_shape += [jax.ShapeDtypeStruct((bsz, s, LANES), jnp.int32),
                      jax.ShapeDtypeStruct((bsz, s, LANES), F32)]
    return pl.pallas_call(
        functools.partial(_postmix_kernel, d=d, n_experts=n_experts, route=route),
        grid=(bsz, n_tiles),
        in_specs=in_specs, out_specs=out_specs, out_shape=out_shape,
        compiler_params=_cparams(("arbitrary", "arbitrary")),
        name="postmix_route" if route else "postmix",
    )(*args)


def _ffn_kernel(x1_ref, h2_ref, mod_ref, w1_ref, w3_ref, w2_ref, o_ref, *, d, fc):
    h = h2_ref[0]
    d_ff = w1_ref.shape[1]
    n_chunks = d_ff // fc

    def up(c):
        return (jnp.dot(h, w1_ref[:, c * fc:(c + 1) * fc], preferred_element_type=F32),
                jnp.dot(h, w3_ref[:, c * fc:(c + 1) * fc], preferred_element_type=F32))

    acc = jnp.zeros((h.shape[0], d), F32)
    a, b = up(0)
    for c in range(n_chunks):
        ac, bc = a, b
        if c + 1 < n_chunks:
            a, b = up(c + 1)
        act = (ac * jax.nn.sigmoid(ac) * bc).astype(BF16)
        acc = acc + jnp.dot(act, w2_ref[c * fc:(c + 1) * fc, :], preferred_element_type=F32)
    gate2 = mod_ref[0, :, 5 * d:6 * d]
    o_ref[0] = x1_ref[0] + gate2 * acc


def _ffn(x1, h2, mods, w1, w3, w2, *, n_lat_tiles, n_tiles):
    bsz, s, d = x1.shape
    tile = lambda b, i: (b, i, 0)
    const2 = lambda b, i: (0, 0)
    full = lambda a: pl.BlockSpec(a.shape, const2)
    return pl.pallas_call(
        functools.partial(_ffn_kernel, d=d, fc=256),
        grid=(bsz, n_tiles),
        in_specs=[pl.BlockSpec((1, TM, d), tile), pl.BlockSpec((1, TM, d), tile),
                  pl.BlockSpec((1, 1, mods.shape[-1]), _tile_mod_index(n_lat_tiles, bsz)),
                  full(w1), full(w3), full(w2)],
        out_specs=pl.BlockSpec((1, TM, d), tile),
        out_shape=jax.ShapeDtypeStruct((bsz, s, d), F32),
        compiler_params=_cparams(("arbitrary", "arbitrary")),
        name="ffn",
    )(x1, h2, mods, w1, w3, w2)


DMA_UNROLL = 8


def _row_copy(src_ref, dst_ref, sem, src_row, dst_row):
    return pltpu.make_async_copy(src_ref.at[pl.ds(src_row, 1)], dst_ref.at[pl.ds(dst_row, 1)], sem)


def _dispatch_kernel(pos_ref, h_ref, xs_in_hbm, xs_hbm, stage_ref, sem, *, n_steps):
    del xs_in_hbm
    n = h_ref.shape[1]
    t = pl.program_id(0) * pl.num_programs(1) + pl.program_id(1)
    slot = t % 2

    def wait_slot(sl):
        for kk in range(TOP_K):
            pltpu.make_async_copy(stage_ref.at[sl], xs_hbm.at[pl.ds(0, n)], sem.at[sl]).wait()

    @pl.when(t >= 2)
    def _():
        wait_slot(slot)

    stage_ref[slot] = h_ref[0]

    def start(j, _):
        for kk in range(TOP_K):
            _row_copy(stage_ref.at[slot], xs_hbm, sem.at[slot], j, pos_ref[0, 0, kk * n + j]).start(
                priority=kk % 2)
        return 0

    lax.fori_loop(0, n, start, 0, unroll=DMA_UNROLL)

    @pl.when(t == n_steps - 1)
    def _():
        wait_slot(slot)
        if n_steps > 1:
            wait_slot(1 - slot)


def _pad_fill_kernel(start_ref, len_ref, xs_hbm, zero_ref, sem):
    zero_ref[...] = jnp.zeros_like(zero_ref)
    n_experts = start_ref.shape[0]
    top = zero_ref.shape[0]

    def pieces(e, fn):
        off = start_ref[e]
        size = top
        while size >= SUBLANES:
            @pl.when((len_ref[e] & size) != 0)
            def _(off=off, size=size):
                dst = xs_hbm.at[pl.ds(pl.multiple_of(off, SUBLANES), size)]
                fn(pltpu.make_async_copy(zero_ref.at[pl.ds(0, size)], dst, sem))
            off = off + (len_ref[e] & size)
            size //= 2

    for e in range(n_experts):
        pieces(e, lambda cp: cp.start())
    for e in range(n_experts):
        pieces(e, lambda cp: cp.wait())


def _pad_fill(fill_start, fill_len, n_pad, d, dtype):
    return pl.pallas_call(
        _pad_fill_kernel,
        grid_spec=pltpu.PrefetchScalarGridSpec(
            num_scalar_prefetch=2, grid=(1,), in_specs=[],
            out_specs=pl.BlockSpec(memory_space=pl.ANY),
            scratch_shapes=[pltpu.VMEM((TMB, d), dtype), pltpu.SemaphoreType.DMA(())]),
        out_shape=jax.ShapeDtypeStruct((n_pad, d), dtype),
        compiler_params=_cparams(("arbitrary",)),
        name="moe_pad_fill",
    )(fill_start, fill_len)


def _dispatch(pos, h2, xs0, *, n_tiles):
    bsz, s, d = h2.shape
    n_pad = xs0.shape[0]
    return pl.pallas_call(
        functools.partial(_dispatch_kernel, n_steps=bsz * n_tiles),
        grid=(bsz, n_tiles),
        in_specs=[pl.BlockSpec((1, 1, TOP_K * TM), lambda b, i: (b * n_tiles + i, 0, 0),
                               memory_space=pltpu.SMEM),
                  pl.BlockSpec((1, TM, d), lambda b, i: (b, i, 0)),
                  pl.BlockSpec(memory_space=pl.ANY)],
        out_specs=pl.BlockSpec(memory_space=pl.ANY),
        out_shape=jax.ShapeDtypeStruct((n_pad, d), h2.dtype),
        scratch_shapes=[pltpu.VMEM((2, TM, d), h2.dtype), pltpu.SemaphoreType.DMA((2,))],
        input_output_aliases={2: 0},
        compiler_params=_cparams(("arbitrary", "arbitrary")),
        name="moe_dispatch",
    )(pos, h2, xs0)


def _experts_kernel(be_ref, nu_ref, xs_ref, w1_ref, w3_ref, w2_ref, o_ref, xb_ref, acc_ref):
    r = pl.program_id(0)
    f = pl.program_id(1)
    nf = pl.num_programs(1)

    @pl.when(r < nu_ref[0])
    def _():
        @pl.when(f == 0)
        def _():
            xb_ref[...] = xs_ref[...].astype(BF16)
            acc_ref[...] = jnp.zeros_like(acc_ref)

        xb = xb_ref[...]
        a = jnp.dot(xb, w1_ref[0, 0].astype(BF16), preferred_element_type=F32)
        b = jnp.dot(xb, w3_ref[0, 0].astype(BF16), preferred_element_type=F32)
        act = (a * jax.nn.sigmoid(a) * b).astype(BF16)
        acc_ref[...] += jnp.dot(act, w2_ref[0, 0].astype(BF16), preferred_element_type=F32)

        @pl.when(f == nf - 1)
        def _():
            o_ref[...] = acc_ref[...]

    @pl.when(jnp.logical_and(r >= nu_ref[0], f == nf - 1))
    def _():
        o_ref[...] = jnp.zeros_like(o_ref)


def _experts(blk_e, n_used, xs, w1, w3, w2, layer):
    n_pad, d = xs.shape
    n_blk = n_pad // TMB
    d_ff = w1.shape[-1]
    nf = d_ff // TF

    def w_col(r, f, be, nu):
        live = r < nu[0]
        return (layer, be[r], 0, jnp.where(live, f, nf - 1))

    def w_row(r, f, be, nu):
        live = r < nu[0]
        return (layer, be[r], jnp.where(live, f, nf - 1), 0)

    grid_spec = pltpu.PrefetchScalarGridSpec(
        num_scalar_prefetch=2,
        grid=(n_blk, nf),
        in_specs=[pl.BlockSpec((TMB, d), lambda r, f, be, nu: (jnp.where(r < nu[0], r, 0), 0)),
                  pl.BlockSpec((1, 1, d, TF), w_col),
                  pl.BlockSpec((1, 1, d, TF), w_col),
                  pl.BlockSpec((1, 1, TF, d), w_row)],
        out_specs=pl.BlockSpec((TMB, d), lambda r, f, be, nu: (r, 0)),
        scratch_shapes=[pltpu.VMEM((TMB, d), BF16), pltpu.VMEM((TMB, d), F32)])
    return pl.pallas_call(
        _experts_kernel,
        grid_spec=grid_spec,
        out_shape=jax.ShapeDtypeStruct((n_pad, d), F32),
        compiler_params=_cparams(("arbitrary", "arbitrary")),
        name="moe_experts",
    )(blk_e, n_used, xs, w1, w3, w2)


def _combine_kernel(pos_ref, posn_ref, x1_ref, mod_ref, rg_ref, yp_hbm, *rest, d, final):
    if final:
        gf_ref, o_ref, buf_ref, sem = rest
    else:
        o_ref, buf_ref, sem = rest
    n = x1_ref.shape[1]
    t = pl.program_id(0) * pl.num_programs(1) + pl.program_id(1)
    n_steps = pl.num_programs(0) * pl.num_programs(1)
    slot = t % 2

    def gather(idx_ref, sl):
        def start(j, _):
            for kk in range(TOP_K):
                _row_copy(yp_hbm, buf_ref.at[sl, kk], sem.at[sl], idx_ref[0, 0, kk * n + j], j).start(
                    priority=kk % 2)
            return 0
        lax.fori_loop(0, n, start, 0, unroll=DMA_UNROLL)

    @pl.when(t == 0)
    def _():
        gather(pos_ref, slot)

    @pl.when(t + 1 < n_steps)
    def _():
        gather(posn_ref, 1 - slot)

    for kk in range(TOP_K):
        pltpu.make_async_copy(yp_hbm.at[pl.ds(0, n)], buf_ref.at[slot, kk], sem.at[slot]).wait()
    gate2 = mod_ref[0, :, 5 * d:6 * d]
    rg = rg_ref[0]
    y = rg[:, 0:1] * buf_ref[slot, 0]
    for kk in range(1, TOP_K):
        y = y + rg[:, kk:kk + 1] * buf_ref[slot, kk]
    x2 = x1_ref[0] + gate2 * y
    if final:
        x2 = _rms(x2, gf_ref[...])
    o_ref[0] = x2


def _combine(pos, x1, mods, route_g, yp, gf, *, n_lat_tiles, n_tiles):
    bsz, s, d = x1.shape
    final = gf is not None
    tile = lambda b, i: (b, i, 0)
    last_step = bsz * n_tiles - 1
    in_specs = [pl.BlockSpec((1, 1, TOP_K * TM), lambda b, i: (b * n_tiles + i, 0, 0),
                             memory_space=pltpu.SMEM),
                pl.BlockSpec((1, 1, TOP_K * TM),
                             lambda b, i: (jnp.minimum(b * n_tiles + i + 1, last_step), 0, 0),
                             memory_space=pltpu.SMEM),
                pl.BlockSpec((1, TM, d), tile),
                pl.BlockSpec((1, 1, mods.shape[-1]), _tile_mod_index(n_lat_tiles, bsz)),
                pl.BlockSpec((1, TM, LANES), tile),
                pl.BlockSpec(memory_space=pl.ANY)]
    args = [pos, pos, x1, mods, route_g, yp]
    if final:
        in_specs.append(pl.BlockSpec(gf.shape, lambda b, i: (0, 0)))
        args.append(gf)
    return pl.pallas_call(
        functools.partial(_combine_kernel, d=d, final=final),
        grid=(bsz, n_tiles),
        in_specs=in_specs,
        out_specs=pl.BlockSpec((1, TM, d), tile),
        out_shape=jax.ShapeDtypeStruct((bsz, n_tiles * TM, d), F32),
        scratch_shapes=[pltpu.VMEM((2, TOP_K, TM, d), F32), pltpu.SemaphoreType.DMA((2,))],
        compiler_params=_cparams(("arbitrary", "arbitrary")),
        name="moe_combine_final" if final else "moe_combine",
    )(*args)


def _route_tables(route_e, n_experts, s_eff):
    bsz = route_e.shape[0]
    s = s_eff
    n_assign = bsz * s * TOP_K
    lane = jnp.arange(route_e.shape[-1], dtype=jnp.int32)
    flat_e = jnp.stack([jnp.max(jnp.where(lane == k, route_e[:, :s], -1), axis=-1)
                        for k in range(TOP_K)]).reshape(n_assign)
    onehot = (flat_e[:, None] == jnp.arange(n_experts, dtype=jnp.int32)[None, :]).astype(jnp.int32)
    csum = jnp.cumsum(onehot, axis=0)
    rank = jnp.sum(csum * onehot, axis=1) - 1
    counts = csum[-1]
    padded = (counts + TMB - 1) // TMB * TMB
    pad_ends = jnp.cumsum(padded)
    pad_starts = pad_ends - padded
    dest = pad_starts[flat_e] + rank
    n_blk = (n_assign + TMB - 1) // TMB + n_experts
    n_pad = n_blk * TMB
    blk_e = jnp.minimum(
        jnp.searchsorted(pad_ends, jnp.arange(n_blk, dtype=jnp.int32) * TMB, side='right'),
        n_experts - 1).astype(jnp.int32)
    n_used = (pad_ends[-1] // TMB).astype(jnp.int32).reshape(1)
    fill_start = ((pad_starts + counts) // SUBLANES * SUBLANES).astype(jnp.int32)
    fill_len = (pad_ends - fill_start).astype(jnp.int32)
    n_tok_tiles = bsz * s // TM
    pos = dest.astype(jnp.int32).reshape(TOP_K, n_tok_tiles, TM).transpose(1, 0, 2)
    return blk_e, n_used, pos.reshape(n_tok_tiles, 1, TOP_K * TM), n_pad, fill_start, fill_len


def _final_kernel(x_ref, g_ref, o_ref):
    o_ref[0] = _rms(x_ref[0], g_ref[...])


def _final_norm(xt, g, *, n_tiles):
    bsz, s, d = xt.shape
    tile = lambda b, i: (b, i, 0)
    return pl.pallas_call(
        _final_kernel,
        grid=(bsz, n_tiles),
        in_specs=[pl.BlockSpec((1, TM, d), tile), pl.BlockSpec(g.shape, lambda b, i: (0, 0))],
        out_specs=pl.BlockSpec((1, TM, d), tile),
        out_shape=jax.ShapeDtypeStruct((bsz, n_tiles * TM, d), F32),
        compiler_params=_cparams(("arbitrary", "arbitrary")),
        name="final_norm",
    )(xt, g)


def _rope_partner(r):
    return jnp.concatenate([-r[..., 8:16], r[..., 0:8], -r[..., 24:32], r[..., 16:24]], axis=-1)


def _rope_tables(l_lat, s_total, d_rope, d_nope):
    t = jnp.arange(l_lat, dtype=jnp.int32)
    row = (t // GRID_W).astype(F32)
    col = (t % GRID_W).astype(F32)
    half = d_rope // 2
    inv_freq = ROPE_BASE ** (-jnp.arange(0, half, 2, dtype=F32) / half)
    ang_r = row[:, None] * inv_freq
    ang_c = col[:, None] * inv_freq
    cos = jnp.concatenate([jnp.cos(ang_r), jnp.cos(ang_r), jnp.cos(ang_c), jnp.cos(ang_c)], axis=1)
    sin = jnp.concatenate([jnp.sin(ang_r), jnp.sin(ang_r), jnp.sin(ang_c), jnp.sin(ang_c)], axis=1)
    n_ctx = s_total - l_lat
    cosk = jnp.concatenate([cos, jnp.ones((n_ctx, d_rope), F32)], axis=0)
    sink = jnp.concatenate([sin, jnp.zeros((n_ctx, d_rope), F32)], axis=0)
    pad = LANES - d_nope - d_rope
    cosq = jnp.concatenate([jnp.ones((s_total, d_nope), F32), cosk, jnp.zeros((s_total, pad), F32)], axis=1)
    sinq = jnp.concatenate([jnp.zeros((s_total, d_nope), F32), sink, jnp.zeros((s_total, pad), F32)], axis=1)
    return cosq, sinq, cosk, sink


def _block_diag(w):
    h, a, b = w.shape
    eye = jnp.eye(h, dtype=w.dtype)
    return (eye[:, None, :, None] * w[:, :, None, :]).reshape(h * a, h * b)


def kernel(x, c, ctx, c_ctx, w_mod, b_mod, norm1_g, norm2_g, w_in, w_out, sgu_ln_g, sgu_ln_b, sgu_w, sgu_b, conv_w, conv_b, lru_w_a, lru_b_a, lru_w_x, lru_b_x, lru_lam, mla_q_norm, mla_w_uq, mla_kv_norm, mla_w_uk, mla_w_uv, ffn_w1, ffn_w3, ffn_w2, moe_router, moe_w1, moe_w3, moe_w2, final_norm_g):
    bsz, l_lat, d = x.shape
    l_ctx = ctx.shape[1]
    s_total = l_lat + l_ctx
    depth = w_mod.shape[0]
    d_a = sgu_ln_g.shape[-1]
    h_a, chunk = sgu_w.shape[1], sgu_w.shape[2]
    d_b = conv_w.shape[-1]
    q_lora = mla_q_norm.shape[-1]
    kv_lora = mla_kv_norm.shape[-1]
    d_c = mla_w_uv.shape[-1]
    d_v = d_c // H_C
    d_nope = mla_w_uk.shape[-1] // H_C
    d_qk = mla_w_uq.shape[-1] // H_C
    d_rope = d_qk - d_nope
    n_experts = moe_router.shape[-1]
    assert l_lat % TM == 0 and l_ctx % TM == 0 and TM % chunk == 0 and TM == T_SCAN
    assert l_lat % TQ == 0 and l_lat % l_ctx == 0
    assert d_qk <= LANES and 2 * d_v == LANES and H_C % 2 == 0 and d_rope == 32
    n_lat_tiles = l_lat // TM
    n_ctx_tiles = l_ctx // TM
    dims = (d, d_a, d_b, q_lora, kv_lora, d_rope, h_a, chunk)

    xt = jnp.concatenate([x, ctx], axis=1)

    n_rows = (bsz + 1 + 7) // 8 * 8
    cond = jnp.zeros((n_rows, d), F32).at[:bsz].set(c).at[bsz].set(c_ctx)
    mods_all = _modulation(cond, w_mod.astype(BF16), b_mod[:, None, :])
    mods_all = mods_all[:, :bsz + 1, None, :]

    cosq, sinq, cosk, sink = _rope_tables(l_lat, s_total, d_rope, d_nope)
    q_scale = float(d_qk) ** -0.5 * 1.4426950408889634
    cosq, sinq = cosq * q_scale, sinq * q_scale
    head_pad = LANES - d_qk
    vone = jnp.tile((jnp.arange(LANES) == d_v).astype(F32)[None], (1, H_C))
    e_head = jnp.concatenate([jnp.zeros((d_rope, d_nope), F32), jnp.eye(d_rope, dtype=F32),
                              jnp.zeros((d_rope, head_pad), F32)], axis=1)
    emat = jnp.tile(e_head, (1, H_C)).astype(BF16)

    out = None
    for l in range(depth):
        last = l == depth - 1
        mods = mods_all[l]
        o_m = 2 * d_a + 2 * d_b
        o_r = o_m + q_lora + kv_lora
        w_rope = w_in[l][:, o_r:o_r + d_rope]
        n_in = (o_r + 2 * d_rope + LANES - 1) // LANES * LANES
        win = jnp.concatenate([w_in[l][:, :o_r + d_rope], _rope_partner(w_rope),
                               jnp.zeros((d, n_in - o_r - 2 * d_rope), F32)], axis=1).astype(BF16)
        wq = mla_w_uq[l].reshape(q_lora, H_C, d_qk)
        zq = jnp.zeros((q_lora, H_C, head_pad), F32)
        wuq = jnp.concatenate([wq, zq], axis=-1).reshape(q_lora, H_C * LANES).astype(BF16)
        wuqp = jnp.concatenate([jnp.zeros((q_lora, H_C, d_nope), F32), _rope_partner(wq[..., d_nope:]), zq],
                               axis=-1).reshape(q_lora, H_C * LANES).astype(BF16)
        wk = mla_w_uk[l].reshape(kv_lora, H_C, d_nope)
        wuk = jnp.concatenate([wk, jnp.zeros((kv_lora, H_C, LANES - d_nope), F32)],
                              axis=-1).reshape(kv_lora, H_C * LANES).astype(BF16)
        wv = mla_w_uv[l].reshape(kv_lora, H_C, d_v)
        wuv = jnp.concatenate([wv, jnp.zeros((kv_lora, H_C, LANES - d_v), F32)],
                              axis=-1).reshape(kv_lora, H_C * LANES).astype(BF16)
        bs_full = jnp.repeat(sgu_b[l].T, d_a // h_a, axis=1)

        ya, gg, xb, q, k, v = _premix(
            xt, mods, norm1_g[l][None], win, sgu_ln_g[l][None], sgu_ln_b[l][None],
            sgu_w[l].astype(BF16), bs_full, mla_q_norm[l][None], wuq, wuqp, mla_kv_norm[l][None],
            wuk, emat, wuv, vone, cosq, sinq, cosk, sink,
            n_lat_tiles=n_lat_tiles, dims=dims)

        wg = jnp.concatenate([_block_diag(lru_w_a[l, 0]), _block_diag(lru_w_x[l, 0]),
                              _block_diag(lru_w_a[l, 1]), _block_diag(lru_w_x[l, 1])], axis=1).astype(BF16)
        bg = jnp.concatenate([lru_b_a[l, 0], lru_b_x[l, 0], lru_b_a[l, 1], lru_b_x[l, 1]])[None]
        yb = _lru(xb, gg, conv_w[l], conv_b[l][None], wg, bg, lru_lam[l], n_lat=n_lat_tiles, n_ctx=n_ctx_tiles)

        n_tiles = n_lat_tiles if last else n_lat_tiles + n_ctx_tiles
        yc = _attention(q, k, v, l_lat=l_lat, d_v=d_v, ctx_queries=not last)

        if l % 2 == 0:
            i = l // 2
            x1, h2 = _postmix(xt, mods, ya, yb, yc, w_out[l].astype(BF16), norm2_g[l][None], None,
                              n_lat_tiles=n_lat_tiles, n_tiles=n_tiles, n_experts=n_experts)
            xt = _ffn(x1, h2, mods, ffn_w1[i].astype(BF16), ffn_w3[i].astype(BF16), ffn_w2[i].astype(BF16),
                      n_lat_tiles=n_lat_tiles, n_tiles=n_tiles)
            if last:
                out = _final_norm(xt, final_norm_g[None], n_tiles=n_lat_tiles)
        else:
            i = l // 2
            wr = jnp.concatenate([moe_router[i], jnp.zeros((d, LANES - n_experts), F32)], axis=1)
            x1, h2, route_e, route_g = _postmix(xt, mods, ya, yb, yc, w_out[l].astype(BF16),
                                                norm2_g[l][None], wr,
                                                n_lat_tiles=n_lat_tiles, n_tiles=n_tiles, n_experts=n_experts)
            blk_e, n_used, pos, n_pad, fill_start, fill_len = _route_tables(route_e, n_experts, n_tiles * TM)
            xs = _dispatch(pos, h2, _pad_fill(fill_start, fill_len, n_pad, d, h2.dtype), n_tiles=n_tiles)
            yp = _experts(blk_e, n_used, xs, moe_w1, moe_w3, moe_w2, i)
            out_or_xt = _combine(pos, x1, mods, route_g, yp, final_norm_g[None] if last else None,
                                 n_lat_tiles=n_lat_tiles, n_tiles=n_tiles)
            if last:
                out = out_or_xt
            else:
                xt = out_or_xt
    return out
```
